```python
import math
import jax, jax.numpy as jnp
from jax import lax
import numpy as np

D_MODEL = 1024
BATCH = 4
SEQ = 8192
DEPTH = 4

RET_HEADS = 4
RET_DK = 128
RET_DV = 256
RET_CHUNK = 128
RET_THETA = 10000.0
NSA_HEADS = 16
NSA_GROUPS = 2
HEAD_DIM = 64
ROPE_DIM = HEAD_DIM // 4
ROPE_THETA = 500000.0
CMP_LEN = 32
CMP_STRIDE = 16
CMP_HIDDEN = 4 * HEAD_DIM
SLC_BLOCK = 64
SLC_TOPK = 16
WINDOW = 512
Q_BLOCK = 128
N_EXPERTS = 32
TOP_K = 4
D_FF = D_MODEL
SWIGLU_LIMIT = 7.0
SWIGLU_ALPHA = 1.702
MOE_BLOCK = 256
NORM_EPS = 1e-6

RET_QK = RET_HEADS * RET_DK
RET_V = RET_HEADS * RET_DV
NSA_Q = NSA_HEADS * HEAD_DIM
NSA_KV = NSA_GROUPS * HEAD_DIM
NSA_GATE = NSA_HEADS * 3
IN_SIZES = (RET_QK, RET_QK, RET_V, RET_V, NSA_Q, NSA_KV, NSA_KV, NSA_KV, NSA_KV, NSA_KV, NSA_KV, NSA_GATE, 2 * D_MODEL)
IN_SPLIT = tuple(int(v) for v in np.cumsum(IN_SIZES)[:-1])
IN_TOTAL = int(sum(IN_SIZES))

kernel_name = 'hybrid_retention_nsa_moe'


def rms_norm(x, w):
    xf = x.astype(jnp.float32)
    y = xf * lax.rsqrt(jnp.mean(xf * xf, axis=-1, keepdims=True) + NORM_EPS)
    return (y * w.astype(jnp.float32)).astype(x.dtype)


def rope_tables(pos, rot_dim, theta):
    inv = theta ** (-jnp.arange(0, rot_dim, 2, dtype=jnp.float32) / rot_dim)
    ang = pos.astype(jnp.float32)[:, None] * inv[None, :]
    return jnp.cos(ang), jnp.sin(ang)


def apply_rope(x, cos, sin):
    r = 2 * cos.shape[-1]
    xr, xp = x[..., :r], x[..., r:]
    x1, x2 = xr[..., : r // 2], xr[..., r // 2:]
    c, s = cos.astype(x.dtype), sin.astype(x.dtype)
    return jnp.concatenate([x1 * c - x2 * s, x1 * s + x2 * c, xp], axis=-1)


def masked_softmax(s, mask):
    s = jnp.where(mask, s.astype(jnp.float32), -jnp.inf)
    m = jnp.max(s, axis=-1, keepdims=True)
    m = jnp.where(jnp.isfinite(m), m, 0.0)
    p = jnp.exp(s - m)
    den = jnp.sum(p, axis=-1, keepdims=True)
    return p / jnp.where(den > 0, den, 1.0)


def retention(q, k, v, g, gn_w):
    B, S = q.shape[:2]
    dt = q.dtype
    cos, sin = rope_tables(jnp.arange(S), RET_DK, RET_THETA)
    q = apply_rope(q, cos[:, None], sin[:, None])
    k = apply_rope(k, cos[:, None], sin[:, None]) * (RET_DK ** -0.5)
    C = RET_CHUNK
    NC = S // C
    qc = q.reshape(B, NC, C, RET_HEADS, RET_DK)
    kc = k.reshape(B, NC, C, RET_HEADS, RET_DK)
    vc = v.reshape(B, NC, C, RET_HEADS, RET_DV)
    log_g = jnp.log1p(-jnp.exp2(-5.0 - jnp.arange(RET_HEADS, dtype=jnp.float32)))
    n = jnp.arange(C, dtype=jnp.float32)
    diff = n[:, None] - n[None, :]
    decay_in = jnp.where(diff >= 0, jnp.exp(log_g[:, None, None] * jnp.maximum(diff, 0.0)), 0.0)
    zeta = jnp.exp(log_g[None, :] * (C - 1 - n)[:, None])
    xi = jnp.exp(log_g[None, :] * (n + 1)[:, None])
    chunk_decay = jnp.exp(log_g * C).astype(dt)
    scores = jnp.einsum('bcnhd,bcmhd->bchnm', qc, kc) * decay_in.astype(dt)
    inner = jnp.einsum('bchnm,bcmhe->bcnhe', scores, vc)
    u = jnp.einsum('bcmhd,bcmhe->cbhde', kc * zeta[:, :, None].astype(dt), vc)

    def step(r, u_c):
        return chunk_decay[None, :, None, None] * r + u_c, r

    _, r_prev = lax.scan(step, jnp.zeros_like(u[0]), u)
    cross = jnp.einsum('bcnhd,cbhde->bcnhe', qc, r_prev) * xi[:, :, None].astype(dt)
    o = (inner + cross).reshape(B, S, RET_HEADS, RET_DV).astype(jnp.float32)
    mu = jnp.mean(o, axis=-1, keepdims=True)
    var = jnp.mean(jnp.square(o - mu), axis=-1, keepdims=True)
    o = ((o - mu) * lax.rsqrt(var + NORM_EPS)).reshape(B, S, RET_V) * gn_w.astype(jnp.float32)
    return (jax.nn.silu(g.astype(jnp.float32)) * o).astype(dt)


def nsa(q, kc_tok, vc_tok, ks, vs, kw, vw, gates, qk_w, cmp_pos, cmp_w1, cmp_w2):
    B, S = q.shape[:2]
    dt = q.dtype
    G, Hg, d = NSA_GROUPS, NSA_HEADS // NSA_GROUPS, HEAD_DIM
    cos, sin = rope_tables(jnp.arange(S), ROPE_DIM, ROPE_THETA)
    q = apply_rope(rms_norm(q, qk_w[0]), cos[:, None], sin[:, None])
    ks = apply_rope(rms_norm(ks, qk_w[2]), cos[:, None], sin[:, None])
    kw = apply_rope(rms_norm(kw, qk_w[3]), cos[:, None], sin[:, None])

    ratio = CMP_LEN // CMP_STRIDE
    n_cmp = S // CMP_STRIDE - ratio + 1

    def compress(t, pe, w1, w2):
        c = t.reshape(B, S // CMP_STRIDE, CMP_STRIDE, G, d)
        blocks = jnp.concatenate([c[:, j:j + n_cmp] for j in range(ratio)], axis=2)
        blocks = blocks + pe[None, None, :, None, :]
        blocks = blocks.transpose(0, 3, 1, 2, 4).reshape(B, G, n_cmp, CMP_LEN * d)
        return jax.nn.gelu(blocks @ w1) @ w2

    k_cmp = compress(kc_tok, cmp_pos[0], cmp_w1[0], cmp_w2[0])
    v_cmp = compress(vc_tok, cmp_pos[1], cmp_w1[1], cmp_w2[1])
    cmp_end = jnp.arange(n_cmp) * CMP_STRIDE + CMP_LEN - 1
    cc, cs = rope_tables(cmp_end, ROPE_DIM, ROPE_THETA)
    k_cmp = apply_rope(rms_norm(k_cmp, qk_w[1]), cc, cs)

    n_slc = S // SLC_BLOCK
    n_sel = min(SLC_TOPK, n_slc)
    ci = np.arange(n_cmp)[:, None]
    sj = np.arange(n_slc)[None, :]
    overlap = jnp.asarray(((ci * CMP_STRIDE < (sj + 1) * SLC_BLOCK) & (ci * CMP_STRIDE + CMP_LEN > sj * SLC_BLOCK)).astype(np.float32))
    k_blk = ks.reshape(B, n_slc, SLC_BLOCK, G, d).transpose(0, 3, 1, 2, 4)
    v_blk = vs.reshape(B, n_slc, SLC_BLOCK, G, d).transpose(0, 3, 1, 2, 4)
    gather = jax.vmap(jax.vmap(lambda t, ix: t[ix]))

    kw_pad = jnp.pad(kw, ((0, 0), (WINDOW, 0), (0, 0), (0, 0)))
    vw_pad = jnp.pad(vw, ((0, 0), (WINDOW, 0), (0, 0), (0, 0)))

    nq = S // Q_BLOCK
    q_blocks = q.reshape(B, nq, Q_BLOCK, G, Hg, d).transpose(1, 0, 2, 3, 4, 5)
    g_blocks = gates.reshape(B, nq, Q_BLOCK, NSA_HEADS, 3).transpose(1, 0, 2, 3, 4)
    scale = HEAD_DIM ** -0.5
    jj = jnp.arange(n_slc)

    def block_fn(args):
        qb, gb, bi = args
        t = bi * Q_BLOCK + jnp.arange(Q_BLOCK)
        s_c = jnp.einsum('bqghd,bgnd->bghqn', qb, k_cmp) * scale
        p_c = masked_softmax(s_c, cmp_end[None, :] <= t[:, None])
        o_c = jnp.einsum('bghqn,bgnd->bqghd', p_c.astype(dt), v_cmp)
        imp = jnp.einsum('bghqn,nj->bgqj', p_c, overlap)
        cur = t // SLC_BLOCK
        valid = jj[None, :] <= cur[:, None]
        forced = (jj[None, :] == 0) | (jj[None, :] == cur[:, None]) | (jj[None, :] == cur[:, None] - 1)
        imp = jnp.where(forced, jnp.inf, jnp.where(valid, imp, -jnp.inf))
        _, idx = lax.top_k(imp, n_sel)
        ksel = gather(k_blk, idx)
        vsel = gather(v_blk, idx)
        kpos = idx[..., None] * SLC_BLOCK + jnp.arange(SLC_BLOCK)
        m_s = (kpos <= t[:, None, None]).reshape(B, G, 1, Q_BLOCK, n_sel * SLC_BLOCK)
        s_s = jnp.einsum('bqghd,bgqkld->bghqkl', qb, ksel).reshape(B, G, Hg, Q_BLOCK, n_sel * SLC_BLOCK) * scale
        p_s = masked_softmax(s_s, m_s).reshape(B, G, Hg, Q_BLOCK, n_sel, SLC_BLOCK)
        o_s = jnp.einsum('bghqkl,bgqkld->bqghd', p_s.astype(dt), vsel)
        start = bi * Q_BLOCK
        kwb = lax.dynamic_slice_in_dim(kw_pad, start, WINDOW + Q_BLOCK, axis=1)
        vwb = lax.dynamic_slice_in_dim(vw_pad, start, WINDOW + Q_BLOCK, axis=1)
        kp = start - WINDOW + jnp.arange(WINDOW + Q_BLOCK)
        m_w = (kp[None, :] <= t[:, None]) & (kp[None, :] > t[:, None] - WINDOW) & (kp[None, :] >= 0)
        s_w = jnp.einsum('bqghd,bkgd->bghqk', qb, kwb) * scale
        p_w = masked_softmax(s_w, m_w)
        o_w = jnp.einsum('bghqk,bkgd->bqghd', p_w.astype(dt), vwb)
        gg = jax.nn.sigmoid(gb).reshape(B, Q_BLOCK, G, Hg, 3, 1)
        o = gg[..., 0, :] * o_c + gg[..., 1, :] * o_s + gg[..., 2, :] * o_w
        return o.reshape(B, Q_BLOCK, NSA_Q)

    out = lax.map(block_fn, (q_blocks, g_blocks, jnp.arange(nq)))
    return out.transpose(1, 0, 2, 3).reshape(B, S, NSA_Q)


def moe(h, router_w, router_b, w_gu, b_gu, w_d, b_d):
    B, S, D = h.shape
    N = B * S
    xt = h.reshape(N, D)
    logits = (xt @ router_w + router_b).astype(jnp.float32)
    top_v, top_e = lax.top_k(logits, TOP_K)
    wts = jax.nn.softmax(top_v, axis=-1).astype(h.dtype)
    M = N * TOP_K
    flat_e = top_e.reshape(-1)
    flat_tok = jnp.repeat(jnp.arange(N), TOP_K)
    order = jnp.argsort(flat_e)
    se, stok, sw = flat_e[order], flat_tok[order], wts.reshape(-1)[order]
    counts = jnp.bincount(flat_e, length=N_EXPERTS)
    padded = (counts + MOE_BLOCK - 1) // MOE_BLOCK * MOE_BLOCK
    ustart = jnp.cumsum(counts) - counts
    pend = jnp.cumsum(padded)
    pstart = pend - padded
    dest = pstart[se] + (jnp.arange(M) - ustart[se])
    n_blocks = -(-M // MOE_BLOCK) + N_EXPERTS
    rows = n_blocks * MOE_BLOCK
    xbuf = jnp.zeros((rows, D), h.dtype).at[dest].set(xt[stok])
    blk_e = jnp.minimum(jnp.searchsorted(pend, jnp.arange(n_blocks) * MOE_BLOCK, side='right'), N_EXPERTS - 1)

    def expert_block(args):
        xb, e = args
        gu = xb @ w_gu[e] + b_gu[e]
        gate = jnp.minimum(gu[:, :D_FF], SWIGLU_LIMIT)
        up = jnp.clip(gu[:, D_FF:], -SWIGLU_LIMIT, SWIGLU_LIMIT)
        glu = gate * jax.nn.sigmoid(gate * SWIGLU_ALPHA)
        return ((up + 1.0) * glu) @ w_d[e] + b_d[e]

    ybuf = lax.map(expert_block, (xbuf.reshape(n_blocks, MOE_BLOCK, D), blk_e)).reshape(rows, D)
    y = ybuf[dest] * sw[:, None]
    return jax.ops.segment_sum(y, stok, num_segments=N).reshape(B, S, D)


def setup_inputs(seed: int = 0) -> dict:
    key = jax.random.key(seed)
    ks = jax.random.split(key, 18)
    f32 = jnp.float32
    nrm = lambda k, shape, s: jax.random.normal(k, shape, f32) * s
    resid = (2 * DEPTH) ** -0.5
    return {
        'x': nrm(ks[0], (BATCH, SEQ, D_MODEL), 1.0),
        'norm1_w': 1.0 + nrm(ks[1], (DEPTH, D_MODEL), 0.02),
        'w_in': nrm(ks[2], (DEPTH, D_MODEL, IN_TOTAL), D_MODEL ** -0.5),
        'ret_gn_w': 1.0 + nrm(ks[3], (DEPTH, RET_V), 0.02),
        'qk_norm_w': 1.0 + nrm(ks[4], (DEPTH, 4, HEAD_DIM), 0.02),
        'cmp_pos': nrm(ks[5], (DEPTH, 2, CMP_LEN, HEAD_DIM), 0.1),
        'cmp_w1': nrm(ks[6], (DEPTH, 2, CMP_LEN * HEAD_DIM, CMP_HIDDEN), (CMP_LEN * HEAD_DIM) ** -0.5),
        'cmp_w2': nrm(ks[7], (DEPTH, 2, CMP_HIDDEN, HEAD_DIM), CMP_HIDDEN ** -0.5),
        'w_o_ret': nrm(ks[8], (DEPTH, RET_V, D_MODEL), RET_V ** -0.5),
        'w_o_nsa': nrm(ks[9], (DEPTH, NSA_Q, D_MODEL), NSA_Q ** -0.5),
        'w_out': nrm(ks[10], (DEPTH, D_MODEL, D_MODEL), D_MODEL ** -0.5 * resid),
        'norm2_w': 1.0 + nrm(ks[11], (DEPTH, D_MODEL), 0.02),
        'router_w': nrm(ks[12], (DEPTH, D_MODEL, N_EXPERTS), D_MODEL ** -0.5),
        'router_b': nrm(ks[13], (DEPTH, N_EXPERTS), 0.01),
        'w_gate_up': nrm(ks[14], (DEPTH, N_EXPERTS, D_MODEL, 2 * D_FF), D_MODEL ** -0.5),
        'b_gate_up': nrm(ks[15], (DEPTH, N_EXPERTS, 2 * D_FF), 0.01),
        'w_down': nrm(ks[16], (DEPTH, N_EXPERTS, D_FF, D_MODEL), D_FF ** -0.5 * resid),
        'b_down': nrm(ks[17], (DEPTH, N_EXPERTS, D_MODEL), 0.01),
    }


def reference(x, norm1_w, w_in, ret_gn_w, qk_norm_w, cmp_pos, cmp_w1, cmp_w2, w_o_ret, w_o_nsa, w_out, norm2_w, router_w, router_b, w_gate_up, b_gate_up, w_down, b_down):
    B, S = x.shape[:2]
    for l in range(DEPTH):
        h = rms_norm(x, norm1_w[l])
        proj = h @ w_in[l]
        (rq, rk, rv, rg, nq_, kc, vc, kslc, vslc, kwin, vwin, ng, mg) = jnp.split(proj, IN_SPLIT, axis=-1)
        ret = retention(rq.reshape(B, S, RET_HEADS, RET_DK), rk.reshape(B, S, RET_HEADS, RET_DK),
                        rv.reshape(B, S, RET_HEADS, RET_DV), rg, ret_gn_w[l])
        kv = lambda t: t.reshape(B, S, NSA_GROUPS, HEAD_DIM)
        att = nsa(nq_.reshape(B, S, NSA_HEADS, HEAD_DIM), kv(kc), kv(vc), kv(kslc), kv(vslc), kv(kwin), kv(vwin),
                  ng.reshape(B, S, NSA_HEADS, 3), qk_norm_w[l], cmp_pos[l], cmp_w1[l], cmp_w2[l])
        gate = jax.nn.sigmoid(mg)
        mix = gate[..., :D_MODEL] * (ret @ w_o_ret[l]) + gate[..., D_MODEL:] * (att @ w_o_nsa[l])
        x = x + mix @ w_out[l]
        x = x + moe(rms_norm(x, norm2_w[l]), router_w[l], router_b[l], w_gate_up[l], b_gate_up[l], w_down[l], b_down[l])
    return x
```

```python
import functools

import numpy as np
import jax
import jax.numpy as jnp
from jax import lax
from jax.experimental import pallas as pl
from jax.experimental.pallas import tpu as pltpu

F32 = jnp.float32
BF16 = jnp.bfloat16
I32 = jnp.int32

D_MODEL = 1024
RET_HEADS, RET_DK, RET_DV, RET_CHUNK, RET_THETA = 4, 128, 256, 128, 10000.0
NSA_HEADS, NSA_GROUPS, HEAD_DIM = 16, 2, 64
GROUP_HEADS = NSA_HEADS // NSA_GROUPS
ROPE_DIM, ROPE_THETA = HEAD_DIM // 4, 500000.0
CMP_LEN, CMP_STRIDE, CMP_HIDDEN = 32, 16, 4 * HEAD_DIM
SLC_BLOCK, SLC_TOPK, WINDOW = 64, 16, 512
N_EXPERTS, TOP_K, D_FF = 32, 4, D_MODEL
SWIGLU_LIMIT, SWIGLU_ALPHA = 7.0, 1.702
NORM_EPS = 1e-6
RET_QK, RET_V = RET_HEADS * RET_DK, RET_HEADS * RET_DV
NSA_Q, NSA_KV, NSA_GATE = NSA_HEADS * HEAD_DIM, NSA_GROUPS * HEAD_DIM, NSA_HEADS * 3

LANES = 128
NEG = -1e30
VMEM_LIMIT = 56 * 1024 * 1024

TM_PROJ = 256
TC_RET = 1024
TM_PREP = 512
TQ = 128
TK = 512
WIN_TILES = WINDOW // LANES + 1
TM_MOE = 256
TM_CMB = 256

PA_MG, PA_RG, PA_NQ, PA_NG = 0, 2048, 3072, 4096
PA_W = 4224
PB_RQ, PB_RK, PB_RV, PB_C, PB_KS, PB_VS, PB_KW, PB_VW = 0, 512, 1024, 2048, 2304, 2560, 2816, 3072
PB_W = 3328


def _cparams(sem):
    return pltpu.CompilerParams(dimension_semantics=sem, vmem_limit_bytes=VMEM_LIMIT)


def _dot(a, b):
    return jnp.dot(a, b, preferred_element_type=F32)


def _dot_nt(a, b):
    return lax.dot_general(a, b, (((1,), (1,)), ((), ())), preferred_element_type=F32)


def _split_bf16(x):
    hi = x.astype(BF16)
    lo = (x - hi.astype(F32)).astype(BF16)
    return hi, lo


def _inproj_kernel(x_ref, nw_ref, wa_ref, wb_ref, pa_ref, pb_ref):
    x = x_ref[...]
    h = x * lax.rsqrt(jnp.mean(x * x, axis=-1, keepdims=True) + NORM_EPS) * nw_ref[...]
    h = h.astype(BF16)
    for c in range(0, PA_W, 512):
        w = min(512, PA_W - c)
        pa_ref[:, c:c + w] = _dot(h, wa_ref[:, c:c + w])
    for c in range(0, PB_W, 512):
        w = min(512, PB_W - c)
        pb_ref[:, c:c + w] = _dot(h, wb_ref[:, c:c + w]).astype(BF16)


def _inproj(x2, nw, wa, wb):
    n = x2.shape[0]
    return pl.pallas_call(
        _inproj_kernel,
        out_shape=(jax.ShapeDtypeStruct((n, PA_W), F32), jax.ShapeDtypeStruct((n, PB_W), BF16)),
        grid=(n // TM_PROJ,),
        in_specs=[
            pl.BlockSpec((TM_PROJ, D_MODEL), lambda i: (i, 0)),
            pl.BlockSpec((1, D_MODEL), lambda i: (0, 0)),
            pl.BlockSpec((D_MODEL, PA_W), lambda i: (0, 0)),
            pl.BlockSpec((D_MODEL, PB_W), lambda i: (0, 0)),
        ],
        out_specs=(pl.BlockSpec((TM_PROJ, PA_W), lambda i: (i, 0)),
                   pl.BlockSpec((TM_PROJ, PB_W), lambda i: (i, 0))),
        compiler_params=_cparams(("parallel",)),
        name="inproj",
    )(x2, nw, wa, wb)


def _outproj_kernel(x_ref, ret_ref, att_ref, mg_ref, wr_ref, wn_ref, wo_ref, o_ref):
    mg = mg_ref[...]
    a = jax.nn.sigmoid(mg[:, :D_MODEL]) * _dot(ret_ref[...], wr_ref[...])
    b = jax.nn.sigmoid(mg[:, D_MODEL:]) * _dot(att_ref[...], wn_ref[...])
    o_ref[...] = x_ref[...] + _dot((a + b).astype(BF16), wo_ref[...])


def _outproj(x2, ret, att, pa, wr, wn, wo):
    n = x2.shape[0]
    row = lambda i: (i, 0)
    full = lambda i: (0, 0)
    return pl.pallas_call(
        _outproj_kernel,
        out_shape=jax.ShapeDtypeStruct((n, D_MODEL), F32),
        grid=(n // TM_PROJ,),
        in_specs=[
            pl.BlockSpec((TM_PROJ, D_MODEL), row),
            pl.BlockSpec((TM_PROJ, RET_V), row),
            pl.BlockSpec((TM_PROJ, NSA_Q), row),
            pl.BlockSpec((TM_PROJ, 2 * D_MODEL), lambda i: (i, PA_MG // (2 * D_MODEL))),
            pl.BlockSpec((RET_V, D_MODEL), full),
            pl.BlockSpec((NSA_Q, D_MODEL), full),
            pl.BlockSpec((D_MODEL, D_MODEL), full),
        ],
        out_specs=pl.BlockSpec((TM_PROJ, D_MODEL), row),
        compiler_params=_cparams(("parallel",)),
        name="outproj",
    )(x2, ret, att, pa, wr, wn, wo)


def _ret_consts():
    h = np.arange(RET_HEADS, dtype=np.float32)
    log_g = np.log1p(-np.exp2(-5.0 - h)).astype(np.float32)
    n = np.arange(RET_CHUNK, dtype=np.float32)
    diff = n[:, None] - n[None, :]
    decay_in = np.where(diff >= 0, np.exp(log_g[:, None, None] * np.maximum(diff, 0.0)), 0.0).astype(np.float32)
    zeta = np.exp(log_g[:, None] * (RET_CHUNK - 1 - n)[None, :]).astype(np.float32)
    xi = np.exp(log_g[:, None] * (n + 1)[None, :]).astype(np.float32)
    chunk_decay = np.exp(log_g * RET_CHUNK).astype(np.float32)
    zeta_b = np.broadcast_to(zeta[:, :, None], (RET_HEADS, RET_CHUNK, RET_DK)).copy()
    xi_b = np.broadcast_to(xi[:, :, None], (RET_HEADS, RET_CHUNK, RET_DV)).copy()
    return decay_in, zeta_b, xi_b, [float(v) for v in chunk_decay]


def _rope_half_tables(pos, rot_dim, theta, period):
    inv = theta ** (-jnp.arange(0, rot_dim, 2, dtype=F32) / rot_dim)
    ang = pos.astype(F32)[:, None] * inv[None, :]
    c, s = jnp.cos(ang), jnp.sin(ang)
    half = rot_dim // 2
    p = pos.shape[0]
    one = jnp.ones((p, period - rot_dim), F32)
    zero_h = jnp.zeros((p, half), F32)
    zero_r = jnp.zeros((p, period - rot_dim), F32)
    cos_t = jnp.concatenate([c, c, one], axis=1)
    sm = jnp.concatenate([-s, zero_h, zero_r], axis=1)
    sp = jnp.concatenate([zero_h, s, zero_r], axis=1)
    rep = LANES // period
    return tuple(jnp.tile(t, (1, rep)) for t in (cos_t, sm, sp))


def _ret_kernel(q_ref, k_ref, v_ref, g_ref, cos_ref, sin_ref, dec_ref, zeta_ref, xi_ref, gnw_ref,
                o_ref, state_ref, *, chunk_decay):
    @pl.when(pl.program_id(1) == 0)
    def _():
        state_ref[...] = jnp.zeros_like(state_ref)

    n_chunks = TC_RET // RET_CHUNK
    for h in range(RET_HEADS):
        dec = dec_ref[h]
        zeta = zeta_ref[h]
        xi = xi_ref[h]
        gnw = gnw_ref[:, h * RET_DV:(h + 1) * RET_DV]
        for c in range(n_chunks):
            rows = slice(c * RET_CHUNK, (c + 1) * RET_CHUNK)
            cs = cos_ref[rows, :]
            sn = sin_ref[rows, :]
            q = q_ref[rows, h * RET_DK:(h + 1) * RET_DK].astype(F32)
            k = k_ref[rows, h * RET_DK:(h + 1) * RET_DK].astype(F32)
            q = q * cs + pltpu.roll(q, RET_DK // 2, 1) * sn
            k = (k * cs + pltpu.roll(k, RET_DK // 2, 1) * sn) * (RET_DK ** -0.5)
            v = v_ref[rows, h * RET_DV:(h + 1) * RET_DV]
            qb = q.astype(BF16)
            s = _dot_nt(qb, k.astype(BF16)) * dec
            inner = _dot(s.astype(BF16), v)
            r = state_ref[h]
            cross = _dot(qb, r.astype(BF16)) * xi
            kzt = (k * zeta).T.astype(BF16)
            state_ref[h] = chunk_decay[h] * r + _dot(kzt, v)
            o = inner + cross
            mu = jnp.mean(o, axis=-1, keepdims=True)
            d = o - mu
            var = jnp.mean(d * d, axis=-1, keepdims=True)
            on = d * lax.rsqrt(var + NORM_EPS) * gnw
            g = g_ref[rows, h * RET_DV:(h + 1) * RET_DV]
            o_ref[rows, h * RET_DV:(h + 1) * RET_DV] = (g * jax.nn.sigmoid(g) * on).astype(BF16)


def _retention(pa, pb, gnw, tabs, b, s):
    cos_t, sin_t, dec, zeta_b, xi_b, chunk_decay = tabs
    nt = s // TC_RET
    kern = functools.partial(_ret_kernel, chunk_decay=chunk_decay)
    tok = lambda w, j: pl.BlockSpec((TC_RET, w), lambda bi, si, j=j: (bi * nt + si, j))
    cst3 = lambda shp: pl.BlockSpec(shp, lambda bi, si: (0, 0, 0))
    return pl.pallas_call(
        kern,
        out_shape=jax.ShapeDtypeStruct((b * s, RET_V), BF16),
        grid=(b, nt),
        in_specs=[
            tok(RET_QK, PB_RQ // RET_QK),
            tok(RET_QK, PB_RK // RET_QK),
            tok(RET_V, PB_RV // RET_V),
            tok(RET_V, PA_RG // RET_V),
            pl.BlockSpec((TC_RET, LANES), lambda bi, si: (si, 0)),
            pl.BlockSpec((TC_RET, LANES), lambda bi, si: (si, 0)),
            cst3((RET_HEADS, RET_CHUNK, RET_CHUNK)),
            cst3((RET_HEADS, RET_CHUNK, RET_DK)),
            cst3((RET_HEADS, RET_CHUNK, RET_DV)),
            pl.BlockSpec((1, RET_V), lambda bi, si: (0, 0)),
        ],
        out_specs=pl.BlockSpec((TC_RET, RET_V), lambda bi, si: (bi * nt + si, 0)),
        scratch_shapes=[pltpu.VMEM((RET_HEADS, RET_DK, RET_DV), F32)],
        compiler_params=_cparams(("parallel", "arbitrary")),
        name="retention",
    )(pb, pb, pb, pa, cos_t, sin_t, dec, zeta_b, xi_b, gnw)


def _rope16(x, cs, sm, sp):
    half = ROPE_DIM // 2
    return x * cs + pltpu.roll(x, LANES - half, 1) * sm + pltpu.roll(x, half, 1) * sp


def _prep_kernel(nq_ref, ks_ref, vs_ref, kw_ref, vw_ref, ng_ref, cs_ref, sm_ref, sp_ref,
                 qw_ref, ksw_ref, kww_ref, bd_ref,
                 q_ref, kso_ref, kwo_ref, vst_ref, vwt_ref, gt_ref):
    cs, sm, sp = cs_ref[...], sm_ref[...], sp_ref[...]
    bd = bd_ref[...]
    for p in range(NSA_Q // LANES):
        x = nq_ref[:, p * LANES:(p + 1) * LANES]
        hi, lo = _split_bf16(x * x)
        ms = (_dot(hi, bd) + _dot(lo, bd)) * (1.0 / HEAD_DIM)
        y = x * lax.rsqrt(ms + NORM_EPS) * qw_ref[...]
        y = _rope16(y, cs, sm, sp) * (HEAD_DIM ** -0.5)
        q_ref[:, p * LANES:(p + 1) * LANES] = y.astype(BF16)
    for g in range(NSA_GROUPS):
        sl = slice(g * LANES, (g + 1) * LANES)
        for src, w_ref, dst in ((ks_ref, ksw_ref, kso_ref), (kw_ref, kww_ref, kwo_ref)):
            x = src[:, sl].astype(F32)
            y = x * lax.rsqrt(jnp.mean(x * x, axis=-1, keepdims=True) + NORM_EPS) * w_ref[...]
            dst[:, sl] = _rope16(y, cs, sm, sp).astype(BF16)
        vst_ref[g] = vs_ref[:, sl].astype(F32).T.astype(BF16)
        vt = vw_ref[:, sl].astype(F32).T
        for r in range(TM_PREP // LANES):
            vwt_ref[g, r] = vt[:, r * LANES:(r + 1) * LANES].astype(BF16)
    gt_ref[...] = jax.nn.sigmoid(ng_ref[...]).T


def _nsa_prep(pa, pb, tabs, qw, ksw, kww, bd, b, s):
    cs, sm, sp = tabs
    nt = s // TM_PREP
    tokb = lambda w, j: pl.BlockSpec((TM_PREP, w), lambda bi, si, j=j: (bi * nt + si, j))
    tab = pl.BlockSpec((TM_PREP, LANES), lambda bi, si: (si, 0))
    vec = pl.BlockSpec((1, LANES), lambda bi, si: (0, 0))
    kv_w = NSA_GROUPS * LANES
    out_shape = (
        jax.ShapeDtypeStruct((b, s, NSA_Q), BF16),
        jax.ShapeDtypeStruct((b, s, kv_w), BF16),
        jax.ShapeDtypeStruct((b, s, kv_w), BF16),
        jax.ShapeDtypeStruct((b, NSA_GROUPS, nt, LANES, TM_PREP), BF16),
        jax.ShapeDtypeStruct((b, NSA_GROUPS, s // LANES, LANES, LANES), BF16),
        jax.ShapeDtypeStruct((b, LANES, s), F32),
    )
    out_specs = (
        pl.BlockSpec((None, TM_PREP, NSA_Q), lambda bi, si: (bi, si, 0)),
        pl.BlockSpec((None, TM_PREP, kv_w), lambda bi, si: (bi, si, 0)),
        pl.BlockSpec((None, TM_PREP, kv_w), lambda bi, si: (bi, si, 0)),
        pl.BlockSpec((None, NSA_GROUPS, None, LANES, TM_PREP), lambda bi, si: (bi, 0, si, 0, 0)),
        pl.BlockSpec((None, NSA_GROUPS, TM_PREP // LANES, LANES, LANES), lambda bi, si: (bi, 0, si, 0, 0)),
        pl.BlockSpec((None, LANES, TM_PREP), lambda bi, si: (bi, 0, si)),
    )
    return pl.pallas_call(
        _prep_kernel,
        out_shape=out_shape,
        grid=(b, nt),
        in_specs=[
            tokb(NSA_Q, PA_NQ // NSA_Q),
            tokb(kv_w, PB_KS // kv_w), tokb(kv_w, PB_VS // kv_w),
            tokb(kv_w, PB_KW // kv_w), tokb(kv_w, PB_VW // kv_w),
            tokb(LANES, PA_NG // LANES),
            tab, tab, tab, vec, vec, vec,
            pl.BlockSpec((LANES, LANES), lambda bi, si: (0, 0)),
        ],
        out_specs=out_specs,
        compiler_params=_cparams(("parallel", "parallel")),
        name="nsa_prep",
    )(pa, pb, pb, pb, pb, pa, cs, sm, sp, qw, ksw, kww, bd)


def _compress_kernel(c_ref, w1a_ref, w1b_ref, pe_ref, w2_ref, cs_ref, sm_ref, sp_ref, nw_ref, o_ref, *, is_key):
    c = c_ref[...]
    n = c.shape[0]
    a = _dot(c, w1a_ref[...])
    bb = _dot(c, w1b_ref[...])
    pe_hi, pe_lo = _split_bf16(pe_ref[...])
    w1 = jnp.concatenate([w1a_ref[...], w1b_ref[...]], axis=0)
    pe_term = (_dot(pe_hi, w1) + _dot(pe_lo, w1))[0:1, :]
    hid = a + pltpu.roll(bb, n - 1, 0) + pe_term
    t = hid * (0.7978845608028654 * (1.0 + 0.044715 * hid * hid))
    act = 0.5 * hid * (1.0 + jnp.tanh(t))
    y = _dot(act.astype(BF16), w2_ref[...])
    if is_key:
        y = y * lax.rsqrt(jnp.mean(y * y, axis=-1, keepdims=True) + NORM_EPS) * nw_ref[...]
        o_ref[...] = _rope16(y, cs_ref[...], sm_ref[...], sp_ref[...]).astype(BF16)
    else:
        o_ref[...] = y.T.astype(BF16)


def _compress(cflat, w1a, w1b, pe, w2, tabs, nw, is_key):
    b, g, n, width = cflat.shape
    cs, sm, sp = tabs
    full2 = lambda shp: pl.BlockSpec(shp, lambda bi, gi: (0, 0))
    out_block = (None, None, n, LANES) if is_key else (None, None, LANES, n)
    out_shape = (b, g, n, LANES) if is_key else (b, g, LANES, n)
    return pl.pallas_call(
        functools.partial(_compress_kernel, is_key=is_key),
        out_shape=jax.ShapeDtypeStruct(out_shape, BF16),
        grid=(b, g),
        in_specs=[
            pl.BlockSpec((None, None, n, width), lambda bi, gi: (bi, gi, 0, 0)),
            full2(w1a.shape), full2(w1b.shape), full2(pe.shape), full2(w2.shape),
            full2(cs.shape), full2(sm.shape), full2(sp.shape), full2(nw.shape),
        ],
        out_specs=pl.BlockSpec(out_block, lambda bi, gi: (bi, gi, 0, 0)),
        compiler_params=_cparams(("parallel", "parallel")),
        name="compress_k" if is_key else "compress_v",
    )(cflat, w1a, w1b, pe, w2, cs, sm, sp, nw)


def _attn_kernel(q_ref, kc_ref, vct_ref, ks_ref, vst_ref, kw_ref, vwt_ref, gt_ref, ovt_ref,
                 o_ref, acc_ref, m_ref, l_ref, bias_ref, *, seq):
    gi = pl.program_id(1)
    qi = pl.program_id(2)
    q0 = qi * TQ
    ncols = GROUP_HEADS * TQ
    n_cmp = kc_ref.shape[0]
    n_slc = seq // SLC_BLOCK

    lane = lax.broadcasted_iota(I32, (TQ, LANES), 1)
    low = lane < HEAD_DIM
    qblk = q_ref[...]
    zero = jnp.zeros((TQ, LANES), BF16)
    parts = []
    for h in range(GROUP_HEADS):
        slab = qblk[:, (h // 2) * LANES:(h // 2 + 1) * LANES]
        parts.append(jnp.where(low if h % 2 == 0 else jnp.logical_not(low), slab, zero))
    qs = jnp.concatenate(parts, axis=0)
    t_col = q0 + (lax.broadcasted_iota(I32, (1, ncols), 1) & (TQ - 1))

    st = _dot_nt(kc_ref[...], qs)
    cmp_end = lax.broadcasted_iota(I32, (n_cmp, 1), 0) * CMP_STRIDE + (CMP_LEN - 1)
    st = jnp.where(cmp_end <= t_col, st, NEG)
    m = jnp.max(st, axis=0, keepdims=True)
    m = jnp.where(m > 0.5 * NEG, m, 0.0)
    p = jnp.exp(st - m)
    l = jnp.sum(p, axis=0, keepdims=True)
    p = p * (1.0 / jnp.where(l > 0.0, l, 1.0))
    oc_t = _dot(vct_ref[...], p.astype(BF16))
    psum = p[:, 0:TQ]
    for h in range(1, GROUP_HEADS):
        psum = psum + p[:, h * TQ:(h + 1) * TQ]
    ph, plo = _split_bf16(psum)
    imp_t = _dot(ovt_ref[...], ph) + _dot(ovt_ref[...], plo)

    jrow = lax.broadcasted_iota(I32, (n_slc, TQ), 0)
    tok = q0 + lax.broadcasted_iota(I32, (n_slc, TQ), 1)
    cur = lax.shift_right_logical(tok, int(np.log2(SLC_BLOCK)))
    forced = (jrow == 0) | (jrow == cur) | (jrow == cur - 1)
    valid = jrow <= cur
    val = jnp.where(forced, jnp.inf, jnp.where(valid, imp_t, -jnp.inf))
    bias = jnp.full((n_slc, TQ), NEG, F32)
    for _ in range(min(SLC_TOPK, n_slc)):
        mx = jnp.max(val, axis=0, keepdims=True)
        first = jnp.min(jnp.where(val == mx, jrow, n_slc), axis=0, keepdims=True)
        pick = jrow == first
        bias = jnp.where(pick & valid, 0.0, bias)
        val = jnp.where(pick, -jnp.inf, val)
    bias_ref[...] = bias

    m_ref[...] = jnp.full(m_ref.shape, NEG, F32)
    l_ref[...] = jnp.zeros(l_ref.shape, F32)
    acc_ref[...] = jnp.zeros(acc_ref.shape, F32)
    blocks_per_tile = TK // SLC_BLOCK

    def sel_tile(ti, causal):
        st = _dot_nt(ks_ref[pl.ds(pl.multiple_of(ti * TK, TK), TK), :], qs)
        rows = [jnp.broadcast_to(bias_ref[pl.ds(ti * blocks_per_tile + r, 1), :], (SLC_BLOCK, TQ))
                for r in range(blocks_per_tile)]
        bias = jnp.concatenate(rows, axis=0)
        st = st + jnp.concatenate([bias] * GROUP_HEADS, axis=1)
        if causal:
            kpos = ti * TK + lax.broadcasted_iota(I32, (TK, 1), 0)
            st = jnp.where(kpos <= t_col, st, NEG)
        m_old = m_ref[0:1, :]
        m_new = jnp.maximum(m_old, jnp.max(st, axis=0, keepdims=True))
        alpha = jnp.exp(m_old - m_new)
        p = jnp.exp(st - m_new)
        l_ref[0:1, :] = alpha * l_ref[0:1, :] + jnp.sum(p, axis=0, keepdims=True)
        acc_ref[...] = alpha * acc_ref[...] + _dot(vst_ref[ti], p.astype(BF16))
        m_ref[0:1, :] = m_new

    n_full = lax.shift_right_logical(q0, int(np.log2(TK)))

    def body(ti, carry):
        sel_tile(ti, False)
        return carry

    lax.fori_loop(0, n_full, body, 0)
    sel_tile(n_full, True)
    os_t = acc_ref[...] * (1.0 / l_ref[0:1, :])

    w0 = jnp.maximum(q0 - WINDOW, 0)
    nwin = WIN_TILES * LANES
    st = _dot_nt(kw_ref[pl.ds(pl.multiple_of(w0, LANES), nwin), :], qs)
    kpos = w0 + lax.broadcasted_iota(I32, (nwin, 1), 0)
    st = jnp.where((kpos <= t_col) & (kpos > t_col - WINDOW), st, NEG)
    m = jnp.max(st, axis=0, keepdims=True)
    p = jnp.exp(st - m)
    l = jnp.sum(p, axis=0, keepdims=True)
    pb = p.astype(BF16)
    wt0 = lax.shift_right_logical(w0, int(np.log2(LANES)))
    ow_t = _dot(vwt_ref[wt0], pb[0:LANES, :])
    for r in range(1, WIN_TILES):
        ow_t = ow_t + _dot(vwt_ref[wt0 + r], pb[r * LANES:(r + 1) * LANES, :])
    ow_t = ow_t * (1.0 / l)

    outs = []
    for h in range(GROUP_HEADS):
        cols = slice(h * TQ, (h + 1) * TQ)
        gbase = (gi * GROUP_HEADS + h) * 3
        o_t = (gt_ref[pl.ds(gbase, 1), :] * oc_t[:, cols]
               + gt_ref[pl.ds(gbase + 1, 1), :] * os_t[:, cols]
               + gt_ref[pl.ds(gbase + 2, 1), :] * ow_t[:, cols])
        outs.append(o_t.T)
    for pr in range(GROUP_HEADS // 2):
        o_ref[:, pr * LANES:(pr + 1) * LANES] = jnp.where(low, outs[2 * pr], outs[2 * pr + 1]).astype(BF16)


def _attention(q, kc, vct, ks, vst, kw, vwt, gt, ovt, b, s):
    nq = s // TQ
    n_cmp = kc.shape[2]
    n_slc = s // SLC_BLOCK
    gw = GROUP_HEADS * HEAD_DIM
    return pl.pallas_call(
        functools.partial(_attn_kernel, seq=s),
        out_shape=jax.ShapeDtypeStruct((b, s, NSA_Q), BF16),
        grid=(b, NSA_GROUPS, nq),
        in_specs=[
            pl.BlockSpec((None, TQ, gw), lambda bi, gi, qi: (bi, qi, gi)),
            pl.BlockSpec((None, None, n_cmp, LANES), lambda bi, gi, qi: (bi, gi, 0, 0)),
            pl.BlockSpec((None, None, LANES, n_cmp), lambda bi, gi, qi: (bi, gi, 0, 0)),
            pl.BlockSpec((None, s, LANES), lambda bi, gi, qi: (bi, 0, gi)),
            pl.BlockSpec((None, None, s // TK, LANES, TK), lambda bi, gi, qi: (bi, gi, 0, 0, 0)),
            pl.BlockSpec((None, s, LANES), lambda bi, gi, qi: (bi, 0, gi)),
            pl.BlockSpec((None, None, s // LANES, LANES, LANES), lambda bi, gi, qi: (bi, gi, 0, 0, 0)),
            pl.BlockSpec((None, LANES, TQ), lambda bi, gi, qi: (bi, 0, qi)),
            pl.BlockSpec((n_slc, n_cmp), lambda bi, gi, qi: (0, 0)),
        ],
        out_specs=pl.BlockSpec((None, TQ, gw), lambda bi, gi, qi: (bi, qi, gi)),
        scratch_shapes=[
            pltpu.VMEM((LANES, GROUP_HEADS * TQ), F32),
            pltpu.VMEM((8, GROUP_HEADS * TQ), F32),
            pltpu.VMEM((8, GROUP_HEADS * TQ), F32),
            pltpu.VMEM((n_slc, TQ), F32),
        ],
        compiler_params=_cparams(("parallel", "parallel", "arbitrary")),
        name="nsa_attention",
    )(q, kc, vct, ks, vst, kw, vwt, gt, ovt)


def _router_kernel(x_ref, nw_ref, wt_ref, b_ref, h_ref, e_ref, w_ref):
    x = x_ref[...]
    h = x * lax.rsqrt(jnp.mean(x * x, axis=-1, keepdims=True) + NORM_EPS) * nw_ref[...]
    h_ref[...] = h
    h_hi, h_lo = _split_bf16(h)
    w_hi, w_lo = _split_bf16(wt_ref[...])
    lg = _dot_nt(w_hi, h_hi) + _dot_nt(w_hi, h_lo) + _dot_nt(w_lo, h_hi) + b_ref[...]
    erow = lax.broadcasted_iota(I32, lg.shape, 0)
    vals, idxs = [], []
    for _ in range(TOP_K):
        mx = jnp.max(lg, axis=0, keepdims=True)
        first = jnp.min(jnp.where(lg == mx, erow, N_EXPERTS), axis=0, keepdims=True)
        vals.append(mx)
        idxs.append(first)
        lg = jnp.where(erow == first, -jnp.inf, lg)
    ex = [jnp.exp(v - vals[0]) for v in vals]
    den = ex[0] + ex[1] + ex[2] + ex[3]
    pad_i = [jnp.zeros_like(idxs[0])] * (8 - TOP_K)
    pad_f = [jnp.zeros_like(den)] * (8 - TOP_K)
    e_ref[...] = jnp.concatenate(idxs + pad_i, axis=0)
    w_ref[...] = jnp.concatenate([e / den for e in ex] + pad_f, axis=0)


def _router(x2, nw, wt, bcol):
    n = x2.shape[0]
    return pl.pallas_call(
        _router_kernel,
        out_shape=(jax.ShapeDtypeStruct((n, D_MODEL), F32),
                   jax.ShapeDtypeStruct((8, n), I32),
                   jax.ShapeDtypeStruct((8, n), F32)),
        grid=(n // TM_PROJ,),
        in_specs=[
            pl.BlockSpec((TM_PROJ, D_MODEL), lambda i: (i, 0)),
            pl.BlockSpec((1, D_MODEL), lambda i: (0, 0)),
            pl.BlockSpec((N_EXPERTS, D_MODEL), lambda i: (0, 0)),
            pl.BlockSpec((N_EXPERTS, 1), lambda i: (0, 0)),
        ],
        out_specs=(pl.BlockSpec((TM_PROJ, D_MODEL), lambda i: (i, 0)),
                   pl.BlockSpec((8, TM_PROJ), lambda i: (0, i)),
                   pl.BlockSpec((8, TM_PROJ), lambda i: (0, i))),
        compiler_params=_cparams(("parallel",)),
        name="router",
    )(x2, nw, wt, bcol)


def _row_copy(src_hbm, idx, dst_vmem, row, sem):
    return pltpu.make_async_copy(src_hbm.at[pl.ds(idx, 1), :], dst_vmem.at[pl.ds(row, 1), :], sem)


def _expert_kernel(blk_e_ref, tok_ref, h_hbm, wgu_ref, bgu_ref, wd_ref, bd_ref, y_ref, xbuf, sem):
    def start(r, c):
        _row_copy(h_hbm, tok_ref[0, 0, r], xbuf, r, sem).start()
        return c

    lax.fori_loop(0, TM_MOE, start, 0)

    def wait(r, c):
        _row_copy(h_hbm, 0, xbuf, r, sem).wait()
        return c

    lax.fori_loop(0, TM_MOE, wait, 0)
    xb = xbuf[...].astype(BF16)
    gu = _dot(xb, wgu_ref[...]) + bgu_ref[...]
    gate = jnp.minimum(gu[:, :D_FF], SWIGLU_LIMIT)
    up = jnp.clip(gu[:, D_FF:], -SWIGLU_LIMIT, SWIGLU_LIMIT)
    glu = gate * jax.nn.sigmoid(gate * SWIGLU_ALPHA)
    y_ref[...] = _dot(((up + 1.0) * glu).astype(BF16), wd_ref[...]) + bd_ref[...]


def _experts(blk_e, row_tok, h2, wgu, bgu, wd, bd):
    n_blocks = blk_e.shape[0]
    grid_spec = pltpu.PrefetchScalarGridSpec(
        num_scalar_prefetch=1,
        grid=(n_blocks,),
        in_specs=[
            pl.BlockSpec((1, 1, TM_MOE), lambda i, be: (i, 0, 0), memory_space=pltpu.SMEM),
            pl.BlockSpec(memory_space=pl.ANY),
            pl.BlockSpec((None, D_MODEL, 2 * D_FF), lambda i, be: (be[i], 0, 0)),
            pl.BlockSpec((None, 1, 2 * D_FF), lambda i, be: (be[i], 0, 0)),
            pl.BlockSpec((None, D_FF, D_MODEL), lambda i, be: (be[i], 0, 0)),
            pl.BlockSpec((None, 1, D_MODEL), lambda i, be: (be[i], 0, 0)),
        ],
        out_specs=pl.BlockSpec((TM_MOE, D_MODEL), lambda i, be: (i, 0)),
        scratch_shapes=[pltpu.VMEM((TM_MOE, D_MODEL), F32), pltpu.SemaphoreType.DMA(())],
    )
    return pl.pallas_call(
        _expert_kernel,
        out_shape=jax.ShapeDtypeStruct((n_blocks * TM_MOE, D_MODEL), F32),
        grid_spec=grid_spec,
        compiler_params=_cparams(("arbitrary",)),
        name="experts",
    )(blk_e, row_tok, h2, wgu, bgu, wd, bd)


def _combine_kernel(dest_ref, x_ref, w_ref, y_hbm, o_ref, ybuf, sem):
    nrow = TOP_K * TM_CMB

    def start(r, c):
        _row_copy(y_hbm, dest_ref[0, 0, r], ybuf, r, sem).start()
        return c

    lax.fori_loop(0, nrow, start, 0)

    def wait(r, c):
        _row_copy(y_hbm, 0, ybuf, r, sem).wait()
        return c

    lax.fori_loop(0, nrow, wait, 0)
    acc = x_ref[...]
    w = w_ref[...]
    for k in range(TOP_K):
        acc = acc + w[:, k:k + 1] * ybuf[k * TM_CMB:(k + 1) * TM_CMB, :]
    o_ref[...] = acc


def _combine(dest, x2, wts, ybuf):
    n = x2.shape[0]
    return pl.pallas_call(
        _combine_kernel,
        out_shape=jax.ShapeDtypeStruct((n, D_MODEL), F32),
        grid=(n // TM_CMB,),
        in_specs=[
            pl.BlockSpec((1, 1, TOP_K * TM_CMB), lambda i: (i, 0, 0), memory_space=pltpu.SMEM),
            pl.BlockSpec((TM_CMB, D_MODEL), lambda i: (i, 0)),
            pl.BlockSpec((TM_CMB, TOP_K), lambda i: (i, 0)),
            pl.BlockSpec(memory_space=pl.ANY),
        ],
        out_specs=pl.BlockSpec((TM_CMB, D_MODEL), lambda i: (i, 0)),
        scratch_shapes=[pltpu.VMEM((TOP_K * TM_CMB, D_MODEL), F32), pltpu.SemaphoreType.DMA(())],
        compiler_params=_cparams(("arbitrary",)),
        name="combine",
    )(dest, x2, wts, ybuf)


def _route(top_e, n):
    m = n * TOP_K
    flat_e = top_e.reshape(-1)
    order = jnp.argsort(flat_e, stable=True)
    se = flat_e[order]
    counts = jnp.bincount(flat_e, length=N_EXPERTS)
    padded = (counts + TM_MOE - 1) // TM_MOE * TM_MOE
    ustart = jnp.cumsum(counts) - counts
    pend = jnp.cumsum(padded)
    pstart = pend - padded
    dest_sorted = (pstart[se] + (jnp.arange(m) - ustart[se])).astype(I32)
    n_blocks = m // TM_MOE + N_EXPERTS
    rows = n_blocks * TM_MOE
    row_tok = jnp.zeros((rows,), I32).at[dest_sorted].set((order // TOP_K).astype(I32))
    dest = jnp.zeros((m,), I32).at[order].set(dest_sorted)
    blk_e = jnp.minimum(jnp.searchsorted(pend, jnp.arange(n_blocks) * TM_MOE, side="right"),
                        N_EXPERTS - 1).astype(I32)
    return blk_e, row_tok.reshape(n_blocks, 1, TM_MOE), dest


def _layer_weights(l, w_in, norm1_w, ret_gn_w, qk_norm_w, cmp_pos, cmp_w1, cmp_w2, w_o_ret, w_o_nsa, w_out,
                   norm2_w, router_w, router_b, w_gate_up, b_gate_up, w_down, b_down):
    w = w_in[l]
    o = np.cumsum((0, RET_QK, RET_QK, RET_V, RET_V, NSA_Q) + (NSA_KV,) * 6 + (NSA_GATE, 2 * D_MODEL))
    rq, rk, rv, rg, nq = (w[:, o[i]:o[i + 1]] for i in range(5))
    kc, vc, ksl, vsl, kwi, vwi = (w[:, o[5 + i]:o[6 + i]] for i in range(6))
    ng, mg = w[:, o[11]:o[12]], w[:, o[12]:o[13]]

    def dup(t):
        t = t.reshape(D_MODEL, NSA_GROUPS, 1, HEAD_DIM)
        return jnp.broadcast_to(t, (D_MODEL, NSA_GROUPS, 2, HEAD_DIM)).reshape(D_MODEL, NSA_GROUPS * LANES)

    ng_pad = jnp.pad(ng, ((0, 0), (0, LANES - NSA_GATE)))
    wa = jnp.concatenate([mg, rg, nq, ng_pad], axis=1).astype(BF16)
    wb = jnp.concatenate([rq, rk, rv, kc, vc, dup(ksl), dup(vsl), dup(kwi), dup(vwi)], axis=1).astype(BF16)
    qk = qk_norm_w[l]
    tile = lambda v: jnp.tile(v, LANES // HEAD_DIM)[None, :]
    w1 = cmp_w1[l].astype(BF16)
    half = CMP_STRIDE * HEAD_DIM
    w2 = jnp.concatenate([cmp_w2[l], cmp_w2[l]], axis=-1).astype(BF16)
    pe = jnp.broadcast_to(cmp_pos[l].reshape(2, 1, CMP_LEN * HEAD_DIM), (2, 8, CMP_LEN * HEAD_DIM))
    return dict(
        nw1=norm1_w[l][None, :], wa=wa, wb=wb, gnw=ret_gn_w[l][None, :],
        qw=tile(qk[0]), kcw=tile(qk[1]), ksw=tile(qk[2]), kww=tile(qk[3]),
        w1a=w1[:, :half], w1b=w1[:, half:], w2=w2, pe=pe,
        wr=w_o_ret[l].astype(BF16), wn=w_o_nsa[l].astype(BF16), wo=w_out[l].astype(BF16),
        nw2=norm2_w[l][None, :], rwt=router_w[l].T, rb=router_b[l][:, None],
        wgu=w_gate_up[l].astype(BF16), bgu=b_gate_up[l][:, None, :],
        wd=w_down[l].astype(BF16), bd=b_down[l][:, None, :],
    )


def kernel(x, norm1_w, w_in, ret_gn_w, qk_norm_w, cmp_pos, cmp_w1, cmp_w2, w_o_ret, w_o_nsa, w_out, norm2_w,
           router_w, router_b, w_gate_up, b_gate_up, w_down, b_down):
    b, s, _ = x.shape
    depth = w_in.shape[0]
    n = b * s
    assert s % TC_RET == 0 and s % TK == 0 and s >= WIN_TILES * LANES and n % TM_PROJ == 0

    pos = jnp.arange(s)
    inv = RET_THETA ** (-jnp.arange(0, RET_DK, 2, dtype=F32) / RET_DK)
    ang = pos.astype(F32)[:, None] * inv[None, :]
    ret_cos = jnp.concatenate([jnp.cos(ang), jnp.cos(ang)], axis=1)
    ret_sin = jnp.concatenate([-jnp.sin(ang), jnp.sin(ang)], axis=1)
    dec, zeta_b, xi_b, chunk_decay = _ret_consts()
    ret_tabs = (ret_cos, ret_sin, jnp.asarray(dec), jnp.asarray(zeta_b), jnp.asarray(xi_b), chunk_decay)
    tok_tabs = _rope_half_tables(pos, ROPE_DIM, ROPE_THETA, HEAD_DIM)
    n_cmp = s // CMP_STRIDE
    cmp_tabs = _rope_half_tables(jnp.arange(n_cmp) * CMP_STRIDE + CMP_LEN - 1, ROPE_DIM, ROPE_THETA, HEAD_DIM)
    li = np.arange(LANES)
    bd = jnp.asarray((li[:, None] // HEAD_DIM == li[None, :] // HEAD_DIM).astype(np.float32), BF16)
    n_slc = s // SLC_BLOCK
    ci, sj = np.arange(n_cmp)[None, :], np.arange(n_slc)[:, None]
    ovt = ((ci * CMP_STRIDE < (sj + 1) * SLC_BLOCK) & (ci * CMP_STRIDE + CMP_LEN > sj * SLC_BLOCK)
           & (ci < n_cmp - CMP_LEN // CMP_STRIDE + 1))
    ovt = jnp.asarray(ovt.astype(np.float32), BF16)

    x2 = x.reshape(n, D_MODEL)
    for l in range(depth):
        p = _layer_weights(l, w_in, norm1_w, ret_gn_w, qk_norm_w, cmp_pos, cmp_w1, cmp_w2, w_o_ret, w_o_nsa,
                           w_out, norm2_w, router_w, router_b, w_gate_up, b_gate_up, w_down, b_down)
        pa, pb = _inproj(x2, p["nw1"], p["wa"], p["wb"])
        ret = _retention(pa, pb, p["gnw"], ret_tabs, b, s)
        q, ks, kw, vst, vwt, gt = _nsa_prep(pa, pb, tok_tabs, p["qw"], p["ksw"], p["kww"], bd, b, s)
        c = pb[:, PB_C:PB_C + 2 * NSA_KV].reshape(b, n_cmp, CMP_STRIDE, 2, NSA_GROUPS, HEAD_DIM)
        c = c.transpose(3, 0, 4, 1, 2, 5).reshape(2, b, NSA_GROUPS, n_cmp, CMP_STRIDE * HEAD_DIM)
        kcmp = _compress(c[0], p["w1a"][0], p["w1b"][0], p["pe"][0], p["w2"][0], cmp_tabs, p["kcw"], True)
        vcmp_t = _compress(c[1], p["w1a"][1], p["w1b"][1], p["pe"][1], p["w2"][1], cmp_tabs, p["kcw"], False)
        att = _attention(q, kcmp, vcmp_t, ks, vst, kw, vwt, gt, ovt, b, s)
        x2 = _outproj(x2, ret, att.reshape(n, NSA_Q), pa, p["wr"], p["wn"], p["wo"])
        h2, e_t, w_t = _router(x2, p["nw2"], p["rwt"], p["rb"])
        blk_e, row_tok, dest = _route(e_t[:TOP_K].T, n)
        ybuf = _experts(blk_e, row_tok, h2, p["wgu"], p["bgu"], p["wd"], p["bd"])
        dest_b = dest.reshape(n // TM_CMB, TM_CMB, TOP_K).transpose(0, 2, 1).reshape(n // TM_CMB, 1, TOP_K * TM_CMB)
        x2 = _combine(dest_b, x2, w_t[:TOP_K].T, ybuf)
    return x2.reshape(b, s, D_MODEL)
```

```python
import functools

import numpy as np
import jax
import jax.numpy as jnp
from jax import lax
from jax.experimental import pallas as pl
from jax.experimental.pallas import tpu as pltpu

F32 = jnp.float32
BF16 = jnp.bfloat16
I32 = jnp.int32

D_MODEL = 1024
RET_HEADS, RET_DK, RET_DV, RET_CHUNK, RET_THETA = 4, 128, 256, 128, 10000.0
NSA_HEADS, NSA_GROUPS, HEAD_DIM = 16, 2, 64
GROUP_HEADS = NSA_HEADS // NSA_GROUPS
ROPE_DIM, ROPE_THETA = HEAD_DIM // 4, 500000.0
CMP_LEN, CMP_STRIDE, CMP_HIDDEN = 32, 16, 4 * HEAD_DIM
SLC_BLOCK, SLC_TOPK, WINDOW = 64, 16, 512
N_EXPERTS, TOP_K, D_FF = 32, 4, D_MODEL
SWIGLU_LIMIT, SWIGLU_ALPHA = 7.0, 1.702
NORM_EPS = 1e-6
RET_QK, RET_V = RET_HEADS * RET_DK, RET_HEADS * RET_DV
NSA_Q, NSA_KV, NSA_GATE = NSA_HEADS * HEAD_DIM, NSA_GROUPS * HEAD_DIM, NSA_HEADS * 3

LANES = 128
NEG = -1e30
LOG2E = 1.4426950408889634
VMEM_LIMIT = 56 * 1024 * 1024

TM_PROJ = 256
TC_RET = 1024
TM_PREP = 512
TQ = 128
TK = 512
WIN_TILES = WINDOW // LANES + 1
VROWS = HEAD_DIM + 16
TM_MOE = 512
TM_CMB = 256

PA_MG, PA_RG, PA_NQ, PA_NG = 0, 2048, 3072, 4096
PA_W = 4224
PB_RQ, PB_RK, PB_RV, PB_C, PB_KS, PB_KW, PB_VS, PB_VW = 0, 512, 1024, 2048, 2304, 2560, 2816, 2944
PB_W = 3072


def _cparams(sem):
    return pltpu.CompilerParams(dimension_semantics=sem, vmem_limit_bytes=VMEM_LIMIT)


def _dot(a, b):
    return jnp.dot(a, b, preferred_element_type=F32)


def _dot_nt(a, b):
    return lax.dot_general(a, b, (((1,), (1,)), ((), ())), preferred_element_type=F32)


def _split_bf16(x):
    hi = x.astype(BF16)
    lo = (x - hi.astype(F32)).astype(BF16)
    return hi, lo


def _inproj_kernel(x_ref, nw_ref, wa_ref, wb_ref, pa_ref, pb_ref):
    x = x_ref[...]
    h = x * lax.rsqrt(jnp.mean(x * x, axis=-1, keepdims=True) + NORM_EPS) * nw_ref[...]
    h = h.astype(BF16)
    for c in range(0, PA_W, 512):
        w = min(512, PA_W - c)
        pa_ref[:, c:c + w] = _dot(h, wa_ref[:, c:c + w])
    for c in range(0, PB_W, 512):
        w = min(512, PB_W - c)
        pb_ref[:, c:c + w] = _dot(h, wb_ref[:, c:c + w]).astype(BF16)


def _inproj(x2, nw, wa, wb):
    n = x2.shape[0]
    return pl.pallas_call(
        _inproj_kernel,
        out_shape=(jax.ShapeDtypeStruct((n, PA_W), F32), jax.ShapeDtypeStruct((n, PB_W), BF16)),
        grid=(n // TM_PROJ,),
        in_specs=[
            pl.BlockSpec((TM_PROJ, D_MODEL), lambda i: (i, 0)),
            pl.BlockSpec((1, D_MODEL), lambda i: (0, 0)),
            pl.BlockSpec((D_MODEL, PA_W), lambda i: (0, 0)),
            pl.BlockSpec((D_MODEL, PB_W), lambda i: (0, 0)),
        ],
        out_specs=(pl.BlockSpec((TM_PROJ, PA_W), lambda i: (i, 0)),
                   pl.BlockSpec((TM_PROJ, PB_W), lambda i: (i, 0))),
        compiler_params=_cparams(("parallel",)),
        name="inproj",
    )(x2, nw, wa, wb)


def _outproj_kernel(x_ref, ret_ref, att_ref, mg_ref, wr_ref, wn_ref, wo_ref, o_ref):
    mg = mg_ref[...]
    a = jax.nn.sigmoid(mg[:, :D_MODEL]) * _dot(ret_ref[...], wr_ref[...])
    b = jax.nn.sigmoid(mg[:, D_MODEL:]) * _dot(att_ref[...], wn_ref[...])
    o_ref[...] = x_ref[...] + _dot((a + b).astype(BF16), wo_ref[...])


def _outproj(x2, ret, att, pa, wr, wn, wo):
    n = x2.shape[0]
    row = lambda i: (i, 0)
    full = lambda i: (0, 0)
    return pl.pallas_call(
        _outproj_kernel,
        out_shape=jax.ShapeDtypeStruct((n, D_MODEL), F32),
        grid=(n // TM_PROJ,),
        in_specs=[
            pl.BlockSpec((TM_PROJ, D_MODEL), row),
            pl.BlockSpec((TM_PROJ, RET_V), row),
            pl.BlockSpec((TM_PROJ, NSA_Q), row),
            pl.BlockSpec((TM_PROJ, 2 * D_MODEL), lambda i: (i, PA_MG // (2 * D_MODEL))),
            pl.BlockSpec((RET_V, D_MODEL), full),
            pl.BlockSpec((NSA_Q, D_MODEL), full),
            pl.BlockSpec((D_MODEL, D_MODEL), full),
        ],
        out_specs=pl.BlockSpec((TM_PROJ, D_MODEL), row),
        compiler_params=_cparams(("parallel",)),
        name="outproj",
    )(x2, ret, att, pa, wr, wn, wo)


def _ret_consts():
    h = np.arange(RET_HEADS, dtype=np.float32)
    log_g = np.log1p(-np.exp2(-5.0 - h)).astype(np.float32)
    n = np.arange(RET_CHUNK, dtype=np.float32)
    diff = n[:, None] - n[None, :]
    decay_in = np.where(diff >= 0, np.exp(log_g[:, None, None] * np.maximum(diff, 0.0)), 0.0).astype(np.float32)
    zeta = np.exp(log_g[:, None] * (RET_CHUNK - 1 - n)[None, :]).astype(np.float32)
    xi = np.exp(log_g[:, None] * (n + 1)[None, :]).astype(np.float32)
    chunk_decay = np.exp(log_g * RET_CHUNK).astype(np.float32)
    zeta_b = np.broadcast_to(zeta[:, :, None], (RET_HEADS, RET_CHUNK, RET_DK)).copy()
    xi_b = np.broadcast_to(xi[:, :, None], (RET_HEADS, RET_CHUNK, RET_DV)).copy()
    return decay_in, zeta_b, xi_b, [float(v) for v in chunk_decay]


def _rope_half_tables(pos, rot_dim, theta, period):
    inv = theta ** (-jnp.arange(0, rot_dim, 2, dtype=F32) / rot_dim)
    ang = pos.astype(F32)[:, None] * inv[None, :]
    c, s = jnp.cos(ang), jnp.sin(ang)
    half = rot_dim // 2
    p = pos.shape[0]
    one = jnp.ones((p, period - rot_dim), F32)
    zero_h = jnp.zeros((p, half), F32)
    zero_r = jnp.zeros((p, period - rot_dim), F32)
    cos_t = jnp.concatenate([c, c, one], axis=1)
    sm = jnp.concatenate([-s, zero_h, zero_r], axis=1)
    sp = jnp.concatenate([zero_h, s, zero_r], axis=1)
    rep = LANES // period
    return tuple(jnp.tile(t, (1, rep)) for t in (cos_t, sm, sp))


def _ret_kernel(q_ref, k_ref, v_ref, g_ref, cos_ref, sin_ref, dec_ref, zeta_ref, xi_ref, gnw_ref,
                o_ref, state_ref, *, chunk_decay):
    @pl.when(pl.program_id(1) == 0)
    def _():
        state_ref[...] = jnp.zeros_like(state_ref)

    n_chunks = TC_RET // RET_CHUNK
    for h in range(RET_HEADS):
        dec = dec_ref[h]
        zeta = zeta_ref[h]
        xi = xi_ref[h]
        gnw = gnw_ref[:, h * RET_DV:(h + 1) * RET_DV]
        for c in range(n_chunks):
            rows = slice(c * RET_CHUNK, (c + 1) * RET_CHUNK)
            cs = cos_ref[rows, :]
            sn = sin_ref[rows, :]
            q = q_ref[rows, h * RET_DK:(h + 1) * RET_DK].astype(F32)
            k = k_ref[rows, h * RET_DK:(h + 1) * RET_DK].astype(F32)
            q = q * cs + pltpu.roll(q, RET_DK // 2, 1) * sn
            k = (k * cs + pltpu.roll(k, RET_DK // 2, 1) * sn) * (RET_DK ** -0.5)
            v = v_ref[rows, h * RET_DV:(h + 1) * RET_DV]
            qb = q.astype(BF16)
            s = _dot_nt(qb, k.astype(BF16)) * dec
            inner = _dot(s.astype(BF16), v)
            r = state_ref[h]
            cross = _dot(qb, r.astype(BF16)) * xi
            kzt = (k * zeta).T.astype(BF16)
            state_ref[h] = chunk_decay[h] * r + _dot(kzt, v)
            o = inner + cross
            mu = jnp.mean(o, axis=-1, keepdims=True)
            d = o - mu
            var = jnp.mean(d * d, axis=-1, keepdims=True)
            on = d * lax.rsqrt(var + NORM_EPS) * gnw
            g = g_ref[rows, h * RET_DV:(h + 1) * RET_DV]
            o_ref[rows, h * RET_DV:(h + 1) * RET_DV] = (g * jax.nn.sigmoid(g) * on).astype(BF16)


def _retention(pa, pb, gnw, tabs, b, s):
    cos_t, sin_t, dec, zeta_b, xi_b, chunk_decay = tabs
    nt = s // TC_RET
    kern = functools.partial(_ret_kernel, chunk_decay=chunk_decay)
    tok = lambda w, j: pl.BlockSpec((TC_RET, w), lambda bi, si, j=j: (bi * nt + si, j))
    cst3 = lambda shp: pl.BlockSpec(shp, lambda bi, si: (0, 0, 0))
    return pl.pallas_call(
        kern,
        out_shape=jax.ShapeDtypeStruct((b * s, RET_V), BF16),
        grid=(b, nt),
        in_specs=[
            tok(RET_QK, PB_RQ // RET_QK),
            tok(RET_QK, PB_RK // RET_QK),
            tok(RET_V, PB_RV // RET_V),
            tok(RET_V, PA_RG // RET_V),
            pl.BlockSpec((TC_RET, LANES), lambda bi, si: (si, 0)),
            pl.BlockSpec((TC_RET, LANES), lambda bi, si: (si, 0)),
            cst3((RET_HEADS, RET_CHUNK, RET_CHUNK)),
            cst3((RET_HEADS, RET_CHUNK, RET_DK)),
            cst3((RET_HEADS, RET_CHUNK, RET_DV)),
            pl.BlockSpec((1, RET_V), lambda bi, si: (0, 0)),
        ],
        out_specs=pl.BlockSpec((TC_RET, RET_V), lambda bi, si: (bi * nt + si, 0)),
        scratch_shapes=[pltpu.VMEM((RET_HEADS, RET_DK, RET_DV), F32)],
        compiler_params=_cparams(("parallel", "arbitrary")),
        name="retention",
    )(pb, pb, pb, pa, cos_t, sin_t, dec, zeta_b, xi_b, gnw)


def _ones_row_pad(n):
    pad_rows = 16
    return jnp.where(lax.broadcasted_iota(I32, (pad_rows, n), 0) == 0, 1.0, 0.0)


def _rope16(x, cs, sm, sp):
    half = ROPE_DIM // 2
    return x * cs + pltpu.roll(x, LANES - half, 1) * sm + pltpu.roll(x, half, 1) * sp


def _prep_kernel(nq_ref, ks_ref, kw_ref, vs_ref, vw_ref, ng_ref, cs_ref, sm_ref, sp_ref,
                 qw_ref, ksw_ref, kww_ref, bd_ref,
                 q_ref, ksa_ref, kwo_ref, vst_ref, vwt_ref, gt_ref):
    cs, sm, sp = cs_ref[...], sm_ref[...], sp_ref[...]
    bd = bd_ref[...]
    for p in range(NSA_Q // LANES):
        x = nq_ref[:, p * LANES:(p + 1) * LANES]
        hi, lo = _split_bf16(x * x)
        ms = (_dot(hi, bd) + _dot(lo, bd)) * (1.0 / HEAD_DIM)
        y = x * lax.rsqrt(ms + NORM_EPS) * qw_ref[...]
        y = _rope16(y, cs, sm, sp) * (HEAD_DIM ** -0.5 * LOG2E)
        q_ref[:, p * LANES:(p + 1) * LANES] = y.astype(BF16)
    tok = pl.program_id(1) * TM_PREP + lax.broadcasted_iota(I32, (TM_PREP, LANES), 0)
    blk = lax.shift_right_logical(tok, int(np.log2(SLC_BLOCK)))
    onehot = jnp.where(lax.broadcasted_iota(I32, (TM_PREP, LANES), 1) == blk, 1.0, 0.0).astype(BF16)
    for g in range(NSA_GROUPS):
        sl = slice(g * LANES, (g + 1) * LANES)
        x = ks_ref[:, sl].astype(F32)
        y = x * lax.rsqrt(jnp.mean(x * x, axis=-1, keepdims=True) + NORM_EPS) * ksw_ref[...]
        ksa_ref[:, 2 * g * LANES:(2 * g + 1) * LANES] = _rope16(y, cs, sm, sp).astype(BF16)
        ksa_ref[:, (2 * g + 1) * LANES:(2 * g + 2) * LANES] = onehot
        x = kw_ref[:, sl].astype(F32)
        y = x * lax.rsqrt(jnp.mean(x * x, axis=-1, keepdims=True) + NORM_EPS) * kww_ref[...]
        kwo_ref[:, sl] = _rope16(y, cs, sm, sp).astype(BF16)
    vt = vs_ref[...].astype(F32).T
    wt = vw_ref[...].astype(F32).T
    for g in range(NSA_GROUPS):
        rows = slice(g * HEAD_DIM, (g + 1) * HEAD_DIM)
        vst_ref[g, 0:HEAD_DIM, :] = vt[rows, :].astype(BF16)
        vst_ref[g, HEAD_DIM:VROWS, :] = _ones_row_pad(TM_PREP).astype(BF16)
        for r in range(TM_PREP // LANES):
            vwt_ref[g, r, 0:HEAD_DIM, :] = wt[rows, r * LANES:(r + 1) * LANES].astype(BF16)
            vwt_ref[g, r, HEAD_DIM:VROWS, :] = _ones_row_pad(LANES).astype(BF16)
    gt_ref[...] = jax.nn.sigmoid(ng_ref[...]).T


def _nsa_prep(pa, pb, tabs, qw, ksw, kww, bd, b, s):
    cs, sm, sp = tabs
    nt = s // TM_PREP
    tokb = lambda w, j: pl.BlockSpec((TM_PREP, w), lambda bi, si, j=j: (bi * nt + si, j))
    tab = pl.BlockSpec((TM_PREP, LANES), lambda bi, si: (si, 0))
    vec = pl.BlockSpec((1, LANES), lambda bi, si: (0, 0))
    kv_w = NSA_GROUPS * LANES
    out_shape = (
        jax.ShapeDtypeStruct((b, s, NSA_Q), BF16),
        jax.ShapeDtypeStruct((b, s, 2 * kv_w), BF16),
        jax.ShapeDtypeStruct((b, s, kv_w), BF16),
        jax.ShapeDtypeStruct((b, NSA_GROUPS, nt, VROWS, TM_PREP), BF16),
        jax.ShapeDtypeStruct((b, NSA_GROUPS, s // LANES, VROWS, LANES), BF16),
        jax.ShapeDtypeStruct((b, LANES, s), F32),
    )
    out_specs = (
        pl.BlockSpec((None, TM_PREP, NSA_Q), lambda bi, si: (bi, si, 0)),
        pl.BlockSpec((None, TM_PREP, 2 * kv_w), lambda bi, si: (bi, si, 0)),
        pl.BlockSpec((None, TM_PREP, kv_w), lambda bi, si: (bi, si, 0)),
        pl.BlockSpec((None, NSA_GROUPS, None, VROWS, TM_PREP), lambda bi, si: (bi, 0, si, 0, 0)),
        pl.BlockSpec((None, NSA_GROUPS, TM_PREP // LANES, VROWS, LANES), lambda bi, si: (bi, 0, si, 0, 0)),
        pl.BlockSpec((None, LANES, TM_PREP), lambda bi, si: (bi, 0, si)),
    )
    return pl.pallas_call(
        _prep_kernel,
        out_shape=out_shape,
        grid=(b, nt),
        in_specs=[
            tokb(NSA_Q, PA_NQ // NSA_Q),
            tokb(kv_w, PB_KS // kv_w), tokb(kv_w, PB_KW // kv_w),
            tokb(NSA_KV, PB_VS // NSA_KV), tokb(NSA_KV, PB_VW // NSA_KV),
            tokb(LANES, PA_NG // LANES),
            tab, tab, tab, vec, vec, vec,
            pl.BlockSpec((LANES, LANES), lambda bi, si: (0, 0)),
        ],
        out_specs=out_specs,
        compiler_params=_cparams(("parallel", "parallel")),
        name="nsa_prep",
    )(pa, pb, pb, pb, pb, pa, cs, sm, sp, qw, ksw, kww, bd)


def _compress_kernel(c_ref, w1a_ref, w1b_ref, pe_ref, w2_ref, cs_ref, sm_ref, sp_ref, nw_ref, o_ref, *, is_key):
    c = c_ref[...]
    n = c.shape[0]
    a = _dot(c, w1a_ref[...])
    bb = _dot(c, w1b_ref[...])
    pe_hi, pe_lo = _split_bf16(pe_ref[...])
    w1 = jnp.concatenate([w1a_ref[...], w1b_ref[...]], axis=0)
    pe_term = (_dot(pe_hi, w1) + _dot(pe_lo, w1))[0:1, :]
    hid = a + pltpu.roll(bb, n - 1, 0) + pe_term
    t = hid * (0.7978845608028654 * (1.0 + 0.044715 * hid * hid))
    act = 0.5 * hid * (1.0 + jnp.tanh(t))
    y = _dot(act.astype(BF16), w2_ref[...])
    if is_key:
        y = y * lax.rsqrt(jnp.mean(y * y, axis=-1, keepdims=True) + NORM_EPS) * nw_ref[...]
        o_ref[...] = _rope16(y, cs_ref[...], sm_ref[...], sp_ref[...]).astype(BF16)
    else:
        o_ref[...] = jnp.concatenate([y.T[0:HEAD_DIM, :], _ones_row_pad(n)], axis=0).astype(BF16)


def _compress(cflat, w1a, w1b, pe, w2, tabs, nw, is_key):
    b, g, n, width = cflat.shape
    cs, sm, sp = tabs
    full2 = lambda shp: pl.BlockSpec(shp, lambda bi, gi: (0, 0))
    out_block = (None, None, n, LANES) if is_key else (None, None, VROWS, n)
    out_shape = (b, g, n, LANES) if is_key else (b, g, VROWS, n)
    return pl.pallas_call(
        functools.partial(_compress_kernel, is_key=is_key),
        out_shape=jax.ShapeDtypeStruct(out_shape, BF16),
        grid=(b, g),
        in_specs=[
            pl.BlockSpec((None, None, n, width), lambda bi, gi: (bi, gi, 0, 0)),
            full2(w1a.shape), full2(w1b.shape), full2(pe.shape), full2(w2.shape),
            full2(cs.shape), full2(sm.shape), full2(sp.shape), full2(nw.shape),
        ],
        out_specs=pl.BlockSpec(out_block, lambda bi, gi: (bi, gi, 0, 0)),
        compiler_params=_cparams(("parallel", "parallel")),
        name="compress_k" if is_key else "compress_v",
    )(cflat, w1a, w1b, pe, w2, cs, sm, sp, nw)


def _weighted_values(vt, e):
    acc = _dot(vt, e.astype(BF16))
    return acc[0:HEAD_DIM, :], acc[HEAD_DIM:HEAD_DIM + 1, :]


def _attn_kernel(q_ref, kc_ref, vct_ref, ksa_ref, vst_ref, kw_ref, vwt_ref, gt_ref, ovt_ref,
                 o_ref, s_ref, cm_ref, qa_ref):
    gi = pl.program_id(1)
    q0 = pl.program_id(2) * TQ
    n_cmp = kc_ref.shape[0]
    ncols = GROUP_HEADS * TQ

    low = lax.broadcasted_iota(I32, (TQ, LANES), 1) < HEAD_DIM
    zero = jnp.zeros((TQ, LANES), BF16)
    parts = []
    for h in range(GROUP_HEADS):
        slab = q_ref[:, (h // 2) * LANES:(h // 2 + 1) * LANES]
        parts.append(jnp.where(low, slab, zero) if h % 2 == 0 else jnp.where(low, zero, slab))
    qs = jnp.concatenate(parts, axis=0)
    t_row = q0 + lax.broadcasted_iota(I32, (1, TQ), 1)

    def all_heads(bias):
        return jnp.concatenate([bias] * GROUP_HEADS, axis=1)

    cmp_end = lax.broadcasted_iota(I32, (n_cmp, 1), 0) * CMP_STRIDE + (CMP_LEN - 1)
    cbias = jnp.where(cmp_end <= t_row, 0.0, NEG)
    st = _dot_nt(kc_ref[...], qs) + all_heads(cbias)
    m = jnp.max(st, axis=0, keepdims=True)
    e = jnp.exp2(st - jnp.where(m > 0.5 * NEG, m, 0.0))
    oc_all, l = _weighted_values(vct_ref[...], e)
    rl = 1.0 / jnp.where(l > 0.0, l, 1.0)
    oc_all = oc_all * rl
    pn = e * rl
    psum = pn[:, 0:TQ]
    for h in range(1, GROUP_HEADS):
        psum = psum + pn[:, h * TQ:(h + 1) * TQ]
    ph, plo = _split_bf16(psum)
    imp_t = _dot(ovt_ref[...], ph) + _dot(ovt_ref[...], plo)

    w0 = jnp.maximum(q0 - WINDOW, 0)
    nwin = WIN_TILES * LANES
    kpos = w0 + lax.broadcasted_iota(I32, (nwin, 1), 0)
    wbias = jnp.where((kpos <= t_row) & (kpos > t_row - WINDOW), 0.0, NEG)
    st = _dot_nt(kw_ref[pl.ds(pl.multiple_of(w0, LANES), nwin), :], qs) + all_heads(wbias)
    e = jnp.exp2(st - jnp.max(st, axis=0, keepdims=True))
    wt0 = lax.shift_right_logical(w0, int(np.log2(LANES)))
    vwt = jnp.concatenate([vwt_ref[wt0 + r] for r in range(WIN_TILES)], axis=1)
    ow_all, l = _weighted_values(vwt, e)
    ow_all = ow_all * (1.0 / l)

    jrow = lax.broadcasted_iota(I32, (LANES, TQ), 0)
    cur = lax.shift_right_logical(q0 + lax.broadcasted_iota(I32, (LANES, TQ), 1), int(np.log2(SLC_BLOCK)))
    forced = (jrow == 0) | (jrow == cur) | (jrow == cur - 1)
    valid = jrow <= cur
    val = jnp.where(forced, jnp.inf, jnp.where(valid, imp_t, -jnp.inf))
    bias = jnp.full((LANES, TQ), NEG, F32)
    for _ in range(SLC_TOPK):
        mx = jnp.max(val, axis=0, keepdims=True)
        first = jnp.min(jnp.where(val == mx, jrow, LANES), axis=0, keepdims=True)
        pick = jrow == first
        bias = jnp.where(pick & valid, 0.0, bias)
        val = jnp.where(pick, -jnp.inf, val)
    selb = bias.T.astype(BF16)
    qa_ref[...] = jnp.concatenate([qs, jnp.concatenate([selb] * GROUP_HEADS, axis=0)], axis=1)

    def scores(ti, slot):
        st = _dot_nt(ksa_ref[pl.ds(pl.multiple_of(ti * TK, TK), TK), :], qa_ref[...])
        s_ref[slot] = st
        cm_ref[slot] = jnp.broadcast_to(jnp.max(st, axis=0, keepdims=True), (8, ncols))

    def consume(ti, slot, carry, causal):
        m_old, a_old = carry
        st = s_ref[slot]
        if causal:
            kp = ti * TK + lax.broadcasted_iota(I32, (TK, 1), 0)
            st = st + all_heads(jnp.where(kp <= t_row, 0.0, NEG))
            cm = jnp.max(st, axis=0, keepdims=True)
        else:
            cm = cm_ref[slot][0:1, :]
        m_new = jnp.maximum(m_old, cm)
        alpha = jnp.exp2(m_old - m_new)
        e = jnp.exp2(st - m_new)
        return m_new, alpha * a_old + _dot(vst_ref[ti], e.astype(BF16))

    n_full = lax.shift_right_logical(q0, int(np.log2(TK)))
    scores(0, 0)

    def two_tiles(j, carry):
        scores(2 * j + 1, 1)
        carry = consume(2 * j, 0, carry, False)
        scores(2 * j + 2, 0)
        return consume(2 * j + 1, 1, carry, False)

    init = (jnp.full((1, ncols), NEG, F32), jnp.zeros((VROWS, ncols), F32))
    carry = lax.fori_loop(0, lax.shift_right_logical(n_full, 1), two_tiles, init)

    def odd_tail(c):
        scores(n_full, 1)
        return consume(n_full, 1, consume(n_full - 1, 0, c, False), True)

    _, a_s = lax.cond((n_full & 1) == 1, odd_tail, lambda c: consume(n_full, 0, c, True), carry)
    os_all = a_s[0:HEAD_DIM, :] * (1.0 / a_s[HEAD_DIM:HEAD_DIM + 1, :])

    for p in range(GROUP_HEADS // 2):
        halves = []
        for h in (2 * p, 2 * p + 1):
            c = slice(h * TQ, (h + 1) * TQ)
            gbase = (gi * GROUP_HEADS + h) * 3
            halves.append(gt_ref[pl.ds(gbase, 1), :] * oc_all[:, c]
                          + gt_ref[pl.ds(gbase + 1, 1), :] * os_all[:, c]
                          + gt_ref[pl.ds(gbase + 2, 1), :] * ow_all[:, c])
        o_ref[:, p * LANES:(p + 1) * LANES] = jnp.concatenate(halves, axis=0).T.astype(BF16)


def _attention(q, kc, vct, ksa, vst, kw, vwt, gt, ovt, b, s):
    nq = s // TQ
    n_cmp = kc.shape[2]
    gw = GROUP_HEADS * HEAD_DIM
    ncols = GROUP_HEADS * TQ
    return pl.pallas_call(
        _attn_kernel,
        out_shape=jax.ShapeDtypeStruct((b, s, NSA_Q), BF16),
        grid=(b, NSA_GROUPS, nq),
        in_specs=[
            pl.BlockSpec((None, TQ, gw), lambda bi, gi, qi: (bi, qi, gi)),
            pl.BlockSpec((None, None, n_cmp, LANES), lambda bi, gi, qi: (bi, gi, 0, 0)),
            pl.BlockSpec((None, None, VROWS, n_cmp), lambda bi, gi, qi: (bi, gi, 0, 0)),
            pl.BlockSpec((None, s, 2 * LANES), lambda bi, gi, qi: (bi, 0, gi)),
            pl.BlockSpec((None, None, s // TK, VROWS, TK), lambda bi, gi, qi: (bi, gi, 0, 0, 0)),
            pl.BlockSpec((None, s, LANES), lambda bi, gi, qi: (bi, 0, gi)),
            pl.BlockSpec((None, None, s // LANES, VROWS, LANES), lambda bi, gi, qi: (bi, gi, 0, 0, 0)),
            pl.BlockSpec((None, LANES, TQ), lambda bi, gi, qi: (bi, 0, qi)),
            pl.BlockSpec((LANES, n_cmp), lambda bi, gi, qi: (0, 0)),
        ],
        out_specs=pl.BlockSpec((None, TQ, gw), lambda bi, gi, qi: (bi, qi, gi)),
        scratch_shapes=[
            pltpu.VMEM((2, TK, ncols), F32),
            pltpu.VMEM((2, 8, ncols), F32),
            pltpu.VMEM((ncols, 2 * LANES), BF16),
        ],
        compiler_params=_cparams(("parallel", "parallel", "arbitrary")),
        name="nsa_attention",
    )(q, kc, vct, ksa, vst, kw, vwt, gt, ovt)


def _router_kernel(x_ref, nw_ref, wt_ref, b_ref, tri_ref, h_ref, e_ref, w_ref, r_ref, cnt_ref, base_ref):
    @pl.when(pl.program_id(0) == 0)
    def _():
        base_ref[...] = jnp.zeros_like(base_ref)

    x = x_ref[...]
    h = x * lax.rsqrt(jnp.mean(x * x, axis=-1, keepdims=True) + NORM_EPS) * nw_ref[...]
    h_ref[...] = h
    h_hi, h_lo = _split_bf16(h)
    w_hi, w_lo = _split_bf16(wt_ref[...])
    lg = _dot_nt(w_hi, h_hi) + _dot_nt(w_hi, h_lo) + _dot_nt(w_lo, h_hi) + b_ref[...]
    erow = lax.broadcasted_iota(I32, lg.shape, 0)
    vals, hots = [], []
    for _ in range(TOP_K):
        mx = jnp.max(lg, axis=0, keepdims=True)
        first = jnp.min(jnp.where(lg == mx, erow, N_EXPERTS), axis=0, keepdims=True)
        hot = erow == first
        vals.append(mx)
        hots.append(hot)
        lg = jnp.where(hot, -jnp.inf, lg)
    ex = [jnp.exp(v - vals[0]) for v in vals]
    den = ex[0] + ex[1] + ex[2] + ex[3]
    onehots = [h.astype(F32) for h in hots]
    cnt = onehots[0] + onehots[1] + onehots[2] + onehots[3]
    pref = _dot(cnt.astype(BF16), tri_ref[...]) + base_ref[:, 0:1]
    erow_f = erow.astype(F32)
    idxs = [jnp.sum(oh * erow_f, axis=0, keepdims=True) for oh in onehots]
    ranks = [jnp.sum(oh * pref, axis=0, keepdims=True) for oh in onehots]
    base_ref[...] = base_ref[...] + jnp.sum(cnt, axis=1, keepdims=True)
    cnt_ref[...] = base_ref[...]
    pad_f = [jnp.zeros_like(den)] * (8 - TOP_K)
    e_ref[...] = jnp.concatenate(idxs + pad_f, axis=0).astype(I32)
    r_ref[...] = jnp.concatenate(ranks + pad_f, axis=0).astype(I32)
    w_ref[...] = jnp.concatenate([e / den for e in ex] + pad_f, axis=0)


def _router(x2, nw, wt, bcol, tri):
    n = x2.shape[0]
    small = pl.BlockSpec((8, TM_PROJ), lambda i: (0, i))
    return pl.pallas_call(
        _router_kernel,
        out_shape=(jax.ShapeDtypeStruct((n, D_MODEL), F32),
                   jax.ShapeDtypeStruct((8, n), I32),
                   jax.ShapeDtypeStruct((8, n), F32),
                   jax.ShapeDtypeStruct((8, n), I32),
                   jax.ShapeDtypeStruct((N_EXPERTS, LANES), F32)),
        grid=(n // TM_PROJ,),
        in_specs=[
            pl.BlockSpec((TM_PROJ, D_MODEL), lambda i: (i, 0)),
            pl.BlockSpec((1, D_MODEL), lambda i: (0, 0)),
            pl.BlockSpec((N_EXPERTS, D_MODEL), lambda i: (0, 0)),
            pl.BlockSpec((N_EXPERTS, 1), lambda i: (0, 0)),
            pl.BlockSpec((TM_PROJ, TM_PROJ), lambda i: (0, 0)),
        ],
        out_specs=(pl.BlockSpec((TM_PROJ, D_MODEL), lambda i: (i, 0)), small, small, small,
                   pl.BlockSpec((N_EXPERTS, LANES), lambda i: (0, 0))),
        scratch_shapes=[pltpu.VMEM((N_EXPERTS, LANES), F32)],
        compiler_params=_cparams(("arbitrary",)),
        name="router",
    )(x2, nw, wt, bcol, tri)


def _row_copy(src, src_row, dst, dst_row, sem):
    return pltpu.make_async_copy(src.at[pl.ds(src_row, 1), :], dst.at[pl.ds(dst_row, 1), :], sem)


def _dispatch_kernel(zinfo_ref, dest_ref, h_ref, x_hbm, zbuf, sem, zsem, *, n_blocks):
    def zero_block(row):
        return pltpu.make_async_copy(zbuf, x_hbm.at[pl.ds(pl.multiple_of(row, TM_MOE), TM_MOE), :], zsem)

    @pl.when(pl.program_id(0) == 0)
    def _():
        zbuf[...] = jnp.zeros_like(zbuf)
        for e in range(N_EXPERTS):
            zero_block(zinfo_ref[e]).start()
        for e in range(N_EXPERTS):
            zero_block(0).wait()
        nvalid = zinfo_ref[N_EXPERTS]

        def ztail(bk, c):
            zero_block(bk * TM_MOE).start()
            return c

        lax.fori_loop(nvalid, n_blocks, ztail, 0)

        def zwait(bk, c):
            zero_block(0).wait()
            return c

        lax.fori_loop(nvalid, n_blocks, zwait, 0)

    nrow = TOP_K * TM_CMB

    def start(r, c):
        _row_copy(h_ref, r & (TM_CMB - 1), x_hbm, dest_ref[0, 0, r], sem).start()
        return c

    lax.fori_loop(0, nrow, start, 0, unroll=8)

    def wait(r, c):
        _row_copy(h_ref, 0, x_hbm, 0, sem).wait()
        return c

    lax.fori_loop(0, nrow, wait, 0, unroll=8)


def _dispatch(zrow, dest_b, h2, rows):
    n = h2.shape[0]
    grid_spec = pltpu.PrefetchScalarGridSpec(
        num_scalar_prefetch=1,
        grid=(n // TM_CMB,),
        in_specs=[
            pl.BlockSpec((1, 1, TOP_K * TM_CMB), lambda i, z: (i, 0, 0), memory_space=pltpu.SMEM),
            pl.BlockSpec((TM_CMB, D_MODEL), lambda i, z: (i, 0)),
        ],
        out_specs=pl.BlockSpec(memory_space=pl.ANY),
        scratch_shapes=[pltpu.VMEM((TM_MOE, D_MODEL), F32), pltpu.SemaphoreType.DMA(()),
                        pltpu.SemaphoreType.DMA(())],
    )
    return pl.pallas_call(
        functools.partial(_dispatch_kernel, n_blocks=rows // TM_MOE),
        out_shape=jax.ShapeDtypeStruct((rows, D_MODEL), F32),
        grid_spec=grid_spec,
        compiler_params=_cparams(("arbitrary",)),
        name="dispatch",
    )(zrow, dest_b, h2)


def _expert_kernel(blk_e_ref, nvalid_ref, x_ref, wgu_ref, bgu_ref, wd_ref, bd_ref, y_ref):
    @pl.when(pl.program_id(0) >= nvalid_ref[0])
    def _():
        y_ref[...] = jnp.zeros_like(y_ref)

    @pl.when(pl.program_id(0) < nvalid_ref[0])
    def _():
        gu = _dot(x_ref[...].astype(BF16), wgu_ref[...]) + bgu_ref[...]
        gate = jnp.minimum(gu[:, :D_FF], SWIGLU_LIMIT)
        up = jnp.clip(gu[:, D_FF:], -SWIGLU_LIMIT, SWIGLU_LIMIT)
        glu = gate * jax.nn.sigmoid(gate * SWIGLU_ALPHA)
        y_ref[...] = _dot(((up + 1.0) * glu).astype(BF16), wd_ref[...]) + bd_ref[...]


def _experts(blk_e, nvalid, xbuf, wgu, bgu, wd, bd):
    n_blocks = blk_e.shape[0]
    rowblk = lambda i, be, nv: (jnp.minimum(i, nv[0] - 1), 0)
    wblk = lambda i, be, nv: (be[i], 0, 0)
    grid_spec = pltpu.PrefetchScalarGridSpec(
        num_scalar_prefetch=2,
        grid=(n_blocks,),
        in_specs=[
            pl.BlockSpec((TM_MOE, D_MODEL), rowblk),
            pl.BlockSpec((None, D_MODEL, 2 * D_FF), wblk),
            pl.BlockSpec((None, 1, 2 * D_FF), wblk),
            pl.BlockSpec((None, D_FF, D_MODEL), wblk),
            pl.BlockSpec((None, 1, D_MODEL), wblk),
        ],
        out_specs=pl.BlockSpec((TM_MOE, D_MODEL), lambda i, be, nv: (i, 0)),
    )
    return pl.pallas_call(
        _expert_kernel,
        out_shape=jax.ShapeDtypeStruct((n_blocks * TM_MOE, D_MODEL), F32),
        grid_spec=grid_spec,
        compiler_params=_cparams(("arbitrary",)),
        name="experts",
    )(blk_e, nvalid, xbuf, wgu, bgu, wd, bd)


def _combine_kernel(dcur_ref, dnext_ref, x_ref, w_ref, y_hbm, o_ref, ybuf, sem):
    i = pl.program_id(0)
    nrow = TOP_K * TM_CMB
    slot = i & 1

    def issue(d_ref, s):
        def start(r, c):
            _row_copy(y_hbm, d_ref[0, 0, r], ybuf.at[s], r, sem.at[s]).start()
            return c

        lax.fori_loop(0, nrow, start, 0, unroll=8)

    @pl.when(i == 0)
    def _():
        issue(dcur_ref, 0)

    @pl.when(i + 1 < pl.num_programs(0))
    def _():
        issue(dnext_ref, 1 - slot)

    def wait(r, c):
        _row_copy(y_hbm, 0, ybuf.at[slot], r, sem.at[slot]).wait()
        return c

    lax.fori_loop(0, nrow, wait, 0, unroll=8)
    acc = x_ref[...]
    w = w_ref[...]
    for k in range(TOP_K):
        acc = acc + w[:, k:k + 1] * ybuf[slot, k * TM_CMB:(k + 1) * TM_CMB, :]
    o_ref[...] = acc


def _combine(dest_b, x2, wts, ybuf):
    n = x2.shape[0]
    nt = n // TM_CMB
    return pl.pallas_call(
        _combine_kernel,
        out_shape=jax.ShapeDtypeStruct((n, D_MODEL), F32),
        grid=(nt,),
        in_specs=[
            pl.BlockSpec((1, 1, TOP_K * TM_CMB), lambda i: (i, 0, 0), memory_space=pltpu.SMEM),
            pl.BlockSpec((1, 1, TOP_K * TM_CMB), lambda i: (jnp.minimum(i + 1, nt - 1), 0, 0),
                         memory_space=pltpu.SMEM),
            pl.BlockSpec((TM_CMB, D_MODEL), lambda i: (i, 0)),
            pl.BlockSpec((TM_CMB, TOP_K), lambda i: (i, 0)),
            pl.BlockSpec(memory_space=pl.ANY),
        ],
        out_specs=pl.BlockSpec((TM_CMB, D_MODEL), lambda i: (i, 0)),
        scratch_shapes=[pltpu.VMEM((2, TOP_K * TM_CMB, D_MODEL), F32), pltpu.SemaphoreType.DMA((2,))],
        compiler_params=_cparams(("arbitrary",)),
        name="combine",
    )(dest_b, dest_b, x2, wts, ybuf)


def _route(e_t, r_t, cnt, n):
    counts = cnt[:, 0].astype(I32)
    padded = (counts + TM_MOE - 1) // TM_MOE * TM_MOE
    pend = jnp.cumsum(padded)
    pstart = pend - padded
    n_blocks = n * TOP_K // TM_MOE + N_EXPERTS
    rows = n_blocks * TM_MOE
    hit = e_t[:TOP_K, :, None] == jnp.arange(N_EXPERTS, dtype=I32)
    dest = jnp.sum(jnp.where(hit, pstart.astype(I32), 0), axis=-1) + r_t[:TOP_K]
    nt = n // TM_CMB
    dest_b = dest.reshape(TOP_K, nt, TM_CMB).transpose(1, 0, 2).reshape(nt, 1, TOP_K * TM_CMB)
    blk_e = jnp.minimum(jnp.searchsorted(pend, jnp.arange(n_blocks) * TM_MOE, side="right"),
                        N_EXPERTS - 1).astype(I32)
    nvalid = (pend[-1:] // TM_MOE).astype(I32)
    zrow = jnp.clip(pstart + padded - TM_MOE, 0, rows - TM_MOE).astype(I32)
    return dest_b, blk_e, nvalid, jnp.concatenate([zrow, nvalid]), rows


def _layer_weights(l, w_in, norm1_w, ret_gn_w, qk_norm_w, cmp_pos, cmp_w1, cmp_w2, w_o_ret, w_o_nsa, w_out,
                   norm2_w, router_w, router_b, w_gate_up, b_gate_up, w_down, b_down):
    w = w_in[l]
    o = np.cumsum((0, RET_QK, RET_QK, RET_V, RET_V, NSA_Q) + (NSA_KV,) * 6 + (NSA_GATE, 2 * D_MODEL))
    rq, rk, rv, rg, nq = (w[:, o[i]:o[i + 1]] for i in range(5))
    kc, vc, ksl, vsl, kwi, vwi = (w[:, o[5 + i]:o[6 + i]] for i in range(6))
    ng, mg = w[:, o[11]:o[12]], w[:, o[12]:o[13]]

    def dup(t):
        t = t.reshape(D_MODEL, NSA_GROUPS, 1, HEAD_DIM)
        return jnp.broadcast_to(t, (D_MODEL, NSA_GROUPS, 2, HEAD_DIM)).reshape(D_MODEL, NSA_GROUPS * LANES)

    ng_pad = jnp.pad(ng, ((0, 0), (0, LANES - NSA_GATE)))
    wa = jnp.concatenate([mg, rg, nq, ng_pad], axis=1).astype(BF16)
    wb = jnp.concatenate([rq, rk, rv, kc, vc, dup(ksl), dup(kwi), vsl, vwi], axis=1).astype(BF16)
    qk = qk_norm_w[l]
    tile = lambda v: jnp.tile(v, LANES // HEAD_DIM)[None, :]
    w1 = cmp_w1[l].astype(BF16)
    half = CMP_STRIDE * HEAD_DIM
    w2 = jnp.concatenate([cmp_w2[l], cmp_w2[l]], axis=-1).astype(BF16)
    pe = jnp.broadcast_to(cmp_pos[l].reshape(2, 1, CMP_LEN * HEAD_DIM), (2, 8, CMP_LEN * HEAD_DIM))
    return dict(
        nw1=norm1_w[l][None, :], wa=wa, wb=wb, gnw=ret_gn_w[l][None, :],
        qw=tile(qk[0]), kcw=tile(qk[1]), ksw=tile(qk[2]), kww=tile(qk[3]),
        w1a=w1[:, :half], w1b=w1[:, half:], w2=w2, pe=pe,
        wr=w_o_ret[l].astype(BF16), wn=w_o_nsa[l].astype(BF16), wo=w_out[l].astype(BF16),
        nw2=norm2_w[l][None, :], rwt=router_w[l].T, rb=router_b[l][:, None],
        wgu=w_gate_up[l].astype(BF16), bgu=b_gate_up[l][:, None, :],
        wd=w_down[l].astype(BF16), bd=b_down[l][:, None, :],
    )


def kernel(x, norm1_w, w_in, ret_gn_w, qk_norm_w, cmp_pos, cmp_w1, cmp_w2, w_o_ret, w_o_nsa, w_out, norm2_w,
           router_w, router_b, w_gate_up, b_gate_up, w_down, b_down):
    b, s, _ = x.shape
    depth = w_in.shape[0]
    n = b * s
    n_slc = s // SLC_BLOCK
    assert s % TC_RET == 0 and s % TK == 0 and s >= WIN_TILES * LANES and n % TM_PROJ == 0
    assert n_slc <= LANES

    pos = jnp.arange(s)
    inv = RET_THETA ** (-jnp.arange(0, RET_DK, 2, dtype=F32) / RET_DK)
    ang = pos.astype(F32)[:, None] * inv[None, :]
    ret_cos = jnp.concatenate([jnp.cos(ang), jnp.cos(ang)], axis=1)
    ret_sin = jnp.concatenate([-jnp.sin(ang), jnp.sin(ang)], axis=1)
    dec, zeta_b, xi_b, chunk_decay = _ret_consts()
    ret_tabs = (ret_cos, ret_sin, jnp.asarray(dec), jnp.asarray(zeta_b), jnp.asarray(xi_b), chunk_decay)
    tok_tabs = _rope_half_tables(pos, ROPE_DIM, ROPE_THETA, HEAD_DIM)
    n_cmp = s // CMP_STRIDE
    cmp_tabs = _rope_half_tables(jnp.arange(n_cmp) * CMP_STRIDE + CMP_LEN - 1, ROPE_DIM, ROPE_THETA, HEAD_DIM)
    li = np.arange(LANES)
    bd = jnp.asarray((li[:, None] // HEAD_DIM == li[None, :] // HEAD_DIM).astype(np.float32), BF16)
    ci, sj = np.arange(n_cmp)[None, :], np.arange(LANES)[:, None]
    ovt = ((ci * CMP_STRIDE < (sj + 1) * SLC_BLOCK) & (ci * CMP_STRIDE + CMP_LEN > sj * SLC_BLOCK)
           & (ci < n_cmp - CMP_LEN // CMP_STRIDE + 1) & (sj < n_slc))
    ovt = jnp.asarray(ovt.astype(np.float32), BF16)
    ti = np.arange(TM_PROJ)
    tri = jnp.asarray((ti[:, None] < ti[None, :]).astype(np.float32), BF16)

    x2 = x.reshape(n, D_MODEL)
    for l in range(depth):
        p = _layer_weights(l, w_in, norm1_w, ret_gn_w, qk_norm_w, cmp_pos, cmp_w1, cmp_w2, w_o_ret, w_o_nsa,
                           w_out, norm2_w, router_w, router_b, w_gate_up, b_gate_up, w_down, b_down)
        pa, pb = _inproj(x2, p["nw1"], p["wa"], p["wb"])
        ret = _retention(pa, pb, p["gnw"], ret_tabs, b, s)
        q, ksa, kw, vst, vwt, gt = _nsa_prep(pa, pb, tok_tabs, p["qw"], p["ksw"], p["kww"], bd, b, s)
        c = pb[:, PB_C:PB_C + 2 * NSA_KV].reshape(b, n_cmp, CMP_STRIDE, 2, NSA_GROUPS, HEAD_DIM)
        c = c.transpose(3, 0, 4, 1, 2, 5).reshape(2, b, NSA_GROUPS, n_cmp, CMP_STRIDE * HEAD_DIM)
        kcmp = _compress(c[0], p["w1a"][0], p["w1b"][0], p["pe"][0], p["w2"][0], cmp_tabs, p["kcw"], True)
        vcmp_t = _compress(c[1], p["w1a"][1], p["w1b"][1], p["pe"][1], p["w2"][1], cmp_tabs, p["kcw"], False)
        att = _attention(q, kcmp, vcmp_t, ksa, vst, kw, vwt, gt, ovt, b, s)
        x2 = _outproj(x2, ret, att.reshape(n, NSA_Q), pa, p["wr"], p["wn"], p["wo"])
        h2, e_t, w_t, r_t, cnt = _router(x2, p["nw2"], p["rwt"], p["rb"], tri)
        dest_b, blk_e, nvalid, zrow, rows = _route(e_t, r_t, cnt, n)
        xbuf = _dispatch(zrow, dest_b, h2, rows)
        ybuf = _experts(blk_e, nvalid, xbuf, p["wgu"], p["bgu"], p["wd"], p["bd"])
        x2 = _combine(dest_b, x2, w_t[:TOP_K].T, ybuf)
    return x2.reshape(b, s, D_MODEL)
```

```python
import functools

import numpy as np
import jax
import jax.numpy as jnp
from jax import lax
from jax.experimental import pallas as pl
from jax.experimental.pallas import tpu as pltpu

F32 = jnp.float32
BF16 = jnp.bfloat16
I32 = jnp.int32

D_MODEL = 1024
RET_HEADS, RET_DK, RET_DV, RET_CHUNK, RET_THETA = 4, 128, 256, 128, 10000.0
NSA_HEADS, NSA_GROUPS, HEAD_DIM = 16, 2, 64
GROUP_HEADS = NSA_HEADS // NSA_GROUPS
ROPE_DIM, ROPE_THETA = HEAD_DIM // 4, 500000.0
CMP_LEN, CMP_STRIDE, CMP_HIDDEN = 32, 16, 4 * HEAD_DIM
SLC_BLOCK, SLC_TOPK, WINDOW = 64, 16, 512
N_EXPERTS, TOP_K, D_FF = 32, 4, D_MODEL
SWIGLU_LIMIT, SWIGLU_ALPHA = 7.0, 1.702
NORM_EPS = 1e-6
RET_QK, RET_V = RET_HEADS * RET_DK, RET_HEADS * RET_DV
NSA_Q, NSA_KV, NSA_GATE = NSA_HEADS * HEAD_DIM, NSA_GROUPS * HEAD_DIM, NSA_HEADS * 3

LANES = 128
NEG = -1e30
LOG2E = 1.4426950408889634
VMEM_LIMIT = 56 * 1024 * 1024

TM_PROJ = 256
TC_RET = 1024
TM_PREP = 512
TQ = 128
TK = 512
WIN_TILES = WINDOW // LANES + 1
VROWS = HEAD_DIM + 16
TM_MOE = 512
TM_CMB = 256
SUB = 8


def _tiled(nrows):
    return (nrows * SUB, LANES)

PA_MG, PA_RG, PA_NQ, PA_NG = 0, 2048, 3072, 4096
PA_W = 4224
PB_RQ, PB_RK, PB_RV, PB_C, PB_KS, PB_KW, PB_VS, PB_VW = 0, 512, 1024, 2048, 2304, 2560, 2816, 2944
PB_W = 3072


def _cparams(sem):
    return pltpu.CompilerParams(dimension_semantics=sem, vmem_limit_bytes=VMEM_LIMIT)


def _dot(a, b):
    return jnp.dot(a, b, preferred_element_type=F32)


def _dot_nt(a, b):
    return lax.dot_general(a, b, (((1,), (1,)), ((), ())), preferred_element_type=F32)


def _split_bf16(x):
    hi = x.astype(BF16)
    lo = (x - hi.astype(F32)).astype(BF16)
    return hi, lo


def _inproj_kernel(x_ref, nw_ref, wa_ref, wb_ref, pa_ref, pb_ref):
    x = x_ref[...]
    h = x * lax.rsqrt(jnp.mean(x * x, axis=-1, keepdims=True) + NORM_EPS) * nw_ref[...]
    h = h.astype(BF16)
    for c in range(0, PA_W, 512):
        w = min(512, PA_W - c)
        pa_ref[:, c:c + w] = _dot(h, wa_ref[:, c:c + w])
    for c in range(0, PB_W, 512):
        w = min(512, PB_W - c)
        pb_ref[:, c:c + w] = _dot(h, wb_ref[:, c:c + w]).astype(BF16)


def _inproj(x2, nw, wa, wb):
    n = x2.shape[0]
    return pl.pallas_call(
        _inproj_kernel,
        out_shape=(jax.ShapeDtypeStruct((n, PA_W), F32), jax.ShapeDtypeStruct((n, PB_W), BF16)),
        grid=(n // TM_PROJ,),
        in_specs=[
            pl.BlockSpec((TM_PROJ, D_MODEL), lambda i: (i, 0)),
            pl.BlockSpec((1, D_MODEL), lambda i: (0, 0)),
            pl.BlockSpec((D_MODEL, PA_W), lambda i: (0, 0)),
            pl.BlockSpec((D_MODEL, PB_W), lambda i: (0, 0)),
        ],
        out_specs=(pl.BlockSpec((TM_PROJ, PA_W), lambda i: (i, 0)),
                   pl.BlockSpec((TM_PROJ, PB_W), lambda i: (i, 0))),
        compiler_params=_cparams(("parallel",)),
        name="inproj",
    )(x2, nw, wa, wb)


def _outproj_kernel(x_ref, ret_ref, att_ref, mg_ref, wr_ref, wn_ref, wo_ref, o_ref):
    mg = mg_ref[...]
    a = jax.nn.sigmoid(mg[:, :D_MODEL]) * _dot(ret_ref[...], wr_ref[...])
    b = jax.nn.sigmoid(mg[:, D_MODEL:]) * _dot(att_ref[...], wn_ref[...])
    o_ref[...] = x_ref[...] + _dot((a + b).astype(BF16), wo_ref[...])


def _outproj(x2, ret, att, pa, wr, wn, wo):
    n = x2.shape[0]
    row = lambda i: (i, 0)
    full = lambda i: (0, 0)
    return pl.pallas_call(
        _outproj_kernel,
        out_shape=jax.ShapeDtypeStruct((n, D_MODEL), F32),
        grid=(n // TM_PROJ,),
        in_specs=[
            pl.BlockSpec((TM_PROJ, D_MODEL), row),
            pl.BlockSpec((TM_PROJ, RET_V), row),
            pl.BlockSpec((TM_PROJ, NSA_Q), row),
            pl.BlockSpec((TM_PROJ, 2 * D_MODEL), lambda i: (i, PA_MG // (2 * D_MODEL))),
            pl.BlockSpec((RET_V, D_MODEL), full),
            pl.BlockSpec((NSA_Q, D_MODEL), full),
            pl.BlockSpec((D_MODEL, D_MODEL), full),
        ],
        out_specs=pl.BlockSpec((TM_PROJ, D_MODEL), row),
        compiler_params=_cparams(("parallel",)),
        name="outproj",
    )(x2, ret, att, pa, wr, wn, wo)


def _ret_consts():
    h = np.arange(RET_HEADS, dtype=np.float32)
    log_g = np.log1p(-np.exp2(-5.0 - h)).astype(np.float32)
    n = np.arange(RET_CHUNK, dtype=np.float32)
    diff = n[:, None] - n[None, :]
    decay_in = np.where(diff >= 0, np.exp(log_g[:, None, None] * np.maximum(diff, 0.0)), 0.0).astype(np.float32)
    zeta = np.exp(log_g[:, None] * (RET_CHUNK - 1 - n)[None, :]).astype(np.float32)
    xi = np.exp(log_g[:, None] * (n + 1)[None, :]).astype(np.float32)
    chunk_decay = np.exp(log_g * RET_CHUNK).astype(np.float32)
    zeta_b = np.broadcast_to(zeta[:, :, None], (RET_HEADS, RET_CHUNK, RET_DK)).copy()
    xi_b = np.broadcast_to(xi[:, :, None], (RET_HEADS, RET_CHUNK, RET_DV)).copy()
    return decay_in, zeta_b, xi_b, [float(v) for v in chunk_decay]


def _rope_half_tables(pos, rot_dim, theta, period):
    inv = theta ** (-jnp.arange(0, rot_dim, 2, dtype=F32) / rot_dim)
    ang = pos.astype(F32)[:, None] * inv[None, :]
    c, s = jnp.cos(ang), jnp.sin(ang)
    half = rot_dim // 2
    p = pos.shape[0]
    one = jnp.ones((p, period - rot_dim), F32)
    zero_h = jnp.zeros((p, half), F32)
    zero_r = jnp.zeros((p, period - rot_dim), F32)
    cos_t = jnp.concatenate([c, c, one], axis=1)
    sm = jnp.concatenate([-s, zero_h, zero_r], axis=1)
    sp = jnp.concatenate([zero_h, s, zero_r], axis=1)
    rep = LANES // period
    return tuple(jnp.tile(t, (1, rep)) for t in (cos_t, sm, sp))


def _ret_kernel(q_ref, k_ref, v_ref, g_ref, cos_ref, sin_ref, dec_ref, zeta_ref, xi_ref, gnw_ref,
                o_ref, state_ref, *, chunk_decay):
    @pl.when(pl.program_id(1) == 0)
    def _():
        state_ref[...] = jnp.zeros_like(state_ref)

    n_chunks = TC_RET // RET_CHUNK
    for h in range(RET_HEADS):
        dec = dec_ref[h]
        zeta = zeta_ref[h]
        xi = xi_ref[h]
        gnw = gnw_ref[:, h * RET_DV:(h + 1) * RET_DV]
        for c in range(n_chunks):
            rows = slice(c * RET_CHUNK, (c + 1) * RET_CHUNK)
            cs = cos_ref[rows, :]
            sn = sin_ref[rows, :]
            q = q_ref[rows, h * RET_DK:(h + 1) * RET_DK].astype(F32)
            k = k_ref[rows, h * RET_DK:(h + 1) * RET_DK].astype(F32)
            q = q * cs + pltpu.roll(q, RET_DK // 2, 1) * sn
            k = (k * cs + pltpu.roll(k, RET_DK // 2, 1) * sn) * (RET_DK ** -0.5)
            v = v_ref[rows, h * RET_DV:(h + 1) * RET_DV]
            qb = q.astype(BF16)
            s = _dot_nt(qb, k.astype(BF16)) * dec
            inner = _dot(s.astype(BF16), v)
            r = state_ref[h]
            cross = _dot(qb, r.astype(BF16)) * xi
            kzt = (k * zeta).T.astype(BF16)
            state_ref[h] = chunk_decay[h] * r + _dot(kzt, v)
            o = inner + cross
            mu = jnp.mean(o, axis=-1, keepdims=True)
            d = o - mu
            var = jnp.mean(d * d, axis=-1, keepdims=True)
            on = d * lax.rsqrt(var + NORM_EPS) * gnw
            g = g_ref[rows, h * RET_DV:(h + 1) * RET_DV]
            o_ref[rows, h * RET_DV:(h + 1) * RET_DV] = (g * jax.nn.sigmoid(g) * on).astype(BF16)


def _retention(pa, pb, gnw, tabs, b, s):
    cos_t, sin_t, dec, zeta_b, xi_b, chunk_decay = tabs
    nt = s // TC_RET
    kern = functools.partial(_ret_kernel, chunk_decay=chunk_decay)
    tok = lambda w, j: pl.BlockSpec((TC_RET, w), lambda bi, si, j=j: (bi * nt + si, j))
    cst3 = lambda shp: pl.BlockSpec(shp, lambda bi, si: (0, 0, 0))
    return pl.pallas_call(
        kern,
        out_shape=jax.ShapeDtypeStruct((b * s, RET_V), BF16),
        grid=(b, nt),
        in_specs=[
            tok(RET_QK, PB_RQ // RET_QK),
            tok(RET_QK, PB_RK // RET_QK),
            tok(RET_V, PB_RV // RET_V),
            tok(RET_V, PA_RG // RET_V),
            pl.BlockSpec((TC_RET, LANES), lambda bi, si: (si, 0)),
            pl.BlockSpec((TC_RET, LANES), lambda bi, si: (si, 0)),
            cst3((RET_HEADS, RET_CHUNK, RET_CHUNK)),
            cst3((RET_HEADS, RET_CHUNK, RET_DK)),
            cst3((RET_HEADS, RET_CHUNK, RET_DV)),
            pl.BlockSpec((1, RET_V), lambda bi, si: (0, 0)),
        ],
        out_specs=pl.BlockSpec((TC_RET, RET_V), lambda bi, si: (bi * nt + si, 0)),
        scratch_shapes=[pltpu.VMEM((RET_HEADS, RET_DK, RET_DV), F32)],
        compiler_params=_cparams(("parallel", "arbitrary")),
        name="retention",
    )(pb, pb, pb, pa, cos_t, sin_t, dec, zeta_b, xi_b, gnw)


def _ones_row_pad(n):
    pad_rows = 16
    return jnp.where(lax.broadcasted_iota(I32, (pad_rows, n), 0) == 0, 1.0, 0.0)


def _rope16(x, cs, sm, sp):
    half = ROPE_DIM // 2
    return x * cs + pltpu.roll(x, LANES - half, 1) * sm + pltpu.roll(x, half, 1) * sp


def _prep_kernel(nq_ref, ks_ref, kw_ref, vs_ref, vw_ref, ng_ref, cs_ref, sm_ref, sp_ref,
                 qw_ref, ksw_ref, kww_ref, bd_ref,
                 q_ref, ksa_ref, kwo_ref, vst_ref, vwt_ref, gt_ref):
    cs, sm, sp = cs_ref[...], sm_ref[...], sp_ref[...]
    bd = bd_ref[...]
    for p in range(NSA_Q // LANES):
        x = nq_ref[:, p * LANES:(p + 1) * LANES]
        hi, lo = _split_bf16(x * x)
        ms = (_dot(hi, bd) + _dot(lo, bd)) * (1.0 / HEAD_DIM)
        y = x * lax.rsqrt(ms + NORM_EPS) * qw_ref[...]
        y = _rope16(y, cs, sm, sp) * (HEAD_DIM ** -0.5 * LOG2E)
        q_ref[:, p * LANES:(p + 1) * LANES] = y.astype(BF16)
    tok = pl.program_id(1) * TM_PREP + lax.broadcasted_iota(I32, (TM_PREP, LANES), 0)
    blk = lax.shift_right_logical(tok, int(np.log2(SLC_BLOCK)))
    onehot = jnp.where(lax.broadcasted_iota(I32, (TM_PREP, LANES), 1) == blk, 1.0, 0.0).astype(BF16)
    for g in range(NSA_GROUPS):
        sl = slice(g * LANES, (g + 1) * LANES)
        x = ks_ref[:, sl].astype(F32)
        y = x * lax.rsqrt(jnp.mean(x * x, axis=-1, keepdims=True) + NORM_EPS) * ksw_ref[...]
        ksa_ref[:, 2 * g * LANES:(2 * g + 1) * LANES] = _rope16(y, cs, sm, sp).astype(BF16)
        ksa_ref[:, (2 * g + 1) * LANES:(2 * g + 2) * LANES] = onehot
        x = kw_ref[:, sl].astype(F32)
        y = x * lax.rsqrt(jnp.mean(x * x, axis=-1, keepdims=True) + NORM_EPS) * kww_ref[...]
        kwo_ref[:, sl] = _rope16(y, cs, sm, sp).astype(BF16)
    vt = vs_ref[...].astype(F32).T
    wt = vw_ref[...].astype(F32).T
    for g in range(NSA_GROUPS):
        rows = slice(g * HEAD_DIM, (g + 1) * HEAD_DIM)
        vst_ref[g, 0:HEAD_DIM, :] = vt[rows, :].astype(BF16)
        vst_ref[g, HEAD_DIM:VROWS, :] = _ones_row_pad(TM_PREP).astype(BF16)
        for r in range(TM_PREP // LANES):
            vwt_ref[g, r, 0:HEAD_DIM, :] = wt[rows, r * LANES:(r + 1) * LANES].astype(BF16)
            vwt_ref[g, r, HEAD_DIM:VROWS, :] = _ones_row_pad(LANES).astype(BF16)
    gt_ref[...] = jax.nn.sigmoid(ng_ref[...]).T


def _nsa_prep(pa, pb, tabs, qw, ksw, kww, bd, b, s):
    cs, sm, sp = tabs
    nt = s // TM_PREP
    tokb = lambda w, j: pl.BlockSpec((TM_PREP, w), lambda bi, si, j=j: (bi * nt + si, j))
    tab = pl.BlockSpec((TM_PREP, LANES), lambda bi, si: (si, 0))
    vec = pl.BlockSpec((1, LANES), lambda bi, si: (0, 0))
    kv_w = NSA_GROUPS * LANES
    out_shape = (
        jax.ShapeDtypeStruct((b, s, NSA_Q), BF16),
        jax.ShapeDtypeStruct((b, s, 2 * kv_w), BF16),
        jax.ShapeDtypeStruct((b, s, kv_w), BF16),
        jax.ShapeDtypeStruct((b, NSA_GROUPS, nt, VROWS, TM_PREP), BF16),
        jax.ShapeDtypeStruct((b, NSA_GROUPS, s // LANES, VROWS, LANES), BF16),
        jax.ShapeDtypeStruct((b, LANES, s), F32),
    )
    out_specs = (
        pl.BlockSpec((None, TM_PREP, NSA_Q), lambda bi, si: (bi, si, 0)),
        pl.BlockSpec((None, TM_PREP, 2 * kv_w), lambda bi, si: (bi, si, 0)),
        pl.BlockSpec((None, TM_PREP, kv_w), lambda bi, si: (bi, si, 0)),
        pl.BlockSpec((None, NSA_GROUPS, None, VROWS, TM_PREP), lambda bi, si: (bi, 0, si, 0, 0)),
        pl.BlockSpec((None, NSA_GROUPS, TM_PREP // LANES, VROWS, LANES), lambda bi, si: (bi, 0, si, 0, 0)),
        pl.BlockSpec((None, LANES, TM_PREP), lambda bi, si: (bi, 0, si)),
    )
    return pl.pallas_call(
        _prep_kernel,
        out_shape=out_shape,
        grid=(b, nt),
        in_specs=[
            tokb(NSA_Q, PA_NQ // NSA_Q),
            tokb(kv_w, PB_KS // kv_w), tokb(kv_w, PB_KW // kv_w),
            tokb(NSA_KV, PB_VS // NSA_KV), tokb(NSA_KV, PB_VW // NSA_KV),
            tokb(LANES, PA_NG // LANES),
            tab, tab, tab, vec, vec, vec,
            pl.BlockSpec((LANES, LANES), lambda bi, si: (0, 0)),
        ],
        out_specs=out_specs,
        compiler_params=_cparams(("parallel", "parallel")),
        name="nsa_prep",
    )(pa, pb, pb, pb, pb, pa, cs, sm, sp, qw, ksw, kww, bd)


def _compress_kernel(c_ref, w1a_ref, w1b_ref, pe_ref, w2_ref, cs_ref, sm_ref, sp_ref, nw_ref, o_ref, *, is_key):
    c = c_ref[...]
    n = c.shape[0]
    a = _dot(c, w1a_ref[...])
    bb = _dot(c, w1b_ref[...])
    pe_hi, pe_lo = _split_bf16(pe_ref[...])
    w1 = jnp.concatenate([w1a_ref[...], w1b_ref[...]], axis=0)
    pe_term = (_dot(pe_hi, w1) + _dot(pe_lo, w1))[0:1, :]
    hid = a + pltpu.roll(bb, n - 1, 0) + pe_term
    t = hid * (0.7978845608028654 * (1.0 + 0.044715 * hid * hid))
    act = 0.5 * hid * (1.0 + jnp.tanh(t))
    y = _dot(act.astype(BF16), w2_ref[...])
    if is_key:
        y = y * lax.rsqrt(jnp.mean(y * y, axis=-1, keepdims=True) + NORM_EPS) * nw_ref[...]
        o_ref[...] = _rope16(y, cs_ref[...], sm_ref[...], sp_ref[...]).astype(BF16)
    else:
        o_ref[...] = jnp.concatenate([y.T[0:HEAD_DIM, :], _ones_row_pad(n)], axis=0).astype(BF16)


def _compress(cflat, w1a, w1b, pe, w2, tabs, nw, is_key):
    b, g, n, width = cflat.shape
    cs, sm, sp = tabs
    full2 = lambda shp: pl.BlockSpec(shp, lambda bi, gi: (0, 0))
    out_block = (None, None, n, LANES) if is_key else (None, None, VROWS, n)
    out_shape = (b, g, n, LANES) if is_key else (b, g, VROWS, n)
    return pl.pallas_call(
        functools.partial(_compress_kernel, is_key=is_key),
        out_shape=jax.ShapeDtypeStruct(out_shape, BF16),
        grid=(b, g),
        in_specs=[
            pl.BlockSpec((None, None, n, width), lambda bi, gi: (bi, gi, 0, 0)),
            full2(w1a.shape), full2(w1b.shape), full2(pe.shape), full2(w2.shape),
            full2(cs.shape), full2(sm.shape), full2(sp.shape), full2(nw.shape),
        ],
        out_specs=pl.BlockSpec(out_block, lambda bi, gi: (bi, gi, 0, 0)),
        compiler_params=_cparams(("parallel", "parallel")),
        name="compress_k" if is_key else "compress_v",
    )(cflat, w1a, w1b, pe, w2, cs, sm, sp, nw)


def _weighted_values(vt, e):
    acc = _dot(vt, e.astype(BF16))
    return acc[0:HEAD_DIM, :], acc[HEAD_DIM:HEAD_DIM + 1, :]


def _attn_kernel(q_ref, kc_ref, vct_ref, ksa_ref, vst_ref, kw_ref, vwt_ref, gt_ref, ovt_ref,
                 o_ref, s_ref, cm_ref, qa_ref):
    gi = pl.program_id(1)
    q0 = pl.program_id(2) * TQ
    n_cmp = kc_ref.shape[0]
    ncols = GROUP_HEADS * TQ

    low = lax.broadcasted_iota(I32, (TQ, LANES), 1) < HEAD_DIM
    zero = jnp.zeros((TQ, LANES), BF16)
    parts = []
    for h in range(GROUP_HEADS):
        slab = q_ref[:, (h // 2) * LANES:(h // 2 + 1) * LANES]
        parts.append(jnp.where(low, slab, zero) if h % 2 == 0 else jnp.where(low, zero, slab))
    qs = jnp.concatenate(parts, axis=0)
    t_row = q0 + lax.broadcasted_iota(I32, (1, TQ), 1)

    def all_heads(bias):
        return jnp.concatenate([bias] * GROUP_HEADS, axis=1)

    cmp_end = lax.broadcasted_iota(I32, (n_cmp, 1), 0) * CMP_STRIDE + (CMP_LEN - 1)
    cbias = jnp.where(cmp_end <= t_row, 0.0, NEG)
    st = _dot_nt(kc_ref[...], qs) + all_heads(cbias)
    m = jnp.max(st, axis=0, keepdims=True)
    e = jnp.exp2(st - jnp.where(m > 0.5 * NEG, m, 0.0))
    oc_all, l = _weighted_values(vct_ref[...], e)
    rl = 1.0 / jnp.where(l > 0.0, l, 1.0)
    oc_all = oc_all * rl
    pn = e * rl
    psum = pn[:, 0:TQ]
    for h in range(1, GROUP_HEADS):
        psum = psum + pn[:, h * TQ:(h + 1) * TQ]
    ph, plo = _split_bf16(psum)
    imp_t = _dot(ovt_ref[...], ph) + _dot(ovt_ref[...], plo)

    w0 = jnp.maximum(q0 - WINDOW, 0)
    nwin = WIN_TILES * LANES
    kpos = w0 + lax.broadcasted_iota(I32, (nwin, 1), 0)
    wbias = jnp.where((kpos <= t_row) & (kpos > t_row - WINDOW), 0.0, NEG)
    st = _dot_nt(kw_ref[pl.ds(pl.multiple_of(w0, LANES), nwin), :], qs) + all_heads(wbias)
    e = jnp.exp2(st - jnp.max(st, axis=0, keepdims=True))
    wt0 = lax.shift_right_logical(w0, int(np.log2(LANES)))
    vwt = jnp.concatenate([vwt_ref[wt0 + r] for r in range(WIN_TILES)], axis=1)
    ow_all, l = _weighted_values(vwt, e)
    ow_all = ow_all * (1.0 / l)

    jrow = lax.broadcasted_iota(I32, (LANES, TQ), 0)
    cur = lax.shift_right_logical(q0 + lax.broadcasted_iota(I32, (LANES, TQ), 1), int(np.log2(SLC_BLOCK)))
    forced = (jrow == 0) | (jrow == cur) | (jrow == cur - 1)
    valid = jrow <= cur
    val = jnp.where(forced, jnp.inf, jnp.where(valid, imp_t, -jnp.inf))
    bias = jnp.full((LANES, TQ), NEG, F32)
    for _ in range(SLC_TOPK):
        mx = jnp.max(val, axis=0, keepdims=True)
        first = jnp.min(jnp.where(val == mx, jrow, LANES), axis=0, keepdims=True)
        pick = jrow == first
        bias = jnp.where(pick & valid, 0.0, bias)
        val = jnp.where(pick, -jnp.inf, val)
    selb = bias.T.astype(BF16)
    qa_ref[...] = jnp.concatenate([qs, jnp.concatenate([selb] * GROUP_HEADS, axis=0)], axis=1)

    def scores(ti, slot):
        st = _dot_nt(ksa_ref[pl.ds(pl.multiple_of(ti * TK, TK), TK), :], qa_ref[...])
        s_ref[slot] = st
        cm_ref[slot] = jnp.broadcast_to(jnp.max(st, axis=0, keepdims=True), (8, ncols))

    def consume(ti, slot, carry, causal):
        m_old, a_old = carry
        st = s_ref[slot]
        if causal:
            kp = ti * TK + lax.broadcasted_iota(I32, (TK, 1), 0)
            st = st + all_heads(jnp.where(kp <= t_row, 0.0, NEG))
            cm = jnp.max(st, axis=0, keepdims=True)
        else:
            cm = cm_ref[slot][0:1, :]
        m_new = jnp.maximum(m_old, cm)
        alpha = jnp.exp2(m_old - m_new)
        e = jnp.exp2(st - m_new)
        return m_new, alpha * a_old + _dot(vst_ref[ti], e.astype(BF16))

    n_full = lax.shift_right_logical(q0, int(np.log2(TK)))
    scores(0, 0)

    def two_tiles(j, carry):
        scores(2 * j + 1, 1)
        carry = consume(2 * j, 0, carry, False)
        scores(2 * j + 2, 0)
        return consume(2 * j + 1, 1, carry, False)

    init = (jnp.full((1, ncols), NEG, F32), jnp.zeros((VROWS, ncols), F32))
    carry = lax.fori_loop(0, lax.shift_right_logical(n_full, 1), two_tiles, init)

    def odd_tail(c):
        scores(n_full, 1)
        return consume(n_full, 1, consume(n_full - 1, 0, c, False), True)

    _, a_s = lax.cond((n_full & 1) == 1, odd_tail, lambda c: consume(n_full, 0, c, True), carry)
    os_all = a_s[0:HEAD_DIM, :] * (1.0 / a_s[HEAD_DIM:HEAD_DIM + 1, :])

    for p in range(GROUP_HEADS // 2):
        halves = []
        for h in (2 * p, 2 * p + 1):
            c = slice(h * TQ, (h + 1) * TQ)
            gbase = (gi * GROUP_HEADS + h) * 3
            halves.append(gt_ref[pl.ds(gbase, 1), :] * oc_all[:, c]
                          + gt_ref[pl.ds(gbase + 1, 1), :] * os_all[:, c]
                          + gt_ref[pl.ds(gbase + 2, 1), :] * ow_all[:, c])
        o_ref[:, p * LANES:(p + 1) * LANES] = jnp.concatenate(halves, axis=0).T.astype(BF16)


def _attention(q, kc, vct, ksa, vst, kw, vwt, gt, ovt, b, s):
    nq = s // TQ
    n_cmp = kc.shape[2]
    gw = GROUP_HEADS * HEAD_DIM
    ncols = GROUP_HEADS * TQ
    return pl.pallas_call(
        _attn_kernel,
        out_shape=jax.ShapeDtypeStruct((b, s, NSA_Q), BF16),
        grid=(b, NSA_GROUPS, nq),
        in_specs=[
            pl.BlockSpec((None, TQ, gw), lambda bi, gi, qi: (bi, qi, gi)),
            pl.BlockSpec((None, None, n_cmp, LANES), lambda bi, gi, qi: (bi, gi, 0, 0)),
            pl.BlockSpec((None, None, VROWS, n_cmp), lambda bi, gi, qi: (bi, gi, 0, 0)),
            pl.BlockSpec((None, s, 2 * LANES), lambda bi, gi, qi: (bi, 0, gi)),
            pl.BlockSpec((None, None, s // TK, VROWS, TK), lambda bi, gi, qi: (bi, gi, 0, 0, 0)),
            pl.BlockSpec((None, s, LANES), lambda bi, gi, qi: (bi, 0, gi)),
            pl.BlockSpec((None, None, s // LANES, VROWS, LANES), lambda bi, gi, qi: (bi, gi, 0, 0, 0)),
            pl.BlockSpec((None, LANES, TQ), lambda bi, gi, qi: (bi, 0, qi)),
            pl.BlockSpec((LANES, n_cmp), lambda bi, gi, qi: (0, 0)),
        ],
        out_specs=pl.BlockSpec((None, TQ, gw), lambda bi, gi, qi: (bi, qi, gi)),
        scratch_shapes=[
            pltpu.VMEM((2, TK, ncols), F32),
            pltpu.VMEM((2, 8, ncols), F32),
            pltpu.VMEM((ncols, 2 * LANES), BF16),
        ],
        compiler_params=_cparams(("parallel", "parallel", "arbitrary")),
        name="nsa_attention",
    )(q, kc, vct, ksa, vst, kw, vwt, gt, ovt)


def _router_kernel(x_ref, nw_ref, wt_ref, b_ref, tri_ref, h_ref, e_ref, w_ref, r_ref, cnt_ref, base_ref):
    @pl.when(pl.program_id(0) == 0)
    def _():
        base_ref[...] = jnp.zeros_like(base_ref)

    x = x_ref[...]
    h = x * lax.rsqrt(jnp.mean(x * x, axis=-1, keepdims=True) + NORM_EPS) * nw_ref[...]
    _store_row_tiles(h_ref, h)
    h_hi, h_lo = _split_bf16(h)
    w_hi, w_lo = _split_bf16(wt_ref[...])
    lg = _dot_nt(w_hi, h_hi) + _dot_nt(w_hi, h_lo) + _dot_nt(w_lo, h_hi) + b_ref[...]
    erow = lax.broadcasted_iota(I32, lg.shape, 0)
    vals, hots = [], []
    for _ in range(TOP_K):
        mx = jnp.max(lg, axis=0, keepdims=True)
        first = jnp.min(jnp.where(lg == mx, erow, N_EXPERTS), axis=0, keepdims=True)
        hot = erow == first
        vals.append(mx)
        hots.append(hot)
        lg = jnp.where(hot, -jnp.inf, lg)
    ex = [jnp.exp(v - vals[0]) for v in vals]
    den = ex[0] + ex[1] + ex[2] + ex[3]
    onehots = [h.astype(F32) for h in hots]
    cnt = onehots[0] + onehots[1] + onehots[2] + onehots[3]
    pref = _dot(cnt.astype(BF16), tri_ref[...]) + base_ref[:, 0:1]
    erow_f = erow.astype(F32)
    idxs = [jnp.sum(oh * erow_f, axis=0, keepdims=True) for oh in onehots]
    ranks = [jnp.sum(oh * pref, axis=0, keepdims=True) for oh in onehots]
    base_ref[...] = base_ref[...] + jnp.sum(cnt, axis=1, keepdims=True)
    cnt_ref[...] = base_ref[...]
    pad_f = [jnp.zeros_like(den)] * (8 - TOP_K)
    e_ref[...] = jnp.concatenate(idxs + pad_f, axis=0).astype(I32)
    r_ref[...] = jnp.concatenate(ranks + pad_f, axis=0).astype(I32)
    w_ref[...] = jnp.concatenate([e / den for e in ex] + pad_f, axis=0)


def _router(x2, nw, wt, bcol, tri):
    n = x2.shape[0]
    small = pl.BlockSpec((8, TM_PROJ), lambda i: (0, i))
    return pl.pallas_call(
        _router_kernel,
        out_shape=(jax.ShapeDtypeStruct(_tiled(n), F32),
                   jax.ShapeDtypeStruct((8, n), I32),
                   jax.ShapeDtypeStruct((8, n), F32),
                   jax.ShapeDtypeStruct((8, n), I32),
                   jax.ShapeDtypeStruct((N_EXPERTS, LANES), F32)),
        grid=(n // TM_PROJ,),
        in_specs=[
            pl.BlockSpec((TM_PROJ, D_MODEL), lambda i: (i, 0)),
            pl.BlockSpec((1, D_MODEL), lambda i: (0, 0)),
            pl.BlockSpec((N_EXPERTS, D_MODEL), lambda i: (0, 0)),
            pl.BlockSpec((N_EXPERTS, 1), lambda i: (0, 0)),
            pl.BlockSpec((TM_PROJ, TM_PROJ), lambda i: (0, 0)),
        ],
        out_specs=(pl.BlockSpec(_tiled(TM_PROJ), lambda i: (i, 0)), small, small, small,
                   pl.BlockSpec((N_EXPERTS, LANES), lambda i: (0, 0))),
        scratch_shapes=[pltpu.VMEM((N_EXPERTS, LANES), F32)],
        compiler_params=_cparams(("arbitrary",)),
        name="router",
    )(x2, nw, wt, bcol, tri)


def _row_copy(src, src_row, dst, dst_row, sem):
    tile = lambda ref, r: ref.at[pl.ds(pl.multiple_of(r * SUB, SUB), SUB), :]
    return pltpu.make_async_copy(tile(src, src_row), tile(dst, dst_row), sem)


def _issue_rows(nrow, copy_of):
    def body(g, c):
        for j in range(8):
            copy_of(g * 8 + j).start(priority=j % 2)
        return c

    lax.fori_loop(0, nrow // 8, body, 0)


def _wait_rows(nrow, copy_of):
    def body(g, c):
        for j in range(8):
            copy_of(0).wait()
        return c

    lax.fori_loop(0, nrow // 8, body, 0)


def _store_row_tiles(ref, x):
    for c in range(SUB):
        ref[pl.ds(c, x.shape[0], stride=SUB), :] = x[:, c * LANES:(c + 1) * LANES]


def _load_row_tiles(ref, first_row, nrows):
    return jnp.concatenate([ref[pl.ds(first_row * SUB + c, nrows, stride=SUB), :] for c in range(SUB)], axis=1)


def _dispatch_kernel(zinfo_ref, dest_ref, h_ref, x_hbm, zbuf, sem, zsem, *, n_blocks):
    def zero_block(row):
        return pltpu.make_async_copy(
            zbuf, x_hbm.at[pl.ds(pl.multiple_of(row * SUB, TM_MOE * SUB), TM_MOE * SUB), :], zsem)

    @pl.when(pl.program_id(0) == 0)
    def _():
        zbuf[...] = jnp.zeros_like(zbuf)
        for e in range(N_EXPERTS):
            zero_block(zinfo_ref[e]).start()
        for e in range(N_EXPERTS):
            zero_block(0).wait()
        nvalid = zinfo_ref[N_EXPERTS]

        def ztail(bk, c):
            zero_block(bk * TM_MOE).start()
            return c

        lax.fori_loop(nvalid, n_blocks, ztail, 0)

        def zwait(bk, c):
            zero_block(0).wait()
            return c

        lax.fori_loop(nvalid, n_blocks, zwait, 0)

    nrow = TOP_K * TM_CMB
    _issue_rows(nrow, lambda r: _row_copy(h_ref, r & (TM_CMB - 1), x_hbm, dest_ref[0, 0, r], sem))
    _wait_rows(nrow, lambda r: _row_copy(h_ref, 0, x_hbm, 0, sem))


def _dispatch(zinfo, dest_b, h3, rows):
    n = h3.shape[0] // SUB
    grid_spec = pltpu.PrefetchScalarGridSpec(
        num_scalar_prefetch=1,
        grid=(n // TM_CMB,),
        in_specs=[
            pl.BlockSpec((1, 1, TOP_K * TM_CMB), lambda i, z: (i, 0, 0), memory_space=pltpu.SMEM),
            pl.BlockSpec(_tiled(TM_CMB), lambda i, z: (i, 0)),
        ],
        out_specs=pl.BlockSpec(memory_space=pl.ANY),
        scratch_shapes=[pltpu.VMEM(_tiled(TM_MOE), F32), pltpu.SemaphoreType.DMA(()),
                        pltpu.SemaphoreType.DMA(())],
    )
    return pl.pallas_call(
        functools.partial(_dispatch_kernel, n_blocks=rows // TM_MOE),
        out_shape=jax.ShapeDtypeStruct(_tiled(rows), F32),
        grid_spec=grid_spec,
        compiler_params=_cparams(("arbitrary",)),
        name="dispatch",
    )(zinfo, dest_b, h3)


def _expert_kernel(blk_e_ref, nvalid_ref, x_ref, wgu_ref, bgu_ref, wd_ref, bd_ref, y_ref, wgu_bf, wd_bf):
    i = pl.program_id(0)
    used = i < nvalid_ref[0]

    @pl.when(jnp.logical_not(used))
    def _():
        y_ref[...] = jnp.zeros_like(y_ref)

    @pl.when(used & ((i == 0) | (blk_e_ref[i] != blk_e_ref[jnp.maximum(i - 1, 0)])))
    def _():
        wgu_bf[...] = wgu_ref[...].astype(BF16)
        wd_bf[...] = wd_ref[...].astype(BF16)

    @pl.when(used)
    def _():
        gu = _dot(_load_row_tiles(x_ref, 0, TM_MOE).astype(BF16), wgu_bf[...]) + bgu_ref[...]
        gate = jnp.minimum(gu[:, :D_FF], SWIGLU_LIMIT)
        up = jnp.clip(gu[:, D_FF:], -SWIGLU_LIMIT, SWIGLU_LIMIT)
        glu = gate * jax.nn.sigmoid(gate * SWIGLU_ALPHA)
        _store_row_tiles(y_ref, _dot(((up + 1.0) * glu).astype(BF16), wd_bf[...]) + bd_ref[...])


def _experts(blk_e, nvalid, xbuf, wgu, bgu, wd, bd):
    n_blocks = blk_e.shape[0]
    wblk = lambda i, be, nv: (be[i], 0, 0)
    grid_spec = pltpu.PrefetchScalarGridSpec(
        num_scalar_prefetch=2,
        grid=(n_blocks,),
        in_specs=[
            pl.BlockSpec(_tiled(TM_MOE), lambda i, be, nv: (jnp.minimum(i, nv[0] - 1), 0)),
            pl.BlockSpec((None, D_MODEL, 2 * D_FF), wblk),
            pl.BlockSpec((None, 1, 2 * D_FF), wblk),
            pl.BlockSpec((None, D_FF, D_MODEL), wblk),
            pl.BlockSpec((None, 1, D_MODEL), wblk),
        ],
        out_specs=pl.BlockSpec(_tiled(TM_MOE), lambda i, be, nv: (i, 0)),
        scratch_shapes=[pltpu.VMEM((D_MODEL, 2 * D_FF), BF16), pltpu.VMEM((D_FF, D_MODEL), BF16)],
    )
    return pl.pallas_call(
        _expert_kernel,
        out_shape=jax.ShapeDtypeStruct(_tiled(n_blocks * TM_MOE), F32),
        grid_spec=grid_spec,
        compiler_params=_cparams(("arbitrary",)),
        name="experts",
    )(blk_e, nvalid, xbuf, wgu, bgu, wd, bd)


def _combine_kernel(dcur_ref, dnext_ref, x_ref, w_ref, y_hbm, o_ref, ybuf, sem):
    i = pl.program_id(0)
    nrow = TOP_K * TM_CMB
    slot = i & 1

    def issue(d_ref, s):
        _issue_rows(nrow, lambda r: _row_copy(y_hbm, d_ref[0, 0, r], ybuf.at[s], r, sem.at[s]))

    @pl.when(i == 0)
    def _():
        issue(dcur_ref, 0)

    @pl.when(i + 1 < pl.num_programs(0))
    def _():
        issue(dnext_ref, 1 - slot)

    _wait_rows(nrow, lambda r: _row_copy(y_hbm, 0, ybuf.at[slot], 0, sem.at[slot]))
    acc = x_ref[...]
    w = w_ref[...]
    for k in range(TOP_K):
        acc = acc + w[:, k:k + 1] * _load_row_tiles(ybuf.at[slot], k * TM_CMB, TM_CMB)
    o_ref[...] = acc


def _combine(dest_b, x2, wts, ybuf):
    n = x2.shape[0]
    nt = n // TM_CMB
    return pl.pallas_call(
        _combine_kernel,
        out_shape=jax.ShapeDtypeStruct((n, D_MODEL), F32),
        grid=(nt,),
        in_specs=[
            pl.BlockSpec((1, 1, TOP_K * TM_CMB), lambda i: (i, 0, 0), memory_space=pltpu.SMEM),
            pl.BlockSpec((1, 1, TOP_K * TM_CMB), lambda i: (jnp.minimum(i + 1, nt - 1), 0, 0),
                         memory_space=pltpu.SMEM),
            pl.BlockSpec((TM_CMB, D_MODEL), lambda i: (i, 0)),
            pl.BlockSpec((TM_CMB, TOP_K), lambda i: (i, 0)),
            pl.BlockSpec(memory_space=pl.ANY),
        ],
        out_specs=pl.BlockSpec((TM_CMB, D_MODEL), lambda i: (i, 0)),
        scratch_shapes=[pltpu.VMEM((2,) + _tiled(TOP_K * TM_CMB), F32), pltpu.SemaphoreType.DMA((2,))],
        compiler_params=_cparams(("arbitrary",)),
        name="combine",
    )(dest_b, dest_b, x2, wts, ybuf)


def _route(e_t, r_t, cnt, n):
    counts = cnt[:, 0].astype(I32)
    padded = (counts + TM_MOE - 1) // TM_MOE * TM_MOE
    pend = jnp.cumsum(padded)
    pstart = pend - padded
    n_blocks = n * TOP_K // TM_MOE + N_EXPERTS
    rows = n_blocks * TM_MOE
    dest = r_t[:TOP_K]
    for e in range(N_EXPERTS):
        dest = dest + jnp.where(e_t[:TOP_K] == e, pstart[e].astype(I32), 0)
    nt = n // TM_CMB
    dest_b = dest.reshape(TOP_K, nt, TM_CMB).transpose(1, 0, 2).reshape(nt, 1, TOP_K * TM_CMB)
    blk_row = jnp.arange(n_blocks, dtype=I32)[:, None] * TM_MOE
    blk_e = jnp.minimum(jnp.sum((pend[None, :] <= blk_row).astype(I32), axis=1), N_EXPERTS - 1)
    nvalid = (pend[-1:] // TM_MOE).astype(I32)
    zrow = jnp.clip(pstart + padded - TM_MOE, 0, rows - TM_MOE).astype(I32)
    return dest_b, blk_e, nvalid, jnp.concatenate([zrow, nvalid]), rows


def _layer_weights(l, w_in, norm1_w, ret_gn_w, qk_norm_w, cmp_pos, cmp_w1, cmp_w2, w_o_ret, w_o_nsa, w_out,
                   norm2_w, router_w, router_b, w_gate_up, b_gate_up, w_down, b_down):
    w = w_in[l]
    o = np.cumsum((0, RET_QK, RET_QK, RET_V, RET_V, NSA_Q) + (NSA_KV,) * 6 + (NSA_GATE, 2 * D_MODEL))
    rq, rk, rv, rg, nq = (w[:, o[i]:o[i + 1]] for i in range(5))
    kc, vc, ksl, vsl, kwi, vwi = (w[:, o[5 + i]:o[6 + i]] for i in range(6))
    ng, mg = w[:, o[11]:o[12]], w[:, o[12]:o[13]]

    def dup(t):
        t = t.reshape(D_MODEL, NSA_GROUPS, 1, HEAD_DIM)
        return jnp.broadcast_to(t, (D_MODEL, NSA_GROUPS, 2, HEAD_DIM)).reshape(D_MODEL, NSA_GROUPS * LANES)

    ng_pad = jnp.pad(ng, ((0, 0), (0, LANES - NSA_GATE)))
    wa = jnp.concatenate([mg, rg, nq, ng_pad], axis=1).astype(BF16)
    wb = jnp.concatenate([rq, rk, rv, kc, vc, dup(ksl), dup(kwi), vsl, vwi], axis=1).astype(BF16)
    qk = qk_norm_w[l]
    tile = lambda v: jnp.tile(v, LANES // HEAD_DIM)[None, :]
    w1 = cmp_w1[l].astype(BF16)
    half = CMP_STRIDE * HEAD_DIM
    w2 = jnp.concatenate([cmp_w2[l], cmp_w2[l]], axis=-1).astype(BF16)
    pe = jnp.broadcast_to(cmp_pos[l].reshape(2, 1, CMP_LEN * HEAD_DIM), (2, 8, CMP_LEN * HEAD_DIM))
    return dict(
        nw1=norm1_w[l][None, :], wa=wa, wb=wb, gnw=ret_gn_w[l][None, :],
        qw=tile(qk[0]), kcw=tile(qk[1]), ksw=tile(qk[2]), kww=tile(qk[3]),
        w1a=w1[:, :half], w1b=w1[:, half:], w2=w2, pe=pe,
        wr=w_o_ret[l].astype(BF16), wn=w_o_nsa[l].astype(BF16), wo=w_out[l].astype(BF16),
        nw2=norm2_w[l][None, :], rwt=router_w[l].T, rb=router_b[l][:, None],
        wgu=w_gate_up[l], bgu=b_gate_up[l][:, None, :],
        wd=w_down[l], bd=b_down[l][:, None, :],
    )


def kernel(x, norm1_w, w_in, ret_gn_w, qk_norm_w, cmp_pos, cmp_w1, cmp_w2, w_o_ret, w_o_nsa, w_out, norm2_w,
           router_w, router_b, w_gate_up, b_gate_up, w_down, b_down):
    b, s, _ = x.shape
    depth = w_in.shape[0]
    n = b * s
    n_slc = s // SLC_BLOCK
    assert s % TC_RET == 0 and s % TK == 0 and s >= WIN_TILES * LANES and n % TM_PROJ == 0
    assert n_slc <= LANES

    pos = jnp.arange(s)
    inv = RET_THETA ** (-jnp.arange(0, RET_DK, 2, dtype=F32) / RET_DK)
    ang = pos.astype(F32)[:, None] * inv[None, :]
    ret_cos = jnp.concatenate([jnp.cos(ang), jnp.cos(ang)], axis=1)
    ret_sin = jnp.concatenate([-jnp.sin(ang), jnp.sin(ang)], axis=1)
    dec, zeta_b, xi_b, chunk_decay = _ret_consts()
    ret_tabs = (ret_cos, ret_sin, jnp.asarray(dec), jnp.asarray(zeta_b), jnp.asarray(xi_b), chunk_decay)
    tok_tabs = _rope_half_tables(pos, ROPE_DIM, ROPE_THETA, HEAD_DIM)
    n_cmp = s // CMP_STRIDE
    cmp_tabs = _rope_half_tables(jnp.arange(n_cmp) * CMP_STRIDE + CMP_LEN - 1, ROPE_DIM, ROPE_THETA, HEAD_DIM)
    li = np.arange(LANES)
    bd = jnp.asarray((li[:, None] // HEAD_DIM == li[None, :] // HEAD_DIM).astype(np.float32), BF16)
    ci, sj = np.arange(n_cmp)[None, :], np.arange(LANES)[:, None]
    ovt = ((ci * CMP_STRIDE < (sj + 1) * SLC_BLOCK) & (ci * CMP_STRIDE + CMP_LEN > sj * SLC_BLOCK)
           & (ci < n_cmp - CMP_LEN // CMP_STRIDE + 1) & (sj < n_slc))
    ovt = jnp.asarray(ovt.astype(np.float32), BF16)
    ti = np.arange(TM_PROJ)
    tri = jnp.asarray((ti[:, None] < ti[None, :]).astype(np.float32), BF16)

    x2 = x.reshape(n, D_MODEL)
    for l in range(depth):
        p = _layer_weights(l, w_in, norm1_w, ret_gn_w, qk_norm_w, cmp_pos, cmp_w1, cmp_w2, w_o_ret, w_o_nsa,
                           w_out, norm2_w, router_w, router_b, w_gate_up, b_gate_up, w_down, b_down)
        pa, pb = _inproj(x2, p["nw1"], p["wa"], p["wb"])
        ret = _retention(pa, pb, p["gnw"], ret_tabs, b, s)
        q, ksa, kw, vst, vwt, gt = _nsa_prep(pa, pb, tok_tabs, p["qw"], p["ksw"], p["kww"], bd, b, s)
        c = pb[:, PB_C:PB_C + 2 * NSA_KV].reshape(b, n_cmp, CMP_STRIDE, 2, NSA_GROUPS, HEAD_DIM)
        c = c.transpose(3, 0, 4, 1, 2, 5).reshape(2, b, NSA_GROUPS, n_cmp, CMP_STRIDE * HEAD_DIM)
        kcmp = _compress(c[0], p["w1a"][0], p["w1b"][0], p["pe"][0], p["w2"][0], cmp_tabs, p["kcw"], True)
        vcmp_t = _compress(c[1], p["w1a"][1], p["w1b"][1], p["pe"][1], p["w2"][1], cmp_tabs, p["kcw"], False)
        att = _attention(q, kcmp, vcmp_t, ksa, vst, kw, vwt, gt, ovt, b, s)
        x2 = _outproj(x2, ret, att.reshape(n, NSA_Q), pa, p["wr"], p["wn"], p["wo"])
        h2, e_t, w_t, r_t, cnt = _router(x2, p["nw2"], p["rwt"], p["rb"], tri)
        dest_b, blk_e, nvalid, zinfo, rows = _route(e_t, r_t, cnt, n)
        xbuf = _dispatch(zinfo, dest_b, h2, rows)
        ybuf = _experts(blk_e, nvalid, xbuf, p["wgu"], p["bgu"], p["wd"], p["bd"])
        x2 = _combine(dest_b, x2, w_t[:TOP_K].T, ybuf)
    return x2.reshape(b, s, D_MODEL)
```

```python
import functools

import numpy as np
import jax
import jax.numpy as jnp
from jax import lax
from jax.experimental import pallas as pl
from jax.experimental.pallas import tpu as pltpu

F32 = jnp.float32
BF16 = jnp.bfloat16
I32 = jnp.int32

D_MODEL = 1024
RET_HEADS, RET_DK, RET_DV, RET_CHUNK, RET_THETA = 4, 128, 256, 128, 10000.0
NSA_HEADS, NSA_GROUPS, HEAD_DIM = 16, 2, 64
GROUP_HEADS = NSA_HEADS // NSA_GROUPS
ROPE_DIM, ROPE_THETA = HEAD_DIM // 4, 500000.0
CMP_LEN, CMP_STRIDE, CMP_HIDDEN = 32, 16, 4 * HEAD_DIM
SLC_BLOCK, SLC_TOPK, WINDOW = 64, 16, 512
N_EXPERTS, TOP_K, D_FF = 32, 4, D_MODEL
SWIGLU_LIMIT, SWIGLU_ALPHA = 7.0, 1.702
NORM_EPS = 1e-6
RET_QK, RET_V = RET_HEADS * RET_DK, RET_HEADS * RET_DV
NSA_Q, NSA_KV, NSA_GATE = NSA_HEADS * HEAD_DIM, NSA_GROUPS * HEAD_DIM, NSA_HEADS * 3

LANES = 128
NEG = -1e30
LOG2E = 1.4426950408889634
VMEM_LIMIT = 56 * 1024 * 1024

TM_PROJ = 256
TC_RET = 1024
TM_PREP = 512
TQ = 128
TK = 512
WIN_TILES = WINDOW // LANES + 1
VROWS = HEAD_DIM + 16
TM_MOE = 512
TM_CMB = 256
SUB = 8


def _tiled(nrows):
    return (nrows * SUB, LANES)

PA_MG, PA_RG, PA_NQ, PA_NG = 0, 2048, 3072, 4096
PA_W = 4224
PB_RQ, PB_RK, PB_RV, PB_C, PB_KS, PB_KW, PB_VS, PB_VW = 0, 512, 1024, 2048, 2304, 2560, 2816, 2944
PB_W = 3072


def _cparams(sem):
    return pltpu.CompilerParams(dimension_semantics=sem, vmem_limit_bytes=VMEM_LIMIT)


def _dot(a, b):
    return jnp.dot(a, b, preferred_element_type=F32)


def _dot_nt(a, b):
    return lax.dot_general(a, b, (((1,), (1,)), ((), ())), preferred_element_type=F32)


def _split_bf16(x):
    hi = x.astype(BF16)
    lo = (x - hi.astype(F32)).astype(BF16)
    return hi, lo


def _inproj_kernel(x_ref, nw_ref, wa_ref, wb_ref, pa_ref, pb_ref):
    x = x_ref[...]
    h = x * lax.rsqrt(jnp.mean(x * x, axis=-1, keepdims=True) + NORM_EPS) * nw_ref[...]
    h = h.astype(BF16)
    for c in range(0, PA_W, 512):
        w = min(512, PA_W - c)
        pa_ref[:, c:c + w] = _dot(h, wa_ref[:, c:c + w])
    for c in range(0, PB_W, 512):
        w = min(512, PB_W - c)
        pb_ref[:, c:c + w] = _dot(h, wb_ref[:, c:c + w]).astype(BF16)


def _inproj(x2, nw, wa, wb):
    n = x2.shape[0]
    return pl.pallas_call(
        _inproj_kernel,
        out_shape=(jax.ShapeDtypeStruct((n, PA_W), F32), jax.ShapeDtypeStruct((n, PB_W), BF16)),
        grid=(n // TM_PROJ,),
        in_specs=[
            pl.BlockSpec((TM_PROJ, D_MODEL), lambda i: (i, 0)),
            pl.BlockSpec((1, D_MODEL), lambda i: (0, 0)),
            pl.BlockSpec((D_MODEL, PA_W), lambda i: (0, 0)),
            pl.BlockSpec((D_MODEL, PB_W), lambda i: (0, 0)),
        ],
        out_specs=(pl.BlockSpec((TM_PROJ, PA_W), lambda i: (i, 0)),
                   pl.BlockSpec((TM_PROJ, PB_W), lambda i: (i, 0))),
        compiler_params=_cparams(("parallel",)),
        name="inproj",
    )(x2, nw, wa, wb)


def _outproj_kernel(x_ref, ret_ref, att_ref, mg_ref, wr_ref, wn_ref, wo_ref, o_ref):
    mg = mg_ref[...]
    a = jax.nn.sigmoid(mg[:, :D_MODEL]) * _dot(ret_ref[...], wr_ref[...])
    b = jax.nn.sigmoid(mg[:, D_MODEL:]) * _dot(att_ref[...], wn_ref[...])
    o_ref[...] = x_ref[...] + _dot((a + b).astype(BF16), wo_ref[...])


def _outproj(x2, ret, att, pa, wr, wn, wo):
    n = x2.shape[0]
    row = lambda i: (i, 0)
    full = lambda i: (0, 0)
    return pl.pallas_call(
        _outproj_kernel,
        out_shape=jax.ShapeDtypeStruct((n, D_MODEL), F32),
        grid=(n // TM_PROJ,),
        in_specs=[
            pl.BlockSpec((TM_PROJ, D_MODEL), row),
            pl.BlockSpec((TM_PROJ, RET_V), row),
            pl.BlockSpec((TM_PROJ, NSA_Q), row),
            pl.BlockSpec((TM_PROJ, 2 * D_MODEL), lambda i: (i, PA_MG // (2 * D_MODEL))),
            pl.BlockSpec((RET_V, D_MODEL), full),
            pl.BlockSpec((NSA_Q, D_MODEL), full),
            pl.BlockSpec((D_MODEL, D_MODEL), full),
        ],
        out_specs=pl.BlockSpec((TM_PROJ, D_MODEL), row),
        compiler_params=_cparams(("parallel",)),
        name="outproj",
    )(x2, ret, att, pa, wr, wn, wo)


def _ret_consts():
    h = np.arange(RET_HEADS, dtype=np.float32)
    log_g = np.log1p(-np.exp2(-5.0 - h)).astype(np.float32)
    n = np.arange(RET_CHUNK, dtype=np.float32)
    diff = n[:, None] - n[None, :]
    decay_in = np.where(diff >= 0, np.exp(log_g[:, None, None] * np.maximum(diff, 0.0)), 0.0).astype(np.float32)
    zeta = np.exp(log_g[:, None] * (RET_CHUNK - 1 - n)[None, :]).astype(np.float32)
    xi = np.exp(log_g[:, None] * (n + 1)[None, :]).astype(np.float32)
    chunk_decay = np.exp(log_g * RET_CHUNK).astype(np.float32)
    zeta_b = np.broadcast_to(zeta[:, :, None], (RET_HEADS, RET_CHUNK, RET_DK)).copy()
    xi_b = np.broadcast_to(xi[:, :, None], (RET_HEADS, RET_CHUNK, RET_DV)).copy()
    return decay_in, zeta_b, xi_b, [float(v) for v in chunk_decay]


def _rope_half_tables(pos, rot_dim, theta, period):
    inv = theta ** (-jnp.arange(0, rot_dim, 2, dtype=F32) / rot_dim)
    ang = pos.astype(F32)[:, None] * inv[None, :]
    c, s = jnp.cos(ang), jnp.sin(ang)
    half = rot_dim // 2
    p = pos.shape[0]
    one = jnp.ones((p, period - rot_dim), F32)
    zero_h = jnp.zeros((p, half), F32)
    zero_r = jnp.zeros((p, period - rot_dim), F32)
    cos_t = jnp.concatenate([c, c, one], axis=1)
    sm = jnp.concatenate([-s, zero_h, zero_r], axis=1)
    sp = jnp.concatenate([zero_h, s, zero_r], axis=1)
    rep = LANES // period
    return tuple(jnp.tile(t, (1, rep)) for t in (cos_t, sm, sp))


def _ret_kernel(q_ref, k_ref, v_ref, g_ref, cos_ref, sin_ref, dec_ref, zeta_ref, xi_ref, gnw_ref,
                o_ref, state_ref, *, chunk_decay):
    @pl.when(pl.program_id(1) == 0)
    def _():
        state_ref[...] = jnp.zeros_like(state_ref)

    n_chunks = TC_RET // RET_CHUNK
    for h in range(RET_HEADS):
        dec = dec_ref[h]
        zeta = zeta_ref[h]
        xi = xi_ref[h]
        gnw = gnw_ref[:, h * RET_DV:(h + 1) * RET_DV]
        for c in range(n_chunks):
            rows = slice(c * RET_CHUNK, (c + 1) * RET_CHUNK)
            cs = cos_ref[rows, :]
            sn = sin_ref[rows, :]
            q = q_ref[rows, h * RET_DK:(h + 1) * RET_DK].astype(F32)
            k = k_ref[rows, h * RET_DK:(h + 1) * RET_DK].astype(F32)
            q = q * cs + pltpu.roll(q, RET_DK // 2, 1) * sn
            k = (k * cs + pltpu.roll(k, RET_DK // 2, 1) * sn) * (RET_DK ** -0.5)
            v = v_ref[rows, h * RET_DV:(h + 1) * RET_DV]
            qb = q.astype(BF16)
            s = _dot_nt(qb, k.astype(BF16)) * dec
            inner = _dot(s.astype(BF16), v)
            r = state_ref[h]
            cross = _dot(qb, r.astype(BF16)) * xi
            kzt = (k * zeta).T.astype(BF16)
            state_ref[h] = chunk_decay[h] * r + _dot(kzt, v)
            o = inner + cross
            mu = jnp.mean(o, axis=-1, keepdims=True)
            d = o - mu
            var = jnp.mean(d * d, axis=-1, keepdims=True)
            on = d * lax.rsqrt(var + NORM_EPS) * gnw
            g = g_ref[rows, h * RET_DV:(h + 1) * RET_DV]
            o_ref[rows, h * RET_DV:(h + 1) * RET_DV] = (g * jax.nn.sigmoid(g) * on).astype(BF16)


def _retention(pa, pb, gnw, tabs, b, s):
    cos_t, sin_t, dec, zeta_b, xi_b, chunk_decay = tabs
    nt = s // TC_RET
    kern = functools.partial(_ret_kernel, chunk_decay=chunk_decay)
    tok = lambda w, j: pl.BlockSpec((TC_RET, w), lambda bi, si, j=j: (bi * nt + si, j))
    cst3 = lambda shp: pl.BlockSpec(shp, lambda bi, si: (0, 0, 0))
    return pl.pallas_call(
        kern,
        out_shape=jax.ShapeDtypeStruct((b * s, RET_V), BF16),
        grid=(b, nt),
        in_specs=[
            tok(RET_QK, PB_RQ // RET_QK),
            tok(RET_QK, PB_RK // RET_QK),
            tok(RET_V, PB_RV // RET_V),
            tok(RET_V, PA_RG // RET_V),
            pl.BlockSpec((TC_RET, LANES), lambda bi, si: (si, 0)),
            pl.BlockSpec((TC_RET, LANES), lambda bi, si: (si, 0)),
            cst3((RET_HEADS, RET_CHUNK, RET_CHUNK)),
            cst3((RET_HEADS, RET_CHUNK, RET_DK)),
            cst3((RET_HEADS, RET_CHUNK, RET_DV)),
            pl.BlockSpec((1, RET_V), lambda bi, si: (0, 0)),
        ],
        out_specs=pl.BlockSpec((TC_RET, RET_V), lambda bi, si: (bi * nt + si, 0)),
        scratch_shapes=[pltpu.VMEM((RET_HEADS, RET_DK, RET_DV), F32)],
        compiler_params=_cparams(("parallel", "arbitrary")),
        name="retention",
    )(pb, pb, pb, pa, cos_t, sin_t, dec, zeta_b, xi_b, gnw)


def _ones_row_pad(n):
    pad_rows = 16
    return jnp.where(lax.broadcasted_iota(I32, (pad_rows, n), 0) == 0, 1.0, 0.0)


def _rope16(x, cs, sm, sp):
    half = ROPE_DIM // 2
    return x * cs + pltpu.roll(x, LANES - half, 1) * sm + pltpu.roll(x, half, 1) * sp


def _prep_kernel(nq_ref, ks_ref, kw_ref, vs_ref, vw_ref, ng_ref, cs_ref, sm_ref, sp_ref,
                 qw_ref, ksw_ref, kww_ref, bd_ref,
                 q_ref, ksa_ref, kwo_ref, vst_ref, vwt_ref, gt_ref):
    cs, sm, sp = cs_ref[...], sm_ref[...], sp_ref[...]
    bd = bd_ref[...]
    for p in range(NSA_Q // LANES):
        x = nq_ref[:, p * LANES:(p + 1) * LANES]
        hi, lo = _split_bf16(x * x)
        ms = (_dot(hi, bd) + _dot(lo, bd)) * (1.0 / HEAD_DIM)
        y = x * lax.rsqrt(ms + NORM_EPS) * qw_ref[...]
        y = _rope16(y, cs, sm, sp) * (HEAD_DIM ** -0.5 * LOG2E)
        q_ref[:, p * LANES:(p + 1) * LANES] = y.astype(BF16)
    tok = pl.program_id(1) * TM_PREP + lax.broadcasted_iota(I32, (TM_PREP, LANES), 0)
    blk = lax.shift_right_logical(tok, int(np.log2(SLC_BLOCK)))
    onehot = jnp.where(lax.broadcasted_iota(I32, (TM_PREP, LANES), 1) == blk, 1.0, 0.0).astype(BF16)
    for g in range(NSA_GROUPS):
        sl = slice(g * LANES, (g + 1) * LANES)
        x = ks_ref[:, sl].astype(F32)
        y = x * lax.rsqrt(jnp.mean(x * x, axis=-1, keepdims=True) + NORM_EPS) * ksw_ref[...]
        ksa_ref[:, 2 * g * LANES:(2 * g + 1) * LANES] = _rope16(y, cs, sm, sp).astype(BF16)
        ksa_ref[:, (2 * g + 1) * LANES:(2 * g + 2) * LANES] = onehot
        x = kw_ref[:, sl].astype(F32)
        y = x * lax.rsqrt(jnp.mean(x * x, axis=-1, keepdims=True) + NORM_EPS) * kww_ref[...]
        kwo_ref[:, sl] = _rope16(y, cs, sm, sp).astype(BF16)
    vt = vs_ref[...].astype(F32).T
    wt = vw_ref[...].astype(F32).T
    for g in range(NSA_GROUPS):
        rows = slice(g * HEAD_DIM, (g + 1) * HEAD_DIM)
        vst_ref[g, 0:HEAD_DIM, :] = vt[rows, :].astype(BF16)
        vst_ref[g, HEAD_DIM:VROWS, :] = _ones_row_pad(TM_PREP).astype(BF16)
        for r in range(TM_PREP // LANES):
            vwt_ref[g, r, 0:HEAD_DIM, :] = wt[rows, r * LANES:(r + 1) * LANES].astype(BF16)
            vwt_ref[g, r, HEAD_DIM:VROWS, :] = _ones_row_pad(LANES).astype(BF16)
    gt_ref[...] = jax.nn.sigmoid(ng_ref[...]).T


def _nsa_prep(pa, pb, tabs, qw, ksw, kww, bd, b, s):
    cs, sm, sp = tabs
    nt = s // TM_PREP
    tokb = lambda w, j: pl.BlockSpec((TM_PREP, w), lambda bi, si, j=j: (bi * nt + si, j))
    tab = pl.BlockSpec((TM_PREP, LANES), lambda bi, si: (si, 0))
    vec = pl.BlockSpec((1, LANES), lambda bi, si: (0, 0))
    kv_w = NSA_GROUPS * LANES
    out_shape = (
        jax.ShapeDtypeStruct((b, s, NSA_Q), BF16),
        jax.ShapeDtypeStruct((b, s, 2 * kv_w), BF16),
        jax.ShapeDtypeStruct((b, s, kv_w), BF16),
        jax.ShapeDtypeStruct((b, NSA_GROUPS, nt, VROWS, TM_PREP), BF16),
        jax.ShapeDtypeStruct((b, NSA_GROUPS, s // LANES, VROWS, LANES), BF16),
        jax.ShapeDtypeStruct((b, LANES, s), F32),
    )
    out_specs = (
        pl.BlockSpec((None, TM_PREP, NSA_Q), lambda bi, si: (bi, si, 0)),
        pl.BlockSpec((None, TM_PREP, 2 * kv_w), lambda bi, si: (bi, si, 0)),
        pl.BlockSpec((None, TM_PREP, kv_w), lambda bi, si: (bi, si, 0)),
        pl.BlockSpec((None, NSA_GROUPS, None, VROWS, TM_PREP), lambda bi, si: (bi, 0, si, 0, 0)),
        pl.BlockSpec((None, NSA_GROUPS, TM_PREP // LANES, VROWS, LANES), lambda bi, si: (bi, 0, si, 0, 0)),
        pl.BlockSpec((None, LANES, TM_PREP), lambda bi, si: (bi, 0, si)),
    )
    return pl.pallas_call(
        _prep_kernel,
        out_shape=out_shape,
        grid=(b, nt),
        in_specs=[
            tokb(NSA_Q, PA_NQ // NSA_Q),
            tokb(kv_w, PB_KS // kv_w), tokb(kv_w, PB_KW // kv_w),
            tokb(NSA_KV, PB_VS // NSA_KV), tokb(NSA_KV, PB_VW // NSA_KV),
            tokb(LANES, PA_NG // LANES),
            tab, tab, tab, vec, vec, vec,
            pl.BlockSpec((LANES, LANES), lambda bi, si: (0, 0)),
        ],
        out_specs=out_specs,
        compiler_params=_cparams(("parallel", "parallel")),
        name="nsa_prep",
    )(pa, pb, pb, pb, pb, pa, cs, sm, sp, qw, ksw, kww, bd)


def _compress_kernel(c_ref, w1a_ref, w1b_ref, pe_ref, w2_ref, cs_ref, sm_ref, sp_ref, nw_ref, o_ref, *, is_key):
    c = c_ref[...]
    n = c.shape[0]
    a = _dot(c, w1a_ref[...])
    bb = _dot(c, w1b_ref[...])
    pe_hi, pe_lo = _split_bf16(pe_ref[...])
    w1 = jnp.concatenate([w1a_ref[...], w1b_ref[...]], axis=0)
    pe_term = (_dot(pe_hi, w1) + _dot(pe_lo, w1))[0:1, :]
    hid = a + pltpu.roll(bb, n - 1, 0) + pe_term
    t = hid * (0.7978845608028654 * (1.0 + 0.044715 * hid * hid))
    act = 0.5 * hid * (1.0 + jnp.tanh(t))
    y = _dot(act.astype(BF16), w2_ref[...])
    if is_key:
        y = y * lax.rsqrt(jnp.mean(y * y, axis=-1, keepdims=True) + NORM_EPS) * nw_ref[...]
        o_ref[...] = _rope16(y, cs_ref[...], sm_ref[...], sp_ref[...]).astype(BF16)
    else:
        o_ref[...] = jnp.concatenate([y.T[0:HEAD_DIM, :], _ones_row_pad(n)], axis=0).astype(BF16)


def _compress(cflat, w1a, w1b, pe, w2, tabs, nw, is_key):
    b, g, n, width = cflat.shape
    cs, sm, sp = tabs
    full2 = lambda shp: pl.BlockSpec(shp, lambda bi, gi: (0, 0))
    out_block = (None, None, n, LANES) if is_key else (None, None, VROWS, n)
    out_shape = (b, g, n, LANES) if is_key else (b, g, VROWS, n)
    return pl.pallas_call(
        functools.partial(_compress_kernel, is_key=is_key),
        out_shape=jax.ShapeDtypeStruct(out_shape, BF16),
        grid=(b, g),
        in_specs=[
            pl.BlockSpec((None, None, n, width), lambda bi, gi: (bi, gi, 0, 0)),
            full2(w1a.shape), full2(w1b.shape), full2(pe.shape), full2(w2.shape),
            full2(cs.shape), full2(sm.shape), full2(sp.shape), full2(nw.shape),
        ],
        out_specs=pl.BlockSpec(out_block, lambda bi, gi: (bi, gi, 0, 0)),
        compiler_params=_cparams(("parallel", "parallel")),
        name="compress_k" if is_key else "compress_v",
    )(cflat, w1a, w1b, pe, w2, cs, sm, sp, nw)


def _weighted_values(vt, e):
    acc = _dot(vt, e.astype(BF16))
    return acc[0:HEAD_DIM, :], acc[HEAD_DIM:HEAD_DIM + 1, :]


def _attn_kernel(q_ref, kc_ref, vct_ref, ksa_ref, vst_ref, kw_ref, vwt_ref, gt_ref, ovt_ref,
                 o_ref, s_ref, cm_ref, qa_ref):
    gi = pl.program_id(1)
    q0 = pl.program_id(2) * TQ
    n_cmp = kc_ref.shape[0]
    ncols = GROUP_HEADS * TQ

    low = lax.broadcasted_iota(I32, (TQ, LANES), 1) < HEAD_DIM
    zero = jnp.zeros((TQ, LANES), BF16)
    parts = []
    for h in range(GROUP_HEADS):
        slab = q_ref[:, (h // 2) * LANES:(h // 2 + 1) * LANES]
        parts.append(jnp.where(low, slab, zero) if h % 2 == 0 else jnp.where(low, zero, slab))
    qs = jnp.concatenate(parts, axis=0)
    t_row = q0 + lax.broadcasted_iota(I32, (1, TQ), 1)

    def all_heads(bias):
        return jnp.concatenate([bias] * GROUP_HEADS, axis=1)

    def produce(keys, queries, slot, keep_max):
        st = _dot_nt(keys, queries)
        s_ref[slot, 0:keys.shape[0], :] = st
        if keep_max:
            cm_ref[slot] = jnp.broadcast_to(jnp.max(st, axis=0, keepdims=True), (8, ncols))

    def consume(slot, nk, vt, carry, bias):
        m_old, a_old = carry
        st = s_ref[slot, 0:nk, :]
        if bias is None:
            cm = cm_ref[slot][0:1, :]
        else:
            st = st + all_heads(bias)
            cm = jnp.max(st, axis=0, keepdims=True)
        m_new = jnp.maximum(m_old, cm)
        alpha = jnp.exp2(m_old - m_new)
        e = jnp.exp2(st - m_new)
        return m_new, alpha * a_old + _dot(vt, e.astype(BF16))

    init = (jnp.full((1, ncols), NEG, F32), jnp.zeros((VROWS, ncols), F32))

    def finish(acc):
        return acc[0:HEAD_DIM, :] * (1.0 / acc[HEAD_DIM:HEAD_DIM + 1, :])

    w0 = jnp.maximum(q0 - WINDOW, 0)
    wt0 = lax.shift_right_logical(w0, int(np.log2(LANES)))
    n_wb = WIN_TILES * LANES - TK
    produce(kc_ref[...], qs, 0, False)
    produce(kw_ref[pl.ds(pl.multiple_of(w0, LANES), TK), :], qs, 1, False)

    cmp_end = lax.broadcasted_iota(I32, (n_cmp, 1), 0) * CMP_STRIDE + (CMP_LEN - 1)
    st = s_ref[0, 0:n_cmp, :] + all_heads(jnp.where(cmp_end <= t_row, 0.0, NEG))
    m = jnp.max(st, axis=0, keepdims=True)
    e = jnp.exp2(st - jnp.where(m > 0.5 * NEG, m, 0.0))
    oc_all, l = _weighted_values(vct_ref[...], e)
    rl = 1.0 / jnp.where(l > 0.0, l, 1.0)
    oc_all = oc_all * rl
    psum = e[:, 0:TQ] * rl[:, 0:TQ]
    for h in range(1, GROUP_HEADS):
        psum = psum + e[:, h * TQ:(h + 1) * TQ] * rl[:, h * TQ:(h + 1) * TQ]
    ph, plo = _split_bf16(psum)
    imp_t = _dot(ovt_ref[...], ph) + _dot(ovt_ref[...], plo)

    def wbias(start, nk):
        kpos = start + lax.broadcasted_iota(I32, (nk, 1), 0)
        return jnp.where((kpos <= t_row) & (kpos > t_row - WINDOW), 0.0, NEG)

    produce(kw_ref[pl.ds(pl.multiple_of(w0 + TK, LANES), n_wb), :], qs, 0, False)
    vwa = jnp.concatenate([vwt_ref[wt0 + r] for r in range(TK // LANES)], axis=1)
    carry_w = consume(1, TK, vwa, init, wbias(w0, TK))

    jrow = lax.broadcasted_iota(I32, (LANES, TQ), 0)
    cur = lax.shift_right_logical(q0 + lax.broadcasted_iota(I32, (LANES, TQ), 1), int(np.log2(SLC_BLOCK)))
    forced = (jrow == 0) | (jrow == cur) | (jrow == cur - 1)
    valid = jrow <= cur
    bias = jnp.where(forced & valid, 0.0, NEG)
    val = jnp.where(valid & jnp.logical_not(forced), imp_t, -jnp.inf)
    for _ in range(SLC_TOPK - 3):
        mx = jnp.max(val, axis=0, keepdims=True)
        first = jnp.min(jnp.where(val == mx, jrow, LANES), axis=0, keepdims=True)
        pick = jrow == first
        bias = jnp.where(pick & valid, 0.0, bias)
        val = jnp.where(pick, -jnp.inf, val)

    vwb = jnp.concatenate([vwt_ref[wt0 + TK // LANES + r] for r in range(n_wb // LANES)], axis=1)
    _, a_w = consume(0, n_wb, vwb, carry_w, wbias(w0 + TK, n_wb))
    ow_all = finish(a_w)

    selb = bias.T.astype(BF16)
    qa_ref[...] = jnp.concatenate([qs, jnp.concatenate([selb] * GROUP_HEADS, axis=0)], axis=1)

    def sel_scores(ti, slot):
        produce(ksa_ref[pl.ds(pl.multiple_of(ti * TK, TK), TK), :], qa_ref[...], slot, True)

    def sel_consume(ti, slot, carry, causal):
        cb = None
        if causal:
            cb = jnp.where(ti * TK + lax.broadcasted_iota(I32, (TK, 1), 0) <= t_row, 0.0, NEG)
        return consume(slot, TK, vst_ref[ti], carry, cb)

    n_full = lax.shift_right_logical(q0, int(np.log2(TK)))
    sel_scores(0, 0)

    def two_tiles(j, carry):
        sel_scores(2 * j + 1, 1)
        carry = sel_consume(2 * j, 0, carry, False)
        sel_scores(2 * j + 2, 0)
        return sel_consume(2 * j + 1, 1, carry, False)

    carry = lax.fori_loop(0, lax.shift_right_logical(n_full, 1), two_tiles, init)

    def odd_tail(c):
        sel_scores(n_full, 1)
        return sel_consume(n_full, 1, sel_consume(n_full - 1, 0, c, False), True)

    _, a_s = lax.cond((n_full & 1) == 1, odd_tail, lambda c: sel_consume(n_full, 0, c, True), carry)
    os_all = finish(a_s)

    for p in range(GROUP_HEADS // 2):
        halves = []
        for h in (2 * p, 2 * p + 1):
            c = slice(h * TQ, (h + 1) * TQ)
            gbase = (gi * GROUP_HEADS + h) * 3
            halves.append(gt_ref[pl.ds(gbase, 1), :] * oc_all[:, c]
                          + gt_ref[pl.ds(gbase + 1, 1), :] * os_all[:, c]
                          + gt_ref[pl.ds(gbase + 2, 1), :] * ow_all[:, c])
        o_ref[:, p * LANES:(p + 1) * LANES] = jnp.concatenate(halves, axis=0).T.astype(BF16)


def _attention(q, kc, vct, ksa, vst, kw, vwt, gt, ovt, b, s):
    nq = s // TQ
    n_cmp = kc.shape[2]
    gw = GROUP_HEADS * HEAD_DIM
    ncols = GROUP_HEADS * TQ
    return pl.pallas_call(
        _attn_kernel,
        out_shape=jax.ShapeDtypeStruct((b, s, NSA_Q), BF16),
        grid=(b, NSA_GROUPS, nq),
        in_specs=[
            pl.BlockSpec((None, TQ, gw), lambda bi, gi, qi: (bi, qi, gi)),
            pl.BlockSpec((None, None, n_cmp, LANES), lambda bi, gi, qi: (bi, gi, 0, 0)),
            pl.BlockSpec((None, None, VROWS, n_cmp), lambda bi, gi, qi: (bi, gi, 0, 0)),
            pl.BlockSpec((None, s, 2 * LANES), lambda bi, gi, qi: (bi, 0, gi)),
            pl.BlockSpec((None, None, s // TK, VROWS, TK), lambda bi, gi, qi: (bi, gi, 0, 0, 0)),
            pl.BlockSpec((None, s, LANES), lambda bi, gi, qi: (bi, 0, gi)),
            pl.BlockSpec((None, None, s // LANES, VROWS, LANES), lambda bi, gi, qi: (bi, gi, 0, 0, 0)),
            pl.BlockSpec((None, LANES, TQ), lambda bi, gi, qi: (bi, 0, qi)),
            pl.BlockSpec((LANES, n_cmp), lambda bi, gi, qi: (0, 0)),
        ],
        out_specs=pl.BlockSpec((None, TQ, gw), lambda bi, gi, qi: (bi, qi, gi)),
        scratch_shapes=[
            pltpu.VMEM((2, TK, ncols), F32),
            pltpu.VMEM((2, 8, ncols), F32),
            pltpu.VMEM((ncols, 2 * LANES), BF16),
        ],
        compiler_params=_cparams(("parallel", "parallel", "arbitrary")),
        name="nsa_attention",
    )(q, kc, vct, ksa, vst, kw, vwt, gt, ovt)


def _router_kernel(x_ref, nw_ref, wt_ref, b_ref, tri_ref, h_ref, e_ref, w_ref, r_ref, cnt_ref, base_ref):
    @pl.when(pl.program_id(0) == 0)
    def _():
        base_ref[...] = jnp.zeros_like(base_ref)

    x = x_ref[...]
    h = x * lax.rsqrt(jnp.mean(x * x, axis=-1, keepdims=True) + NORM_EPS) * nw_ref[...]
    _store_row_tiles(h_ref, h)
    h_hi, h_lo = _split_bf16(h)
    w_hi, w_lo = _split_bf16(wt_ref[...])
    lg = _dot_nt(w_hi, h_hi) + _dot_nt(w_hi, h_lo) + _dot_nt(w_lo, h_hi) + b_ref[...]
    erow = lax.broadcasted_iota(I32, lg.shape, 0)
    vals, hots = [], []
    for _ in range(TOP_K):
        mx = jnp.max(lg, axis=0, keepdims=True)
        first = jnp.min(jnp.where(lg == mx, erow, N_EXPERTS), axis=0, keepdims=True)
        hot = erow == first
        vals.append(mx)
        hots.append(hot)
        lg = jnp.where(hot, -jnp.inf, lg)
    ex = [jnp.exp(v - vals[0]) for v in vals]
    den = ex[0] + ex[1] + ex[2] + ex[3]
    onehots = [h.astype(F32) for h in hots]
    cnt = onehots[0] + onehots[1] + onehots[2] + onehots[3]
    pref = _dot(cnt.astype(BF16), tri_ref[...]) + base_ref[:, 0:1]
    erow_f = erow.astype(F32)
    idxs = [jnp.sum(oh * erow_f, axis=0, keepdims=True) for oh in onehots]
    ranks = [jnp.sum(oh * pref, axis=0, keepdims=True) for oh in onehots]
    base_ref[...] = base_ref[...] + jnp.sum(cnt, axis=1, keepdims=True)
    cnt_ref[...] = base_ref[...]
    pad_f = [jnp.zeros_like(den)] * (8 - TOP_K)
    e_ref[...] = jnp.concatenate(idxs + pad_f, axis=0).astype(I32)
    r_ref[...] = jnp.concatenate(ranks + pad_f, axis=0).astype(I32)
    w_ref[...] = jnp.concatenate([e / den for e in ex] + pad_f, axis=0)


def _router(x2, nw, wt, bcol, tri):
    n = x2.shape[0]
    small = pl.BlockSpec((8, TM_PROJ), lambda i: (0, i))
    return pl.pallas_call(
        _router_kernel,
        out_shape=(jax.ShapeDtypeStruct(_tiled(n), F32),
                   jax.ShapeDtypeStruct((8, n), I32),
                   jax.ShapeDtypeStruct((8, n), F32),
                   jax.ShapeDtypeStruct((8, n), I32),
                   jax.ShapeDtypeStruct((N_EXPERTS, LANES), F32)),
        grid=(n // TM_PROJ,),
        in_specs=[
            pl.BlockSpec((TM_PROJ, D_MODEL), lambda i: (i, 0)),
            pl.BlockSpec((1, D_MODEL), lambda i: (0, 0)),
            pl.BlockSpec((N_EXPERTS, D_MODEL), lambda i: (0, 0)),
            pl.BlockSpec((N_EXPERTS, 1), lambda i: (0, 0)),
            pl.BlockSpec((TM_PROJ, TM_PROJ), lambda i: (0, 0)),
        ],
        out_specs=(pl.BlockSpec(_tiled(TM_PROJ), lambda i: (i, 0)), small, small, small,
                   pl.BlockSpec((N_EXPERTS, LANES), lambda i: (0, 0))),
        scratch_shapes=[pltpu.VMEM((N_EXPERTS, LANES), F32)],
        compiler_params=_cparams(("arbitrary",)),
        name="router",
    )(x2, nw, wt, bcol, tri)


def _row_copy(src, src_row, dst, dst_row, sem):
    tile = lambda ref, r: ref.at[pl.ds(pl.multiple_of(r * SUB, SUB), SUB), :]
    return pltpu.make_async_copy(tile(src, src_row), tile(dst, dst_row), sem)


def _issue_rows(nrow, copy_of):
    def body(g, c):
        for j in range(8):
            copy_of(g * 8 + j).start(priority=j % 2)
        return c

    lax.fori_loop(0, nrow // 8, body, 0)


def _wait_rows(nrow, copy_of):
    def body(g, c):
        for j in range(8):
            copy_of(0).wait()
        return c

    lax.fori_loop(0, nrow // 8, body, 0)


def _store_row_tiles(ref, x):
    for c in range(SUB):
        ref[pl.ds(c, x.shape[0], stride=SUB), :] = x[:, c * LANES:(c + 1) * LANES]


def _load_row_tiles(ref, first_row, nrows):
    return jnp.concatenate([ref[pl.ds(first_row * SUB + c, nrows, stride=SUB), :] for c in range(SUB)], axis=1)


def _dispatch_kernel(zinfo_ref, dest_ref, h_ref, x_hbm, zbuf, sem, zsem, *, n_blocks):
    def zero_block(row):
        return pltpu.make_async_copy(
            zbuf, x_hbm.at[pl.ds(pl.multiple_of(row * SUB, TM_MOE * SUB), TM_MOE * SUB), :], zsem)

    @pl.when(pl.program_id(0) == 0)
    def _():
        zbuf[...] = jnp.zeros_like(zbuf)
        for e in range(N_EXPERTS):
            zero_block(zinfo_ref[e]).start()
        for e in range(N_EXPERTS):
            zero_block(0).wait()
        nvalid = zinfo_ref[N_EXPERTS]

        def ztail(bk, c):
            zero_block(bk * TM_MOE).start()
            return c

        lax.fori_loop(nvalid, n_blocks, ztail, 0)

        def zwait(bk, c):
            zero_block(0).wait()
            return c

        lax.fori_loop(nvalid, n_blocks, zwait, 0)

    nrow = TOP_K * TM_CMB
    _issue_rows(nrow, lambda r: _row_copy(h_ref, r & (TM_CMB - 1), x_hbm, dest_ref[0, 0, r], sem))
    _wait_rows(nrow, lambda r: _row_copy(h_ref, 0, x_hbm, 0, sem))


def _dispatch(zinfo, dest_b, h3, rows):
    n = h3.shape[0] // SUB
    grid_spec = pltpu.PrefetchScalarGridSpec(
        num_scalar_prefetch=1,
        grid=(n // TM_CMB,),
        in_specs=[
            pl.BlockSpec((1, 1, TOP_K * TM_CMB), lambda i, z: (i, 0, 0), memory_space=pltpu.SMEM),
            pl.BlockSpec(_tiled(TM_CMB), lambda i, z: (i, 0)),
        ],
        out_specs=pl.BlockSpec(memory_space=pl.ANY),
        scratch_shapes=[pltpu.VMEM(_tiled(TM_MOE), F32), pltpu.SemaphoreType.DMA(()),
                        pltpu.SemaphoreType.DMA(())],
    )
    return pl.pallas_call(
        functools.partial(_dispatch_kernel, n_blocks=rows // TM_MOE),
        out_shape=jax.ShapeDtypeStruct(_tiled(rows), F32),
        grid_spec=grid_spec,
        compiler_params=_cparams(("arbitrary",)),
        name="dispatch",
    )(zinfo, dest_b, h3)


def _expert_kernel(blk_e_ref, nvalid_ref, x_ref, wgu_ref, bgu_ref, wd_ref, bd_ref, y_ref, wgu_bf, wd_bf):
    i = pl.program_id(0)
    used = i < nvalid_ref[0]

    @pl.when(jnp.logical_not(used))
    def _():
        y_ref[...] = jnp.zeros_like(y_ref)

    @pl.when(used & ((i == 0) | (blk_e_ref[i] != blk_e_ref[jnp.maximum(i - 1, 0)])))
    def _():
        wgu_bf[...] = wgu_ref[...].astype(BF16)
        wd_bf[...] = wd_ref[...].astype(BF16)

    @pl.when(used)
    def _():
        gu = _dot(_load_row_tiles(x_ref, 0, TM_MOE).astype(BF16), wgu_bf[...]) + bgu_ref[...]
        gate = jnp.minimum(gu[:, :D_FF], SWIGLU_LIMIT)
        up = jnp.clip(gu[:, D_FF:], -SWIGLU_LIMIT, SWIGLU_LIMIT)
        glu = gate * jax.nn.sigmoid(gate * SWIGLU_ALPHA)
        _store_row_tiles(y_ref, _dot(((up + 1.0) * glu).astype(BF16), wd_bf[...]) + bd_ref[...])


def _experts(blk_e, nvalid, xbuf, wgu, bgu, wd, bd, layer):
    n_blocks = blk_e.shape[0]
    wblk = lambda i, be, nv: (be[i], 0, 0)
    wblk4 = lambda i, be, nv: (layer, be[i], 0, 0)
    grid_spec = pltpu.PrefetchScalarGridSpec(
        num_scalar_prefetch=2,
        grid=(n_blocks,),
        in_specs=[
            pl.BlockSpec(_tiled(TM_MOE), lambda i, be, nv: (jnp.minimum(i, nv[0] - 1), 0)),
            pl.BlockSpec((None, None, D_MODEL, 2 * D_FF), wblk4),
            pl.BlockSpec((None, 1, 2 * D_FF), wblk),
            pl.BlockSpec((None, None, D_FF, D_MODEL), wblk4),
            pl.BlockSpec((None, 1, D_MODEL), wblk),
        ],
        out_specs=pl.BlockSpec(_tiled(TM_MOE), lambda i, be, nv: (i, 0)),
        scratch_shapes=[pltpu.VMEM((D_MODEL, 2 * D_FF), BF16), pltpu.VMEM((D_FF, D_MODEL), BF16)],
    )
    return pl.pallas_call(
        _expert_kernel,
        out_shape=jax.ShapeDtypeStruct(_tiled(n_blocks * TM_MOE), F32),
        grid_spec=grid_spec,
        compiler_params=_cparams(("arbitrary",)),
        name="experts",
    )(blk_e, nvalid, xbuf, wgu, bgu, wd, bd)


def _combine_kernel(dcur_ref, dnext_ref, x_ref, w_ref, y_hbm, o_ref, ybuf, sem):
    i = pl.program_id(0)
    nrow = TOP_K * TM_CMB
    slot = i & 1

    def issue(d_ref, s):
        _issue_rows(nrow, lambda r: _row_copy(y_hbm, d_ref[0, 0, r], ybuf.at[s], r, sem.at[s]))

    @pl.when(i == 0)
    def _():
        issue(dcur_ref, 0)

    @pl.when(i + 1 < pl.num_programs(0))
    def _():
        issue(dnext_ref, 1 - slot)

    _wait_rows(nrow, lambda r: _row_copy(y_hbm, 0, ybuf.at[slot], 0, sem.at[slot]))
    acc = x_ref[...]
    w = w_ref[...]
    for k in range(TOP_K):
        acc = acc + w[:, k:k + 1] * _load_row_tiles(ybuf.at[slot], k * TM_CMB, TM_CMB)
    o_ref[...] = acc


def _combine(dest_b, x2, wts, ybuf):
    n = x2.shape[0]
    nt = n // TM_CMB
    return pl.pallas_call(
        _combine_kernel,
        out_shape=jax.ShapeDtypeStruct((n, D_MODEL), F32),
        grid=(nt,),
        in_specs=[
            pl.BlockSpec((1, 1, TOP_K * TM_CMB), lambda i: (i, 0, 0), memory_space=pltpu.SMEM),
            pl.BlockSpec((1, 1, TOP_K * TM_CMB), lambda i: (jnp.minimum(i + 1, nt - 1), 0, 0),
                         memory_space=pltpu.SMEM),
            pl.BlockSpec((TM_CMB, D_MODEL), lambda i: (i, 0)),
            pl.BlockSpec((TM_CMB, TOP_K), lambda i: (i, 0)),
            pl.BlockSpec(memory_space=pl.ANY),
        ],
        out_specs=pl.BlockSpec((TM_CMB, D_MODEL), lambda i: (i, 0)),
        scratch_shapes=[pltpu.VMEM((2,) + _tiled(TOP_K * TM_CMB), F32), pltpu.SemaphoreType.DMA((2,))],
        compiler_params=_cparams(("arbitrary",)),
        name="combine",
    )(dest_b, dest_b, x2, wts, ybuf)


def _route(e_t, r_t, cnt, n):
    counts = cnt[:, 0].astype(I32)
    padded = (counts + TM_MOE - 1) // TM_MOE * TM_MOE
    pend = jnp.cumsum(padded)
    pstart = pend - padded
    n_blocks = n * TOP_K // TM_MOE + N_EXPERTS
    rows = n_blocks * TM_MOE
    dest = r_t[:TOP_K]
    for e in range(N_EXPERTS):
        dest = dest + jnp.where(e_t[:TOP_K] == e, pstart[e].astype(I32), 0)
    nt = n // TM_CMB
    dest_b = dest.reshape(TOP_K, nt, TM_CMB).transpose(1, 0, 2).reshape(nt, 1, TOP_K * TM_CMB)
    blk_row = jnp.arange(n_blocks, dtype=I32)[:, None] * TM_MOE
    blk_e = jnp.minimum(jnp.sum((pend[None, :] <= blk_row).astype(I32), axis=1), N_EXPERTS - 1)
    nvalid = (pend[-1:] // TM_MOE).astype(I32)
    zrow = jnp.clip(pstart + padded - TM_MOE, 0, rows - TM_MOE).astype(I32)
    return dest_b, blk_e, nvalid, jnp.concatenate([zrow, nvalid]), rows


def _layer_weights(l, w_in, norm1_w, ret_gn_w, qk_norm_w, cmp_pos, cmp_w1, cmp_w2, w_o_ret, w_o_nsa, w_out,
                   norm2_w, router_w, router_b, w_gate_up, b_gate_up, w_down, b_down):
    w = w_in[l]
    o = np.cumsum((0, RET_QK, RET_QK, RET_V, RET_V, NSA_Q) + (NSA_KV,) * 6 + (NSA_GATE, 2 * D_MODEL))
    rq, rk, rv, rg, nq = (w[:, o[i]:o[i + 1]] for i in range(5))
    kc, vc, ksl, vsl, kwi, vwi = (w[:, o[5 + i]:o[6 + i]] for i in range(6))
    ng, mg = w[:, o[11]:o[12]], w[:, o[12]:o[13]]

    def dup(t):
        t = t.reshape(D_MODEL, NSA_GROUPS, 1, HEAD_DIM)
        return jnp.broadcast_to(t, (D_MODEL, NSA_GROUPS, 2, HEAD_DIM)).reshape(D_MODEL, NSA_GROUPS * LANES)

    ng_pad = jnp.pad(ng, ((0, 0), (0, LANES - NSA_GATE)))
    wa = jnp.concatenate([mg, rg, nq, ng_pad], axis=1).astype(BF16)
    wb = jnp.concatenate([rq, rk, rv, kc, vc, dup(ksl), dup(kwi), vsl, vwi], axis=1).astype(BF16)
    qk = qk_norm_w[l]
    tile = lambda v: jnp.tile(v, LANES // HEAD_DIM)[None, :]
    w1 = cmp_w1[l].astype(BF16)
    half = CMP_STRIDE * HEAD_DIM
    w2 = jnp.concatenate([cmp_w2[l], cmp_w2[l]], axis=-1).astype(BF16)
    pe = jnp.broadcast_to(cmp_pos[l].reshape(2, 1, CMP_LEN * HEAD_DIM), (2, 8, CMP_LEN * HEAD_DIM))
    return dict(
        nw1=norm1_w[l][None, :], wa=wa, wb=wb, gnw=ret_gn_w[l][None, :],
        qw=tile(qk[0]), kcw=tile(qk[1]), ksw=tile(qk[2]), kww=tile(qk[3]),
        w1a=w1[:, :half], w1b=w1[:, half:], w2=w2, pe=pe,
        wr=w_o_ret[l].astype(BF16), wn=w_o_nsa[l].astype(BF16), wo=w_out[l].astype(BF16),
        nw2=norm2_w[l][None, :], rwt=router_w[l].T, rb=router_b[l][:, None],
        bgu=b_gate_up[l][:, None, :], bd=b_down[l][:, None, :],
    )


def kernel(x, norm1_w, w_in, ret_gn_w, qk_norm_w, cmp_pos, cmp_w1, cmp_w2, w_o_ret, w_o_nsa, w_out, norm2_w,
           router_w, router_b, w_gate_up, b_gate_up, w_down, b_down):
    b, s, _ = x.shape
    depth = w_in.shape[0]
    n = b * s
    n_slc = s // SLC_BLOCK
    assert s % TC_RET == 0 and s % TK == 0 and s >= WIN_TILES * LANES and n % TM_PROJ == 0
    assert n_slc <= LANES
    assert s // CMP_STRIDE <= TK

    pos = jnp.arange(s)
    inv = RET_THETA ** (-jnp.arange(0, RET_DK, 2, dtype=F32) / RET_DK)
    ang = pos.astype(F32)[:, None] * inv[None, :]
    ret_cos = jnp.concatenate([jnp.cos(ang), jnp.cos(ang)], axis=1)
    ret_sin = jnp.concatenate([-jnp.sin(ang), jnp.sin(ang)], axis=1)
    dec, zeta_b, xi_b, chunk_decay = _ret_consts()
    ret_tabs = (ret_cos, ret_sin, jnp.asarray(dec), jnp.asarray(zeta_b), jnp.asarray(xi_b), chunk_decay)
    tok_tabs = _rope_half_tables(pos, ROPE_DIM, ROPE_THETA, HEAD_DIM)
    n_cmp = s // CMP_STRIDE
    cmp_tabs = _rope_half_tables(jnp.arange(n_cmp) * CMP_STRIDE + CMP_LEN - 1, ROPE_DIM, ROPE_THETA, HEAD_DIM)
    li = np.arange(LANES)
    bd = jnp.asarray((li[:, None] // HEAD_DIM == li[None, :] // HEAD_DIM).astype(np.float32), BF16)
    ci, sj = np.arange(n_cmp)[None, :], np.arange(LANES)[:, None]
    ovt = ((ci * CMP_STRIDE < (sj + 1) * SLC_BLOCK) & (ci * CMP_STRIDE + CMP_LEN > sj * SLC_BLOCK)
           & (ci < n_cmp - CMP_LEN // CMP_STRIDE + 1) & (sj < n_slc))
    ovt = jnp.asarray(ovt.astype(np.float32), BF16)
    ti = np.arange(TM_PROJ)
    tri = jnp.asarray((ti[:, None] < ti[None, :]).astype(np.float32), BF16)

    x2 = x.reshape(n, D_MODEL)
    for l in range(depth):
        p = _layer_weights(l, w_in, norm1_w, ret_gn_w, qk_norm_w, cmp_pos, cmp_w1, cmp_w2, w_o_ret, w_o_nsa,
                           w_out, norm2_w, router_w, router_b, w_gate_up, b_gate_up, w_down, b_down)
        pa, pb = _inproj(x2, p["nw1"], p["wa"], p["wb"])
        ret = _retention(pa, pb, p["gnw"], ret_tabs, b, s)
        q, ksa, kw, vst, vwt, gt = _nsa_prep(pa, pb, tok_tabs, p["qw"], p["ksw"], p["kww"], bd, b, s)
        c = pb[:, PB_C:PB_C + 2 * NSA_KV].reshape(b, n_cmp, CMP_STRIDE, 2, NSA_GROUPS, HEAD_DIM)
        c = c.transpose(3, 0, 4, 1, 2, 5).reshape(2, b, NSA_GROUPS, n_cmp, CMP_STRIDE * HEAD_DIM)
        kcmp = _compress(c[0], p["w1a"][0], p["w1b"][0], p["pe"][0], p["w2"][0], cmp_tabs, p["kcw"], True)
        vcmp_t = _compress(c[1], p["w1a"][1], p["w1b"][1], p["pe"][1], p["w2"][1], cmp_tabs, p["kcw"], False)
        att = _attention(q, kcmp, vcmp_t, ksa, vst, kw, vwt, gt, ovt, b, s)
        x2 = _outproj(x2, ret, att.reshape(n, NSA_Q), pa, p["wr"], p["wn"], p["wo"])
        h2, e_t, w_t, r_t, cnt = _router(x2, p["nw2"], p["rwt"], p["rb"], tri)
        dest_b, blk_e, nvalid, zinfo, rows = _route(e_t, r_t, cnt, n)
        xbuf = _dispatch(zinfo, dest_b, h2, rows)
        ybuf = _experts(blk_e, nvalid, xbuf, w_gate_up, p["bgu"], w_down, p["bd"], l)
        x2 = _combine(dest_b, x2, w_t[:TOP_K].T, ybuf)
    return x2.reshape(b, s, D_MODEL)
```

```python
import functools

import numpy as np
import jax
import jax.numpy as jnp
from jax import lax
from jax.experimental import pallas as pl
from jax.experimental.pallas import tpu as pltpu

F32 = jnp.float32
BF16 = jnp.bfloat16
I32 = jnp.int32

D_MODEL = 1024
RET_HEADS, RET_DK, RET_DV, RET_CHUNK, RET_THETA = 4, 128, 256, 128, 10000.0
NSA_HEADS, NSA_GROUPS, HEAD_DIM = 16, 2, 64
GROUP_HEADS = NSA_HEADS // NSA_GROUPS
ROPE_DIM, ROPE_THETA = HEAD_DIM // 4, 500000.0
CMP_LEN, CMP_STRIDE, CMP_HIDDEN = 32, 16, 4 * HEAD_DIM
SLC_BLOCK, SLC_TOPK, WINDOW = 64, 16, 512
N_EXPERTS, TOP_K, D_FF = 32, 4, D_MODEL
SWIGLU_LIMIT, SWIGLU_ALPHA = 7.0, 1.702
NORM_EPS = 1e-6
RET_QK, RET_V = RET_HEADS * RET_DK, RET_HEADS * RET_DV
NSA_Q, NSA_KV, NSA_GATE = NSA_HEADS * HEAD_DIM, NSA_GROUPS * HEAD_DIM, NSA_HEADS * 3

LANES = 128
NEG = -1e30
LOG2E = 1.4426950408889634
VMEM_LIMIT = 56 * 1024 * 1024

TM_PROJ = 256
TC_RET = 1024
TM_PREP = 512
TQ = 128
TK = 512
WIN_TILES = WINDOW // LANES + 1
VROWS = HEAD_DIM + 16
TM_MOE = 512
TM_CMB = 256
SUB = 8


def _tiled(nrows):
    return (nrows * SUB, LANES)

PA_MG, PA_RG, PA_NQ, PA_NG = 0, 2048, 3072, 4096
PA_W = 4224
PB_RQ, PB_RK, PB_RV, PB_C, PB_KS, PB_KW, PB_VS, PB_VW = 0, 512, 1024, 2048, 2304, 2560, 2816, 2944
PB_W = 3072


def _cparams(sem):
    return pltpu.CompilerParams(dimension_semantics=sem, vmem_limit_bytes=VMEM_LIMIT)


def _dot(a, b):
    return jnp.dot(a, b, preferred_element_type=F32)


def _dot_nt(a, b):
    return lax.dot_general(a, b, (((1,), (1,)), ((), ())), preferred_element_type=F32)


def _split_bf16(x):
    hi = x.astype(BF16)
    lo = (x - hi.astype(F32)).astype(BF16)
    return hi, lo


def _inproj_kernel(x_ref, nw_ref, wa_ref, wb_ref, pa_ref, pb_ref):
    x = x_ref[...]
    h = x * lax.rsqrt(jnp.mean(x * x, axis=-1, keepdims=True) + NORM_EPS) * nw_ref[...]
    h = h.astype(BF16)
    for c in range(0, PA_W, 512):
        w = min(512, PA_W - c)
        pa_ref[:, c:c + w] = _dot(h, wa_ref[:, c:c + w])
    for c in range(0, PB_W, 512):
        w = min(512, PB_W - c)
        pb_ref[:, c:c + w] = _dot(h, wb_ref[:, c:c + w]).astype(BF16)


def _inproj(x2, nw, wa, wb):
    n = x2.shape[0]
    return pl.pallas_call(
        _inproj_kernel,
        out_shape=(jax.ShapeDtypeStruct((n, PA_W), F32), jax.ShapeDtypeStruct((n, PB_W), BF16)),
        grid=(n // TM_PROJ,),
        in_specs=[
            pl.BlockSpec((TM_PROJ, D_MODEL), lambda i: (i, 0)),
            pl.BlockSpec((1, D_MODEL), lambda i: (0, 0)),
            pl.BlockSpec((D_MODEL, PA_W), lambda i: (0, 0)),
            pl.BlockSpec((D_MODEL, PB_W), lambda i: (0, 0)),
        ],
        out_specs=(pl.BlockSpec((TM_PROJ, PA_W), lambda i: (i, 0)),
                   pl.BlockSpec((TM_PROJ, PB_W), lambda i: (i, 0))),
        compiler_params=_cparams(("parallel",)),
        name="inproj",
    )(x2, nw, wa, wb)


def _outproj_kernel(x_ref, ret_ref, att_ref, mg_ref, wr_ref, wn_ref, wo_ref, o_ref):
    mg = mg_ref[...]
    a = jax.nn.sigmoid(mg[:, :D_MODEL]) * _dot(ret_ref[...], wr_ref[...])
    b = jax.nn.sigmoid(mg[:, D_MODEL:]) * _dot(att_ref[...], wn_ref[...])
    o_ref[...] = x_ref[...] + _dot((a + b).astype(BF16), wo_ref[...])


def _outproj(x2, ret, att, pa, wr, wn, wo):
    n = x2.shape[0]
    row = lambda i: (i, 0)
    full = lambda i: (0, 0)
    return pl.pallas_call(
        _outproj_kernel,
        out_shape=jax.ShapeDtypeStruct((n, D_MODEL), F32),
        grid=(n // TM_PROJ,),
        in_specs=[
            pl.BlockSpec((TM_PROJ, D_MODEL), row),
            pl.BlockSpec((TM_PROJ, RET_V), row),
            pl.BlockSpec((TM_PROJ, NSA_Q), row),
            pl.BlockSpec((TM_PROJ, 2 * D_MODEL), lambda i: (i, PA_MG // (2 * D_MODEL))),
            pl.BlockSpec((RET_V, D_MODEL), full),
            pl.BlockSpec((NSA_Q, D_MODEL), full),
            pl.BlockSpec((D_MODEL, D_MODEL), full),
        ],
        out_specs=pl.BlockSpec((TM_PROJ, D_MODEL), row),
        compiler_params=_cparams(("parallel",)),
        name="outproj",
    )(x2, ret, att, pa, wr, wn, wo)


def _ret_consts():
    h = np.arange(RET_HEADS, dtype=np.float32)
    log_g = np.log1p(-np.exp2(-5.0 - h)).astype(np.float32)
    n = np.arange(RET_CHUNK, dtype=np.float32)
    diff = n[:, None] - n[None, :]
    decay_in = np.where(diff >= 0, np.exp(log_g[:, None, None] * np.maximum(diff, 0.0)), 0.0).astype(np.float32)
    zeta = np.exp(log_g[:, None] * (RET_CHUNK - 1 - n)[None, :]).astype(np.float32)
    xi = np.exp(log_g[:, None] * (n + 1)[None, :]).astype(np.float32)
    chunk_decay = np.exp(log_g * RET_CHUNK).astype(np.float32)
    zeta_b = np.broadcast_to(zeta[:, :, None], (RET_HEADS, RET_CHUNK, RET_DK)).copy()
    xi_b = np.broadcast_to(xi[:, :, None], (RET_HEADS, RET_CHUNK, RET_DV)).copy()
    return decay_in, zeta_b, xi_b, [float(v) for v in chunk_decay]


def _rope_half_tables(pos, rot_dim, theta, period):
    inv = theta ** (-jnp.arange(0, rot_dim, 2, dtype=F32) / rot_dim)
    ang = pos.astype(F32)[:, None] * inv[None, :]
    c, s = jnp.cos(ang), jnp.sin(ang)
    half = rot_dim // 2
    p = pos.shape[0]
    one = jnp.ones((p, period - rot_dim), F32)
    zero_h = jnp.zeros((p, half), F32)
    zero_r = jnp.zeros((p, period - rot_dim), F32)
    cos_t = jnp.concatenate([c, c, one], axis=1)
    sm = jnp.concatenate([-s, zero_h, zero_r], axis=1)
    sp = jnp.concatenate([zero_h, s, zero_r], axis=1)
    rep = LANES // period
    return tuple(jnp.tile(t, (1, rep)) for t in (cos_t, sm, sp))


def _ret_kernel(q_ref, k_ref, v_ref, g_ref, cos_ref, sin_ref, dec_ref, zeta_ref, xi_ref, gnw_ref,
                o_ref, state_ref, *, chunk_decay):
    @pl.when(pl.program_id(1) == 0)
    def _():
        state_ref[...] = jnp.zeros_like(state_ref)

    n_chunks = TC_RET // RET_CHUNK
    for h in range(RET_HEADS):
        dec = dec_ref[h]
        zeta = zeta_ref[h]
        xi = xi_ref[h]
        gnw = gnw_ref[:, h * RET_DV:(h + 1) * RET_DV]
        for c in range(n_chunks):
            rows = slice(c * RET_CHUNK, (c + 1) * RET_CHUNK)
            cs = cos_ref[rows, :]
            sn = sin_ref[rows, :]
            q = q_ref[rows, h * RET_DK:(h + 1) * RET_DK].astype(F32)
            k = k_ref[rows, h * RET_DK:(h + 1) * RET_DK].astype(F32)
            q = q * cs + pltpu.roll(q, RET_DK // 2, 1) * sn
            k = (k * cs + pltpu.roll(k, RET_DK // 2, 1) * sn) * (RET_DK ** -0.5)
            v = v_ref[rows, h * RET_DV:(h + 1) * RET_DV]
            qb = q.astype(BF16)
            s = _dot_nt(qb, k.astype(BF16)) * dec
            inner = _dot(s.astype(BF16), v)
            r = state_ref[h]
            cross = _dot(qb, r.astype(BF16)) * xi
            kzt = (k * zeta).T.astype(BF16)
            state_ref[h] = chunk_decay[h] * r + _dot(kzt, v)
            o = inner + cross
            mu = jnp.mean(o, axis=-1, keepdims=True)
            d = o - mu
            var = jnp.mean(d * d, axis=-1, keepdims=True)
            on = d * lax.rsqrt(var + NORM_EPS) * gnw
            g = g_ref[rows, h * RET_DV:(h + 1) * RET_DV]
            o_ref[rows, h * RET_DV:(h + 1) * RET_DV] = (g * jax.nn.sigmoid(g) * on).astype(BF16)


def _retention(pa, pb, gnw, tabs, b, s):
    cos_t, sin_t, dec, zeta_b, xi_b, chunk_decay = tabs
    nt = s // TC_RET
    kern = functools.partial(_ret_kernel, chunk_decay=chunk_decay)
    tok = lambda w, j: pl.BlockSpec((TC_RET, w), lambda bi, si, j=j: (bi * nt + si, j))
    cst3 = lambda shp: pl.BlockSpec(shp, lambda bi, si: (0, 0, 0))
    return pl.pallas_call(
        kern,
        out_shape=jax.ShapeDtypeStruct((b * s, RET_V), BF16),
        grid=(b, nt),
        in_specs=[
            tok(RET_QK, PB_RQ // RET_QK),
            tok(RET_QK, PB_RK // RET_QK),
            tok(RET_V, PB_RV // RET_V),
            tok(RET_V, PA_RG // RET_V),
            pl.BlockSpec((TC_RET, LANES), lambda bi, si: (si, 0)),
            pl.BlockSpec((TC_RET, LANES), lambda bi, si: (si, 0)),
            cst3((RET_HEADS, RET_CHUNK, RET_CHUNK)),
            cst3((RET_HEADS, RET_CHUNK, RET_DK)),
            cst3((RET_HEADS, RET_CHUNK, RET_DV)),
            pl.BlockSpec((1, RET_V), lambda bi, si: (0, 0)),
        ],
        out_specs=pl.BlockSpec((TC_RET, RET_V), lambda bi, si: (bi * nt + si, 0)),
        scratch_shapes=[pltpu.VMEM((RET_HEADS, RET_DK, RET_DV), F32)],
        compiler_params=_cparams(("parallel", "arbitrary")),
        name="retention",
    )(pb, pb, pb, pa, cos_t, sin_t, dec, zeta_b, xi_b, gnw)


def _ones_row_pad(n):
    pad_rows = 16
    return jnp.where(lax.broadcasted_iota(I32, (pad_rows, n), 0) == 0, 1.0, 0.0)


def _rope16(x, cs, sm, sp):
    half = ROPE_DIM // 2
    return x * cs + pltpu.roll(x, LANES - half, 1) * sm + pltpu.roll(x, half, 1) * sp


def _prep_kernel(nq_ref, ks_ref, kw_ref, vs_ref, vw_ref, ng_ref, cs_ref, sm_ref, sp_ref,
                 qw_ref, ksw_ref, kww_ref, bd_ref,
                 q_ref, ksa_ref, kwo_ref, vst_ref, vwt_ref, gt_ref):
    cs, sm, sp = cs_ref[...], sm_ref[...], sp_ref[...]
    bd = bd_ref[...]
    for p in range(NSA_Q // LANES):
        x = nq_ref[:, p * LANES:(p + 1) * LANES]
        hi, lo = _split_bf16(x * x)
        ms = (_dot(hi, bd) + _dot(lo, bd)) * (1.0 / HEAD_DIM)
        y = x * lax.rsqrt(ms + NORM_EPS) * qw_ref[...]
        y = _rope16(y, cs, sm, sp) * (HEAD_DIM ** -0.5 * LOG2E)
        q_ref[:, p * LANES:(p + 1) * LANES] = y.astype(BF16)
    tok = pl.program_id(1) * TM_PREP + lax.broadcasted_iota(I32, (TM_PREP, LANES), 0)
    blk = lax.shift_right_logical(tok, int(np.log2(SLC_BLOCK)))
    onehot = jnp.where(lax.broadcasted_iota(I32, (TM_PREP, LANES), 1) == blk, 1.0, 0.0).astype(BF16)
    for g in range(NSA_GROUPS):
        sl = slice(g * LANES, (g + 1) * LANES)
        x = ks_ref[:, sl].astype(F32)
        y = x * lax.rsqrt(jnp.mean(x * x, axis=-1, keepdims=True) + NORM_EPS) * ksw_ref[...]
        ksa_ref[:, 2 * g * LANES:(2 * g + 1) * LANES] = _rope16(y, cs, sm, sp).astype(BF16)
        ksa_ref[:, (2 * g + 1) * LANES:(2 * g + 2) * LANES] = onehot
        x = kw_ref[:, sl].astype(F32)
        y = x * lax.rsqrt(jnp.mean(x * x, axis=-1, keepdims=True) + NORM_EPS) * kww_ref[...]
        kwo_ref[:, sl] = _rope16(y, cs, sm, sp).astype(BF16)
    vt = vs_ref[...].astype(F32).T
    wt = vw_ref[...].astype(F32).T
    for g in range(NSA_GROUPS):
        rows = slice(g * HEAD_DIM, (g + 1) * HEAD_DIM)
        vst_ref[g, 0:HEAD_DIM, :] = vt[rows, :].astype(BF16)
        vst_ref[g, HEAD_DIM:VROWS, :] = _ones_row_pad(TM_PREP).astype(BF16)
        for r in range(TM_PREP // LANES):
            vwt_ref[g, r, 0:HEAD_DIM, :] = wt[rows, r * LANES:(r + 1) * LANES].astype(BF16)
            vwt_ref[g, r, HEAD_DIM:VROWS, :] = _ones_row_pad(LANES).astype(BF16)
    gt_ref[...] = jax.nn.sigmoid(ng_ref[...]).T


def _nsa_prep(pa, pb, tabs, qw, ksw, kww, bd, b, s):
    cs, sm, sp = tabs
    nt = s // TM_PREP
    tokb = lambda w, j: pl.BlockSpec((TM_PREP, w), lambda bi, si, j=j: (bi * nt + si, j))
    tab = pl.BlockSpec((TM_PREP, LANES), lambda bi, si: (si, 0))
    vec = pl.BlockSpec((1, LANES), lambda bi, si: (0, 0))
    kv_w = NSA_GROUPS * LANES
    out_shape = (
        jax.ShapeDtypeStruct((b, s, NSA_Q), BF16),
        jax.ShapeDtypeStruct((b, s, 2 * kv_w), BF16),
        jax.ShapeDtypeStruct((b, s, kv_w), BF16),
        jax.ShapeDtypeStruct((b, NSA_GROUPS, nt, VROWS, TM_PREP), BF16),
        jax.ShapeDtypeStruct((b, NSA_GROUPS, s // LANES, VROWS, LANES), BF16),
        jax.ShapeDtypeStruct((b, LANES, s), F32),
    )
    out_specs = (
        pl.BlockSpec((None, TM_PREP, NSA_Q), lambda bi, si: (bi, si, 0)),
        pl.BlockSpec((None, TM_PREP, 2 * kv_w), lambda bi, si: (bi, si, 0)),
        pl.BlockSpec((None, TM_PREP, kv_w), lambda bi, si: (bi, si, 0)),
        pl.BlockSpec((None, NSA_GROUPS, None, VROWS, TM_PREP), lambda bi, si: (bi, 0, si, 0, 0)),
        pl.BlockSpec((None, NSA_GROUPS, TM_PREP // LANES, VROWS, LANES), lambda bi, si: (bi, 0, si, 0, 0)),
        pl.BlockSpec((None, LANES, TM_PREP), lambda bi, si: (bi, 0, si)),
    )
    return pl.pallas_call(
        _prep_kernel,
        out_shape=out_shape,
        grid=(b, nt),
        in_specs=[
            tokb(NSA_Q, PA_NQ // NSA_Q),
            tokb(kv_w, PB_KS // kv_w), tokb(kv_w, PB_KW // kv_w),
            tokb(NSA_KV, PB_VS // NSA_KV), tokb(NSA_KV, PB_VW // NSA_KV),
            tokb(LANES, PA_NG // LANES),
            tab, tab, tab, vec, vec, vec,
            pl.BlockSpec((LANES, LANES), lambda bi, si: (0, 0)),
        ],
        out_specs=out_specs,
        compiler_params=_cparams(("parallel", "parallel")),
        name="nsa_prep",
    )(pa, pb, pb, pb, pb, pa, cs, sm, sp, qw, ksw, kww, bd)


def _compress_kernel(c_ref, w1a_ref, w1b_ref, pe_ref, w2_ref, cs_ref, sm_ref, sp_ref, nw_ref, o_ref, *, is_key):
    c = c_ref[...]
    n = c.shape[0]
    a = _dot(c, w1a_ref[...])
    bb = _dot(c, w1b_ref[...])
    pe_hi, pe_lo = _split_bf16(pe_ref[...])
    w1 = jnp.concatenate([w1a_ref[...], w1b_ref[...]], axis=0)
    pe_term = (_dot(pe_hi, w1) + _dot(pe_lo, w1))[0:1, :]
    hid = a + pltpu.roll(bb, n - 1, 0) + pe_term
    t = hid * (0.7978845608028654 * (1.0 + 0.044715 * hid * hid))
    act = 0.5 * hid * (1.0 + jnp.tanh(t))
    y = _dot(act.astype(BF16), w2_ref[...])
    if is_key:
        y = y * lax.rsqrt(jnp.mean(y * y, axis=-1, keepdims=True) + NORM_EPS) * nw_ref[...]
        o_ref[...] = _rope16(y, cs_ref[...], sm_ref[...], sp_ref[...]).astype(BF16)
    else:
        o_ref[...] = jnp.concatenate([y.T[0:HEAD_DIM, :], _ones_row_pad(n)], axis=0).astype(BF16)


def _compress(cflat, w1a, w1b, pe, w2, tabs, nw, is_key):
    b, g, n, width = cflat.shape
    cs, sm, sp = tabs
    full2 = lambda shp: pl.BlockSpec(shp, lambda bi, gi: (0, 0))
    out_block = (None, None, n, LANES) if is_key else (None, None, VROWS, n)
    out_shape = (b, g, n, LANES) if is_key else (b, g, VROWS, n)
    return pl.pallas_call(
        functools.partial(_compress_kernel, is_key=is_key),
        out_shape=jax.ShapeDtypeStruct(out_shape, BF16),
        grid=(b, g),
        in_specs=[
            pl.BlockSpec((None, None, n, width), lambda bi, gi: (bi, gi, 0, 0)),
            full2(w1a.shape), full2(w1b.shape), full2(pe.shape), full2(w2.shape),
            full2(cs.shape), full2(sm.shape), full2(sp.shape), full2(nw.shape),
        ],
        out_specs=pl.BlockSpec(out_block, lambda bi, gi: (bi, gi, 0, 0)),
        compiler_params=_cparams(("parallel", "parallel")),
        name="compress_k" if is_key else "compress_v",
    )(cflat, w1a, w1b, pe, w2, cs, sm, sp, nw)


def _weighted_values(vt, e):
    acc = _dot(vt, e.astype(BF16))
    return acc[0:HEAD_DIM, :], acc[HEAD_DIM:HEAD_DIM + 1, :]


def _attn_kernel(q_ref, kc_ref, vct_ref, ksa_ref, vst_ref, kw_ref, vwt_ref, gt_ref, ovt_ref,
                 o_ref, s_ref, cm_ref, qa_ref):
    gi = pl.program_id(1)
    q0 = pl.program_id(2) * TQ
    n_cmp = kc_ref.shape[0]
    ncols = GROUP_HEADS * TQ

    low = lax.broadcasted_iota(I32, (TQ, LANES), 1) < HEAD_DIM
    zero = jnp.zeros((TQ, LANES), BF16)
    parts = []
    for h in range(GROUP_HEADS):
        slab = q_ref[:, (h // 2) * LANES:(h // 2 + 1) * LANES]
        parts.append(jnp.where(low, slab, zero) if h % 2 == 0 else jnp.where(low, zero, slab))
    qs = jnp.concatenate(parts, axis=0)
    t_row = q0 + lax.broadcasted_iota(I32, (1, TQ), 1)

    def all_heads(bias):
        return jnp.concatenate([bias] * GROUP_HEADS, axis=1)

    def produce(keys, queries, slot, keep_max):
        st = _dot_nt(keys, queries)
        s_ref[slot, 0:keys.shape[0], :] = st
        if keep_max:
            cm_ref[slot] = jnp.broadcast_to(jnp.max(st, axis=0, keepdims=True), (8, ncols))

    def consume(slot, nk, vt, carry, bias):
        m_old, a_old = carry
        st = s_ref[slot, 0:nk, :]
        if bias is None:
            cm = cm_ref[slot][0:1, :]
        else:
            st = st + all_heads(bias)
            cm = jnp.max(st, axis=0, keepdims=True)
        m_new = jnp.maximum(m_old, cm)
        alpha = jnp.exp2(m_old - m_new)
        e = jnp.exp2(st - m_new)
        return m_new, alpha * a_old + _dot(vt, e.astype(BF16))

    init = (jnp.full((1, ncols), NEG, F32), jnp.zeros((VROWS, ncols), F32))

    def finish(acc):
        return acc[0:HEAD_DIM, :] * (1.0 / acc[HEAD_DIM:HEAD_DIM + 1, :])

    w0 = jnp.maximum(q0 - WINDOW, 0)
    wt0 = lax.shift_right_logical(w0, int(np.log2(LANES)))
    n_wb = WIN_TILES * LANES - TK
    produce(kc_ref[...], qs, 1, False)
    produce(kw_ref[pl.ds(pl.multiple_of(w0, LANES), TK), :], qs, 0, False)

    cmp_end = lax.broadcasted_iota(I32, (n_cmp, 1), 0) * CMP_STRIDE + (CMP_LEN - 1)
    st = s_ref[1, 0:n_cmp, :] + all_heads(jnp.where(cmp_end <= t_row, 0.0, NEG))
    m = jnp.max(st, axis=0, keepdims=True)
    e = jnp.exp2(st - jnp.where(m > 0.5 * NEG, m, 0.0))
    oc_all, l = _weighted_values(vct_ref[...], e)
    rl = 1.0 / jnp.where(l > 0.0, l, 1.0)
    oc_all = oc_all * rl
    psum = e[:, 0:TQ] * rl[:, 0:TQ]
    for h in range(1, GROUP_HEADS):
        psum = psum + e[:, h * TQ:(h + 1) * TQ] * rl[:, h * TQ:(h + 1) * TQ]
    ph, plo = _split_bf16(psum)
    imp_t = _dot(ovt_ref[...], ph) + _dot(ovt_ref[...], plo)

    def wbias(start, nk):
        kpos = start + lax.broadcasted_iota(I32, (nk, 1), 0)
        return jnp.where((kpos <= t_row) & (kpos > t_row - WINDOW), 0.0, NEG)

    produce(kw_ref[pl.ds(pl.multiple_of(w0 + TK, LANES), n_wb), :], qs, 1, False)
    vwa = jnp.concatenate([vwt_ref[wt0 + r] for r in range(TK // LANES)], axis=1)
    carry_w = consume(0, TK, vwa, init, wbias(w0, TK))
    produce(ksa_ref[0:TK, 0:LANES], qs, 0, False)

    jrow = lax.broadcasted_iota(I32, (LANES, TQ), 0)
    cur = lax.shift_right_logical(q0 + lax.broadcasted_iota(I32, (LANES, TQ), 1), int(np.log2(SLC_BLOCK)))
    forced = (jrow == 0) | (jrow == cur) | (jrow == cur - 1)
    valid = jrow <= cur
    bias = jnp.where(forced & valid, 0.0, NEG)
    val = jnp.where(valid & jnp.logical_not(forced), imp_t, -jnp.inf)
    for _ in range(SLC_TOPK - 3):
        mx = jnp.max(val, axis=0, keepdims=True)
        first = jnp.min(jnp.where(val == mx, jrow, LANES), axis=0, keepdims=True)
        pick = jrow == first
        bias = jnp.where(pick & valid, 0.0, bias)
        val = jnp.where(pick, -jnp.inf, val)

    vwb = jnp.concatenate([vwt_ref[wt0 + TK // LANES + r] for r in range(n_wb // LANES)], axis=1)
    _, a_w = consume(1, n_wb, vwb, carry_w, wbias(w0 + TK, n_wb))
    ow_all = finish(a_w)

    selb = bias.T.astype(BF16)
    qa_ref[...] = jnp.concatenate([qs, jnp.concatenate([selb] * GROUP_HEADS, axis=0)], axis=1)

    def sel_scores(ti, slot):
        produce(ksa_ref[pl.ds(pl.multiple_of(ti * TK, TK), TK), :], qa_ref[...], slot, True)

    def sel_consume(ti, slot, carry, causal):
        cb = None
        if causal:
            cb = jnp.where(ti * TK + lax.broadcasted_iota(I32, (TK, 1), 0) <= t_row, 0.0, NEG)
        return consume(slot, TK, vst_ref[ti], carry, cb)

    n_full = lax.shift_right_logical(q0, int(np.log2(TK)))
    rows_per_tile = TK // SLC_BLOCK
    bias0 = jnp.concatenate([jnp.broadcast_to(bias[r:r + 1, :], (SLC_BLOCK, TQ)) for r in range(rows_per_tile)],
                            axis=0)
    st0 = s_ref[0] + all_heads(bias0)
    s_ref[0] = st0
    cm_ref[0] = jnp.broadcast_to(jnp.max(st0, axis=0, keepdims=True), (8, ncols))

    def two_tiles(j, carry):
        sel_scores(2 * j + 1, 1)
        carry = sel_consume(2 * j, 0, carry, False)
        sel_scores(2 * j + 2, 0)
        return sel_consume(2 * j + 1, 1, carry, False)

    carry = lax.fori_loop(0, lax.shift_right_logical(n_full, 1), two_tiles, init)

    def odd_tail(c):
        sel_scores(n_full, 1)
        return sel_consume(n_full, 1, sel_consume(n_full - 1, 0, c, False), True)

    _, a_s = lax.cond((n_full & 1) == 1, odd_tail, lambda c: sel_consume(n_full, 0, c, True), carry)
    os_all = finish(a_s)

    for p in range(GROUP_HEADS // 2):
        halves = []
        for h in (2 * p, 2 * p + 1):
            c = slice(h * TQ, (h + 1) * TQ)
            gbase = (gi * GROUP_HEADS + h) * 3
            halves.append(gt_ref[pl.ds(gbase, 1), :] * oc_all[:, c]
                          + gt_ref[pl.ds(gbase + 1, 1), :] * os_all[:, c]
                          + gt_ref[pl.ds(gbase + 2, 1), :] * ow_all[:, c])
        o_ref[:, p * LANES:(p + 1) * LANES] = jnp.concatenate(halves, axis=0).T.astype(BF16)


def _attention(q, kc, vct, ksa, vst, kw, vwt, gt, ovt, b, s):
    nq = s // TQ
    n_cmp = kc.shape[2]
    gw = GROUP_HEADS * HEAD_DIM
    ncols = GROUP_HEADS * TQ
    return pl.pallas_call(
        _attn_kernel,
        out_shape=jax.ShapeDtypeStruct((b, s, NSA_Q), BF16),
        grid=(b, NSA_GROUPS, nq),
        in_specs=[
            pl.BlockSpec((None, TQ, gw), lambda bi, gi, qi: (bi, qi, gi)),
            pl.BlockSpec((None, None, n_cmp, LANES), lambda bi, gi, qi: (bi, gi, 0, 0)),
            pl.BlockSpec((None, None, VROWS, n_cmp), lambda bi, gi, qi: (bi, gi, 0, 0)),
            pl.BlockSpec((None, s, 2 * LANES), lambda bi, gi, qi: (bi, 0, gi)),
            pl.BlockSpec((None, None, s // TK, VROWS, TK), lambda bi, gi, qi: (bi, gi, 0, 0, 0)),
            pl.BlockSpec((None, s, LANES), lambda bi, gi, qi: (bi, 0, gi)),
            pl.BlockSpec((None, None, s // LANES, VROWS, LANES), lambda bi, gi, qi: (bi, gi, 0, 0, 0)),
            pl.BlockSpec((None, LANES, TQ), lambda bi, gi, qi: (bi, 0, qi)),
            pl.BlockSpec((LANES, n_cmp), lambda bi, gi, qi: (0, 0)),
        ],
        out_specs=pl.BlockSpec((None, TQ, gw), lambda bi, gi, qi: (bi, qi, gi)),
        scratch_shapes=[
            pltpu.VMEM((2, TK, ncols), F32),
            pltpu.VMEM((2, 8, ncols), F32),
            pltpu.VMEM((ncols, 2 * LANES), BF16),
        ],
        compiler_params=_cparams(("parallel", "parallel", "arbitrary")),
        name="nsa_attention",
    )(q, kc, vct, ksa, vst, kw, vwt, gt, ovt)


def _router_kernel(x_ref, nw_ref, wt_ref, b_ref, tri_ref, h_ref, e_ref, w_ref, r_ref, cnt_ref, base_ref):
    @pl.when(pl.program_id(0) == 0)
    def _():
        base_ref[...] = jnp.zeros_like(base_ref)

    x = x_ref[...]
    h = x * lax.rsqrt(jnp.mean(x * x, axis=-1, keepdims=True) + NORM_EPS) * nw_ref[...]
    _store_row_tiles(h_ref, h)
    h_hi, h_lo = _split_bf16(h)
    w_hi, w_lo = _split_bf16(wt_ref[...])
    lg = _dot_nt(w_hi, h_hi) + _dot_nt(w_hi, h_lo) + _dot_nt(w_lo, h_hi) + b_ref[...]
    erow = lax.broadcasted_iota(I32, lg.shape, 0)
    vals, hots = [], []
    for _ in range(TOP_K):
        mx = jnp.max(lg, axis=0, keepdims=True)
        first = jnp.min(jnp.where(lg == mx, erow, N_EXPERTS), axis=0, keepdims=True)
        hot = erow == first
        vals.append(mx)
        hots.append(hot)
        lg = jnp.where(hot, -jnp.inf, lg)
    ex = [jnp.exp(v - vals[0]) for v in vals]
    den = ex[0] + ex[1] + ex[2] + ex[3]
    onehots = [h.astype(F32) for h in hots]
    cnt = onehots[0] + onehots[1] + onehots[2] + onehots[3]
    pref = _dot(cnt.astype(BF16), tri_ref[...]) + base_ref[:, 0:1]
    erow_f = erow.astype(F32)
    idxs = [jnp.sum(oh * erow_f, axis=0, keepdims=True) for oh in onehots]
    ranks = [jnp.sum(oh * pref, axis=0, keepdims=True) for oh in onehots]
    base_ref[...] = base_ref[...] + jnp.sum(cnt, axis=1, keepdims=True)
    cnt_ref[...] = base_ref[...]
    pad_f = [jnp.zeros_like(den)] * (8 - TOP_K)
    e_ref[...] = jnp.concatenate(idxs + pad_f, axis=0).astype(I32)
    r_ref[...] = jnp.concatenate(ranks + pad_f, axis=0).astype(I32)
    w_ref[...] = jnp.concatenate([e / den for e in ex] + pad_f, axis=0)


def _router(x2, nw, wt, bcol, tri):
    n = x2.shape[0]
    small = pl.BlockSpec((8, TM_PROJ), lambda i: (0, i))
    return pl.pallas_call(
        _router_kernel,
        out_shape=(jax.ShapeDtypeStruct(_tiled(n), F32),
                   jax.ShapeDtypeStruct((8, n), I32),
                   jax.ShapeDtypeStruct((8, n), F32),
                   jax.ShapeDtypeStruct((8, n), I32),
                   jax.ShapeDtypeStruct((N_EXPERTS, LANES), F32)),
        grid=(n // TM_PROJ,),
        in_specs=[
            pl.BlockSpec((TM_PROJ, D_MODEL), lambda i: (i, 0)),
            pl.BlockSpec((1, D_MODEL), lambda i: (0, 0)),
            pl.BlockSpec((N_EXPERTS, D_MODEL), lambda i: (0, 0)),
            pl.BlockSpec((N_EXPERTS, 1), lambda i: (0, 0)),
            pl.BlockSpec((TM_PROJ, TM_PROJ), lambda i: (0, 0)),
        ],
        out_specs=(pl.BlockSpec(_tiled(TM_PROJ), lambda i: (i, 0)), small, small, small,
                   pl.BlockSpec((N_EXPERTS, LANES), lambda i: (0, 0))),
        scratch_shapes=[pltpu.VMEM((N_EXPERTS, LANES), F32)],
        compiler_params=_cparams(("arbitrary",)),
        name="router",
    )(x2, nw, wt, bcol, tri)


def _row_copy(src, src_row, dst, dst_row, sem):
    tile = lambda ref, r: ref.at[pl.ds(pl.multiple_of(r * SUB, SUB), SUB), :]
    return pltpu.make_async_copy(tile(src, src_row), tile(dst, dst_row), sem)


def _issue_rows(nrow, copy_of):
    def body(g, c):
        for j in range(8):
            copy_of(g * 8 + j).start(priority=j % 2)
        return c

    lax.fori_loop(0, nrow // 8, body, 0)


def _wait_rows(nrow, copy_of):
    def body(g, c):
        for j in range(8):
            copy_of(0).wait()
        return c

    lax.fori_loop(0, nrow // 8, body, 0)


def _store_row_tiles(ref, x):
    for c in range(SUB):
        ref[pl.ds(c, x.shape[0], stride=SUB), :] = x[:, c * LANES:(c + 1) * LANES]


def _load_row_tiles(ref, first_row, nrows):
    return jnp.concatenate([ref[pl.ds(first_row * SUB + c, nrows, stride=SUB), :] for c in range(SUB)], axis=1)


def _dispatch_kernel(zinfo_ref, dest_ref, h_ref, x_hbm, zbuf, sem, zsem, *, n_blocks):
    def zero_block(row):
        return pltpu.make_async_copy(
            zbuf, x_hbm.at[pl.ds(pl.multiple_of(row * SUB, TM_MOE * SUB), TM_MOE * SUB), :], zsem)

    @pl.when(pl.program_id(0) == 0)
    def _():
        zbuf[...] = jnp.zeros_like(zbuf)
        for e in range(N_EXPERTS):
            zero_block(zinfo_ref[e]).start()
        for e in range(N_EXPERTS):
            zero_block(0).wait()
        nvalid = zinfo_ref[N_EXPERTS]

        def ztail(bk, c):
            zero_block(bk * TM_MOE).start()
            return c

        lax.fori_loop(nvalid, n_blocks, ztail, 0)

        def zwait(bk, c):
            zero_block(0).wait()
            return c

        lax.fori_loop(nvalid, n_blocks, zwait, 0)

    nrow = TOP_K * TM_CMB
    _issue_rows(nrow, lambda r: _row_copy(h_ref, r & (TM_CMB - 1), x_hbm, dest_ref[0, 0, r], sem))
    _wait_rows(nrow, lambda r: _row_copy(h_ref, 0, x_hbm, 0, sem))


def _dispatch(zinfo, dest_b, h3, rows):
    n = h3.shape[0] // SUB
    grid_spec = pltpu.PrefetchScalarGridSpec(
        num_scalar_prefetch=1,
        grid=(n // TM_CMB,),
        in_specs=[
            pl.BlockSpec((1, 1, TOP_K * TM_CMB), lambda i, z: (i, 0, 0), memory_space=pltpu.SMEM),
            pl.BlockSpec(_tiled(TM_CMB), lambda i, z: (i, 0)),
        ],
        out_specs=pl.BlockSpec(memory_space=pl.ANY),
        scratch_shapes=[pltpu.VMEM(_tiled(TM_MOE), F32), pltpu.SemaphoreType.DMA(()),
                        pltpu.SemaphoreType.DMA(())],
    )
    return pl.pallas_call(
        functools.partial(_dispatch_kernel, n_blocks=rows // TM_MOE),
        out_shape=jax.ShapeDtypeStruct(_tiled(rows), F32),
        grid_spec=grid_spec,
        compiler_params=_cparams(("arbitrary",)),
        name="dispatch",
    )(zinfo, dest_b, h3)


def _expert_kernel(blk_e_ref, nvalid_ref, x_ref, wgu_ref, bgu_ref, wd_ref, bd_ref, y_ref, wgu_bf, wd_bf):
    i = pl.program_id(0)
    used = i < nvalid_ref[0]

    @pl.when(jnp.logical_not(used))
    def _():
        y_ref[...] = jnp.zeros_like(y_ref)

    @pl.when(used & ((i == 0) | (blk_e_ref[i] != blk_e_ref[jnp.maximum(i - 1, 0)])))
    def _():
        wgu_bf[...] = wgu_ref[...].astype(BF16)
        wd_bf[...] = wd_ref[...].astype(BF16)

    @pl.when(used)
    def _():
        gu = _dot(_load_row_tiles(x_ref, 0, TM_MOE).astype(BF16), wgu_bf[...]) + bgu_ref[...]
        gate = jnp.minimum(gu[:, :D_FF], SWIGLU_LIMIT)
        up = jnp.clip(gu[:, D_FF:], -SWIGLU_LIMIT, SWIGLU_LIMIT)
        glu = gate * jax.nn.sigmoid(gate * SWIGLU_ALPHA)
        _store_row_tiles(y_ref, _dot(((up + 1.0) * glu).astype(BF16), wd_bf[...]) + bd_ref[...])


def _experts(blk_e, nvalid, xbuf, wgu, bgu, wd, bd, layer):
    n_blocks = blk_e.shape[0]
    wblk = lambda i, be, nv: (be[i], 0, 0)
    wblk4 = lambda i, be, nv: (layer, be[i], 0, 0)
    grid_spec = pltpu.PrefetchScalarGridSpec(
        num_scalar_prefetch=2,
        grid=(n_blocks,),
        in_specs=[
            pl.BlockSpec(_tiled(TM_MOE), lambda i, be, nv: (jnp.minimum(i, nv[0] - 1), 0)),
            pl.BlockSpec((None, None, D_MODEL, 2 * D_FF), wblk4),
            pl.BlockSpec((None, 1, 2 * D_FF), wblk),
            pl.BlockSpec((None, None, D_FF, D_MODEL), wblk4),
            pl.BlockSpec((None, 1, D_MODEL), wblk),
        ],
        out_specs=pl.BlockSpec(_tiled(TM_MOE), lambda i, be, nv: (i, 0)),
        scratch_shapes=[pltpu.VMEM((D_MODEL, 2 * D_FF), BF16), pltpu.VMEM((D_FF, D_MODEL), BF16)],
    )
    return pl.pallas_call(
        _expert_kernel,
        out_shape=jax.ShapeDtypeStruct(_tiled(n_blocks * TM_MOE), F32),
        grid_spec=grid_spec,
        compiler_params=_cparams(("arbitrary",)),
        name="experts",
    )(blk_e, nvalid, xbuf, wgu, bgu, wd, bd)


def _combine_kernel(dcur_ref, dnext_ref, x_ref, w_ref, y_hbm, o_ref, ybuf, sem):
    i = pl.program_id(0)
    nrow = TOP_K * TM_CMB
    slot = i & 1

    def issue(d_ref, s):
        _issue_rows(nrow, lambda r: _row_copy(y_hbm, d_ref[0, 0, r], ybuf.at[s], r, sem.at[s]))

    @pl.when(i == 0)
    def _():
        issue(dcur_ref, 0)

    @pl.when(i + 1 < pl.num_programs(0))
    def _():
        issue(dnext_ref, 1 - slot)

    _wait_rows(nrow, lambda r: _row_copy(y_hbm, 0, ybuf.at[slot], 0, sem.at[slot]))
    acc = x_ref[...]
    w = w_ref[...]
    for k in range(TOP_K):
        acc = acc + w[:, k:k + 1] * _load_row_tiles(ybuf.at[slot], k * TM_CMB, TM_CMB)
    o_ref[...] = acc


def _combine(dest_b, x2, wts, ybuf):
    n = x2.shape[0]
    nt = n // TM_CMB
    return pl.pallas_call(
        _combine_kernel,
        out_shape=jax.ShapeDtypeStruct((n, D_MODEL), F32),
        grid=(nt,),
        in_specs=[
            pl.BlockSpec((1, 1, TOP_K * TM_CMB), lambda i: (i, 0, 0), memory_space=pltpu.SMEM),
            pl.BlockSpec((1, 1, TOP_K * TM_CMB), lambda i: (jnp.minimum(i + 1, nt - 1), 0, 0),
                         memory_space=pltpu.SMEM),
            pl.BlockSpec((TM_CMB, D_MODEL), lambda i: (i, 0)),
            pl.BlockSpec((TM_CMB, TOP_K), lambda i: (i, 0)),
            pl.BlockSpec(memory_space=pl.ANY),
        ],
        out_specs=pl.BlockSpec((TM_CMB, D_MODEL), lambda i: (i, 0)),
        scratch_shapes=[pltpu.VMEM((2,) + _tiled(TOP_K * TM_CMB), F32), pltpu.SemaphoreType.DMA((2,))],
        compiler_params=_cparams(("arbitrary",)),
        name="combine",
    )(dest_b, dest_b, x2, wts, ybuf)


def _route(e_t, r_t, cnt, n):
    counts = cnt[:, 0].astype(I32)
    padded = (counts + TM_MOE - 1) // TM_MOE * TM_MOE
    pend = jnp.cumsum(padded)
    pstart = pend - padded
    n_blocks = n * TOP_K // TM_MOE + N_EXPERTS
    rows = n_blocks * TM_MOE
    dest = r_t[:TOP_K]
    for e in range(N_EXPERTS):
        dest = dest + jnp.where(e_t[:TOP_K] == e, pstart[e].astype(I32), 0)
    nt = n // TM_CMB
    dest_b = dest.reshape(TOP_K, nt, TM_CMB).transpose(1, 0, 2).reshape(nt, 1, TOP_K * TM_CMB)
    blk_row = jnp.arange(n_blocks, dtype=I32)[:, None] * TM_MOE
    blk_e = jnp.minimum(jnp.sum((pend[None, :] <= blk_row).astype(I32), axis=1), N_EXPERTS - 1)
    nvalid = (pend[-1:] // TM_MOE).astype(I32)
    zrow = jnp.clip(pstart + padded - TM_MOE, 0, rows - TM_MOE).astype(I32)
    return dest_b, blk_e, nvalid, jnp.concatenate([zrow, nvalid]), rows


def _layer_weights(l, w_in, norm1_w, ret_gn_w, qk_norm_w, cmp_pos, cmp_w1, cmp_w2, w_o_ret, w_o_nsa, w_out,
                   norm2_w, router_w, router_b, w_gate_up, b_gate_up, w_down, b_down):
    w = w_in[l]
    o = np.cumsum((0, RET_QK, RET_QK, RET_V, RET_V, NSA_Q) + (NSA_KV,) * 6 + (NSA_GATE, 2 * D_MODEL))
    rq, rk, rv, rg, nq = (w[:, o[i]:o[i + 1]] for i in range(5))
    kc, vc, ksl, vsl, kwi, vwi = (w[:, o[5 + i]:o[6 + i]] for i in range(6))
    ng, mg = w[:, o[11]:o[12]], w[:, o[12]:o[13]]

    def dup(t):
        t = t.reshape(D_MODEL, NSA_GROUPS, 1, HEAD_DIM)
        return jnp.broadcast_to(t, (D_MODEL, NSA_GROUPS, 2, HEAD_DIM)).reshape(D_MODEL, NSA_GROUPS * LANES)

    ng_pad = jnp.pad(ng, ((0, 0), (0, LANES - NSA_GATE)))
    wa = jnp.concatenate([mg, rg, nq, ng_pad], axis=1).astype(BF16)
    wb = jnp.concatenate([rq, rk, rv, kc, vc, dup(ksl), dup(kwi), vsl, vwi], axis=1).astype(BF16)
    qk = qk_norm_w[l]
    tile = lambda v: jnp.tile(v, LANES // HEAD_DIM)[None, :]
    w1 = cmp_w1[l].astype(BF16)
    half = CMP_STRIDE * HEAD_DIM
    w2 = jnp.concatenate([cmp_w2[l], cmp_w2[l]], axis=-1).astype(BF16)
    pe = jnp.broadcast_to(cmp_pos[l].reshape(2, 1, CMP_LEN * HEAD_DIM), (2, 8, CMP_LEN * HEAD_DIM))
    return dict(
        nw1=norm1_w[l][None, :], wa=wa, wb=wb, gnw=ret_gn_w[l][None, :],
        qw=tile(qk[0]), kcw=tile(qk[1]), ksw=tile(qk[2]), kww=tile(qk[3]),
        w1a=w1[:, :half], w1b=w1[:, half:], w2=w2, pe=pe,
        wr=w_o_ret[l].astype(BF16), wn=w_o_nsa[l].astype(BF16), wo=w_out[l].astype(BF16),
        nw2=norm2_w[l][None, :], rwt=router_w[l].T, rb=router_b[l][:, None],
        bgu=b_gate_up[l][:, None, :], bd=b_down[l][:, None, :],
    )


def kernel(x, norm1_w, w_in, ret_gn_w, qk_norm_w, cmp_pos, cmp_w1, cmp_w2, w_o_ret, w_o_nsa, w_out, norm2_w,
           router_w, router_b, w_gate_up, b_gate_up, w_down, b_down):
    b, s, _ = x.shape
    depth = w_in.shape[0]
    n = b * s
    n_slc = s // SLC_BLOCK
    assert s % TC_RET == 0 and s % TK == 0 and s >= WIN_TILES * LANES and n % TM_PROJ == 0
    assert n_slc <= LANES
    assert s // CMP_STRIDE <= TK

    pos = jnp.arange(s)
    inv = RET_THETA ** (-jnp.arange(0, RET_DK, 2, dtype=F32) / RET_DK)
    ang = pos.astype(F32)[:, None] * inv[None, :]
    ret_cos = jnp.concatenate([jnp.cos(ang), jnp.cos(ang)], axis=1)
    ret_sin = jnp.concatenate([-jnp.sin(ang), jnp.sin(ang)], axis=1)
    dec, zeta_b, xi_b, chunk_decay = _ret_consts()
    ret_tabs = (ret_cos, ret_sin, jnp.asarray(dec), jnp.asarray(zeta_b), jnp.asarray(xi_b), chunk_decay)
    tok_tabs = _rope_half_tables(pos, ROPE_DIM, ROPE_THETA, HEAD_DIM)
    n_cmp = s // CMP_STRIDE
    cmp_tabs = _rope_half_tables(jnp.arange(n_cmp) * CMP_STRIDE + CMP_LEN - 1, ROPE_DIM, ROPE_THETA, HEAD_DIM)
    li = np.arange(LANES)
    bd = jnp.asarray((li[:, None] // HEAD_DIM == li[None, :] // HEAD_DIM).astype(np.float32), BF16)
    ci, sj = np.arange(n_cmp)[None, :], np.arange(LANES)[:, None]
    ovt = ((ci * CMP_STRIDE < (sj + 1) * SLC_BLOCK) & (ci * CMP_STRIDE + CMP_LEN > sj * SLC_BLOCK)
           & (ci < n_cmp - CMP_LEN // CMP_STRIDE + 1) & (sj < n_slc))
    ovt = jnp.asarray(ovt.astype(np.float32), BF16)
    ti = np.arange(TM_PROJ)
    tri = jnp.asarray((ti[:, None] < ti[None, :]).astype(np.float32), BF16)

    x2 = x.reshape(n, D_MODEL)
    for l in range(depth):
        p = _layer_weights(l, w_in, norm1_w, ret_gn_w, qk_norm_w, cmp_pos, cmp_w1, cmp_w2, w_o_ret, w_o_nsa,
                           w_out, norm2_w, router_w, router_b, w_gate_up, b_gate_up, w_down, b_down)
        pa, pb = _inproj(x2, p["nw1"], p["wa"], p["wb"])
        ret = _retention(pa, pb, p["gnw"], ret_tabs, b, s)
        q, ksa, kw, vst, vwt, gt = _nsa_prep(pa, pb, tok_tabs, p["qw"], p["ksw"], p["kww"], bd, b, s)
        c = pb[:, PB_C:PB_C + 2 * NSA_KV].reshape(b, n_cmp, CMP_STRIDE, 2, NSA_GROUPS, HEAD_DIM)
        c = c.transpose(3, 0, 4, 1, 2, 5).reshape(2, b, NSA_GROUPS, n_cmp, CMP_STRIDE * HEAD_DIM)
        kcmp = _compress(c[0], p["w1a"][0], p["w1b"][0], p["pe"][0], p["w2"][0], cmp_tabs, p["kcw"], True)
        vcmp_t = _compress(c[1], p["w1a"][1], p["w1b"][1], p["pe"][1], p["w2"][1], cmp_tabs, p["kcw"], False)
        att = _attention(q, kcmp, vcmp_t, ksa, vst, kw, vwt, gt, ovt, b, s)
        x2 = _outproj(x2, ret, att.reshape(n, NSA_Q), pa, p["wr"], p["wn"], p["wo"])
        h2, e_t, w_t, r_t, cnt = _router(x2, p["nw2"], p["rwt"], p["rb"], tri)
        dest_b, blk_e, nvalid, zinfo, rows = _route(e_t, r_t, cnt, n)
        xbuf = _dispatch(zinfo, dest_b, h2, rows)
        ybuf = _experts(blk_e, nvalid, xbuf, w_gate_up, p["bgu"], w_down, p["bd"], l)
        x2 = _combine(dest_b, x2, w_t[:TOP_K].T, ybuf)
    return x2.reshape(b, s, D_MODEL)
```

```python
import functools

import numpy as np
import jax
import jax.numpy as jnp
from jax import lax
from jax.experimental import pallas as pl
from jax.experimental.pallas import tpu as pltpu

F32 = jnp.float32
BF16 = jnp.bfloat16
I32 = jnp.int32

D_MODEL = 1024
RET_HEADS, RET_DK, RET_DV, RET_CHUNK, RET_THETA = 4, 128, 256, 128, 10000.0
NSA_HEADS, NSA_GROUPS, HEAD_DIM = 16, 2, 64
GROUP_HEADS = NSA_HEADS // NSA_GROUPS
ROPE_DIM, ROPE_THETA = HEAD_DIM // 4, 500000.0
CMP_LEN, CMP_STRIDE, CMP_HIDDEN = 32, 16, 4 * HEAD_DIM
SLC_BLOCK, SLC_TOPK, WINDOW = 64, 16, 512
N_EXPERTS, TOP_K, D_FF = 32, 4, D_MODEL
SWIGLU_LIMIT, SWIGLU_ALPHA = 7.0, 1.702
NORM_EPS = 1e-6
RET_QK, RET_V = RET_HEADS * RET_DK, RET_HEADS * RET_DV
NSA_Q, NSA_KV, NSA_GATE = NSA_HEADS * HEAD_DIM, NSA_GROUPS * HEAD_DIM, NSA_HEADS * 3

LANES = 128
NEG = -1e30
LOG2E = 1.4426950408889634
VMEM_LIMIT = 56 * 1024 * 1024

TM_PROJ = 256
TC_RET = 1024
TM_PREP = 512
TQ = 128
TK = 512
WIN_TILES = WINDOW // LANES + 1
VROWS = HEAD_DIM + 16
TM_MOE = 512
TM_CMB = 256
SUB = 8


def _tiled(nrows):
    return (nrows * SUB, LANES)

PA_MG, PA_RG, PA_NQ, PA_C, PA_NG = 0, 2048, 3072, 4096, 4352
PA_W = 4480
PB_RQ, PB_RK, PB_RV, PB_KS, PB_KW, PB_VS, PB_VW = 0, 512, 1024, 2048, 2304, 2560, 2688
PB_W = 2816


def _cparams(sem):
    return pltpu.CompilerParams(dimension_semantics=sem, vmem_limit_bytes=VMEM_LIMIT)


def _dot(a, b):
    return jnp.dot(a, b, preferred_element_type=F32)


def _dot_nt(a, b):
    return lax.dot_general(a, b, (((1,), (1,)), ((), ())), preferred_element_type=F32)


def _split_bf16(x):
    hi = x.astype(BF16)
    lo = (x - hi.astype(F32)).astype(BF16)
    return hi, lo


def _inproj_kernel(x_ref, nw_ref, wa_ref, wb_ref, pa_ref, pb_ref):
    x = x_ref[...]
    h = x * lax.rsqrt(jnp.mean(x * x, axis=-1, keepdims=True) + NORM_EPS) * nw_ref[...]
    h = h.astype(BF16)
    for c in range(0, PA_W, 512):
        w = min(512, PA_W - c)
        pa_ref[:, c:c + w] = _dot(h, wa_ref[:, c:c + w])
    for c in range(0, PB_W, 512):
        w = min(512, PB_W - c)
        pb_ref[:, c:c + w] = _dot(h, wb_ref[:, c:c + w]).astype(BF16)


def _inproj(x2, nw, wa, wb):
    n = x2.shape[0]
    return pl.pallas_call(
        _inproj_kernel,
        out_shape=(jax.ShapeDtypeStruct((n, PA_W), F32), jax.ShapeDtypeStruct((n, PB_W), BF16)),
        grid=(n // TM_PROJ,),
        in_specs=[
            pl.BlockSpec((TM_PROJ, D_MODEL), lambda i: (i, 0)),
            pl.BlockSpec((1, D_MODEL), lambda i: (0, 0)),
            pl.BlockSpec((D_MODEL, PA_W), lambda i: (0, 0)),
            pl.BlockSpec((D_MODEL, PB_W), lambda i: (0, 0)),
        ],
        out_specs=(pl.BlockSpec((TM_PROJ, PA_W), lambda i: (i, 0)),
                   pl.BlockSpec((TM_PROJ, PB_W), lambda i: (i, 0))),
        compiler_params=_cparams(("parallel",)),
        name="inproj",
    )(x2, nw, wa, wb)


def _outproj_kernel(x_ref, ret_ref, att_ref, mg_ref, wr_ref, wn_ref, wo_ref, o_ref):
    mg = mg_ref[...]
    a = jax.nn.sigmoid(mg[:, :D_MODEL]) * _dot(ret_ref[...], wr_ref[...])
    b = jax.nn.sigmoid(mg[:, D_MODEL:]) * _dot(att_ref[...], wn_ref[...])
    o_ref[...] = x_ref[...] + _dot((a + b).astype(BF16), wo_ref[...])


def _outproj(x2, ret, att, pa, wr, wn, wo):
    n = x2.shape[0]
    row = lambda i: (i, 0)
    full = lambda i: (0, 0)
    return pl.pallas_call(
        _outproj_kernel,
        out_shape=jax.ShapeDtypeStruct((n, D_MODEL), F32),
        grid=(n // TM_PROJ,),
        in_specs=[
            pl.BlockSpec((TM_PROJ, D_MODEL), row),
            pl.BlockSpec((TM_PROJ, RET_V), row),
            pl.BlockSpec((TM_PROJ, NSA_Q), row),
            pl.BlockSpec((TM_PROJ, 2 * D_MODEL), lambda i: (i, PA_MG // (2 * D_MODEL))),
            pl.BlockSpec((RET_V, D_MODEL), full),
            pl.BlockSpec((NSA_Q, D_MODEL), full),
            pl.BlockSpec((D_MODEL, D_MODEL), full),
        ],
        out_specs=pl.BlockSpec((TM_PROJ, D_MODEL), row),
        compiler_params=_cparams(("parallel",)),
        name="outproj",
    )(x2, ret, att, pa, wr, wn, wo)


def _ret_consts():
    h = np.arange(RET_HEADS, dtype=np.float32)
    log_g = np.log1p(-np.exp2(-5.0 - h)).astype(np.float32)
    n = np.arange(RET_CHUNK, dtype=np.float32)
    diff = n[:, None] - n[None, :]
    decay_in = np.where(diff >= 0, np.exp(log_g[:, None, None] * np.maximum(diff, 0.0)), 0.0).astype(np.float32)
    zeta = np.exp(log_g[:, None] * (RET_CHUNK - 1 - n)[None, :]).astype(np.float32)
    xi = np.exp(log_g[:, None] * (n + 1)[None, :]).astype(np.float32)
    chunk_decay = np.exp(log_g * RET_CHUNK).astype(np.float32)
    zeta_b = np.broadcast_to(zeta[:, :, None], (RET_HEADS, RET_CHUNK, RET_DK)).copy()
    xi_b = np.broadcast_to(xi[:, :, None], (RET_HEADS, RET_CHUNK, RET_DV)).copy()
    return decay_in, zeta_b, xi_b, [float(v) for v in chunk_decay]


def _rope_half_tables(pos, rot_dim, theta, period):
    inv = theta ** (-jnp.arange(0, rot_dim, 2, dtype=F32) / rot_dim)
    ang = pos.astype(F32)[:, None] * inv[None, :]
    c, s = jnp.cos(ang), jnp.sin(ang)
    half = rot_dim // 2
    p = pos.shape[0]
    one = jnp.ones((p, period - rot_dim), F32)
    zero_h = jnp.zeros((p, half), F32)
    zero_r = jnp.zeros((p, period - rot_dim), F32)
    cos_t = jnp.concatenate([c, c, one], axis=1)
    sm = jnp.concatenate([-s, zero_h, zero_r], axis=1)
    sp = jnp.concatenate([zero_h, s, zero_r], axis=1)
    rep = LANES // period
    return tuple(jnp.tile(t, (1, rep)) for t in (cos_t, sm, sp))


def _ret_kernel(q_ref, k_ref, v_ref, g_ref, cos_ref, sin_ref, dec_ref, zeta_ref, xi_ref, gnw_ref,
                o_ref, state_ref, *, chunk_decay):
    @pl.when(pl.program_id(1) == 0)
    def _():
        state_ref[...] = jnp.zeros_like(state_ref)

    n_chunks = TC_RET // RET_CHUNK
    for h in range(RET_HEADS):
        dec = dec_ref[h]
        zeta = zeta_ref[h]
        xi = xi_ref[h]
        gnw = gnw_ref[:, h * RET_DV:(h + 1) * RET_DV]
        for c in range(n_chunks):
            rows = slice(c * RET_CHUNK, (c + 1) * RET_CHUNK)
            cs = cos_ref[rows, :]
            sn = sin_ref[rows, :]
            q = q_ref[rows, h * RET_DK:(h + 1) * RET_DK].astype(F32)
            k = k_ref[rows, h * RET_DK:(h + 1) * RET_DK].astype(F32)
            q = q * cs + pltpu.roll(q, RET_DK // 2, 1) * sn
            k = (k * cs + pltpu.roll(k, RET_DK // 2, 1) * sn) * (RET_DK ** -0.5)
            v = v_ref[rows, h * RET_DV:(h + 1) * RET_DV]
            qb = q.astype(BF16)
            s = _dot_nt(qb, k.astype(BF16)) * dec
            inner = _dot(s.astype(BF16), v)
            r = state_ref[h]
            cross = _dot(qb, r.astype(BF16)) * xi
            kzt = (k * zeta).T.astype(BF16)
            state_ref[h] = chunk_decay[h] * r + _dot(kzt, v)
            o = inner + cross
            mu = jnp.mean(o, axis=-1, keepdims=True)
            d = o - mu
            var = jnp.mean(d * d, axis=-1, keepdims=True)
            on = d * lax.rsqrt(var + NORM_EPS) * gnw
            g = g_ref[rows, h * RET_DV:(h + 1) * RET_DV]
            o_ref[rows, h * RET_DV:(h + 1) * RET_DV] = (g * jax.nn.sigmoid(g) * on).astype(BF16)


def _retention(pa, pb, gnw, tabs, b, s):
    cos_t, sin_t, dec, zeta_b, xi_b, chunk_decay = tabs
    nt = s // TC_RET
    kern = functools.partial(_ret_kernel, chunk_decay=chunk_decay)
    tok = lambda w, j: pl.BlockSpec((TC_RET, w), lambda bi, si, j=j: (bi * nt + si, j))
    cst3 = lambda shp: pl.BlockSpec(shp, lambda bi, si: (0, 0, 0))
    return pl.pallas_call(
        kern,
        out_shape=jax.ShapeDtypeStruct((b * s, RET_V), BF16),
        grid=(b, nt),
        in_specs=[
            tok(RET_QK, PB_RQ // RET_QK),
            tok(RET_QK, PB_RK // RET_QK),
            tok(RET_V, PB_RV // RET_V),
            tok(RET_V, PA_RG // RET_V),
            pl.BlockSpec((TC_RET, LANES), lambda bi, si: (si, 0)),
            pl.BlockSpec((TC_RET, LANES), lambda bi, si: (si, 0)),
            cst3((RET_HEADS, RET_CHUNK, RET_CHUNK)),
            cst3((RET_HEADS, RET_CHUNK, RET_DK)),
            cst3((RET_HEADS, RET_CHUNK, RET_DV)),
            pl.BlockSpec((1, RET_V), lambda bi, si: (0, 0)),
        ],
        out_specs=pl.BlockSpec((TC_RET, RET_V), lambda bi, si: (bi * nt + si, 0)),
        scratch_shapes=[pltpu.VMEM((RET_HEADS, RET_DK, RET_DV), F32)],
        compiler_params=_cparams(("parallel", "arbitrary")),
        name="retention",
    )(pb, pb, pb, pa, cos_t, sin_t, dec, zeta_b, xi_b, gnw)


def _ones_row_pad(n):
    pad_rows = 16
    return jnp.where(lax.broadcasted_iota(I32, (pad_rows, n), 0) == 0, 1.0, 0.0)


def _rope16(x, cs, sm, sp):
    half = ROPE_DIM // 2
    return x * cs + pltpu.roll(x, LANES - half, 1) * sm + pltpu.roll(x, half, 1) * sp


def _prep_kernel(nq_ref, ks_ref, kw_ref, vs_ref, vw_ref, ng_ref, cs_ref, sm_ref, sp_ref,
                 qw_ref, ksw_ref, kww_ref, bd_ref,
                 q_ref, ksa_ref, kwo_ref, vst_ref, vwt_ref, gt_ref):
    cs, sm, sp = cs_ref[...], sm_ref[...], sp_ref[...]
    bd = bd_ref[...]
    for p in range(NSA_Q // LANES):
        x = nq_ref[:, p * LANES:(p + 1) * LANES]
        hi, lo = _split_bf16(x * x)
        ms = (_dot(hi, bd) + _dot(lo, bd)) * (1.0 / HEAD_DIM)
        y = x * lax.rsqrt(ms + NORM_EPS) * qw_ref[...]
        y = _rope16(y, cs, sm, sp) * (HEAD_DIM ** -0.5 * LOG2E)
        q_ref[:, p * LANES:(p + 1) * LANES] = y.astype(BF16)
    tok = pl.program_id(1) * TM_PREP + lax.broadcasted_iota(I32, (TM_PREP, LANES), 0)
    blk = lax.shift_right_logical(tok, int(np.log2(SLC_BLOCK)))
    onehot = jnp.where(lax.broadcasted_iota(I32, (TM_PREP, LANES), 1) == blk, 1.0, 0.0).astype(BF16)
    for g in range(NSA_GROUPS):
        sl = slice(g * LANES, (g + 1) * LANES)
        x = ks_ref[:, sl].astype(F32)
        y = x * lax.rsqrt(jnp.mean(x * x, axis=-1, keepdims=True) + NORM_EPS) * ksw_ref[...]
        ksa_ref[:, 2 * g * LANES:(2 * g + 1) * LANES] = _rope16(y, cs, sm, sp).astype(BF16)
        ksa_ref[:, (2 * g + 1) * LANES:(2 * g + 2) * LANES] = onehot
        x = kw_ref[:, sl].astype(F32)
        y = x * lax.rsqrt(jnp.mean(x * x, axis=-1, keepdims=True) + NORM_EPS) * kww_ref[...]
        kwo_ref[:, sl] = _rope16(y, cs, sm, sp).astype(BF16)
    vt = vs_ref[...].astype(F32).T
    wt = vw_ref[...].astype(F32).T
    for g in range(NSA_GROUPS):
        rows = slice(g * HEAD_DIM, (g + 1) * HEAD_DIM)
        vst_ref[g, 0:HEAD_DIM, :] = vt[rows, :].astype(BF16)
        vst_ref[g, HEAD_DIM:VROWS, :] = _ones_row_pad(TM_PREP).astype(BF16)
        for r in range(TM_PREP // LANES):
            vwt_ref[g, r, 0:HEAD_DIM, :] = wt[rows, r * LANES:(r + 1) * LANES].astype(BF16)
            vwt_ref[g, r, HEAD_DIM:VROWS, :] = _ones_row_pad(LANES).astype(BF16)
    gt_ref[...] = jax.nn.sigmoid(ng_ref[...]).T


def _nsa_prep(pa, pb, tabs, qw, ksw, kww, bd, b, s):
    cs, sm, sp = tabs
    nt = s // TM_PREP
    tokb = lambda w, j: pl.BlockSpec((TM_PREP, w), lambda bi, si, j=j: (bi * nt + si, j))
    tab = pl.BlockSpec((TM_PREP, LANES), lambda bi, si: (si, 0))
    vec = pl.BlockSpec((1, LANES), lambda bi, si: (0, 0))
    kv_w = NSA_GROUPS * LANES
    out_shape = (
        jax.ShapeDtypeStruct((b, s, NSA_Q), BF16),
        jax.ShapeDtypeStruct((b, s, 2 * kv_w), BF16),
        jax.ShapeDtypeStruct((b, s, kv_w), BF16),
        jax.ShapeDtypeStruct((b, NSA_GROUPS, nt, VROWS, TM_PREP), BF16),
        jax.ShapeDtypeStruct((b, NSA_GROUPS, s // LANES, VROWS, LANES), BF16),
        jax.ShapeDtypeStruct((b, LANES, s), F32),
    )
    out_specs = (
        pl.BlockSpec((None, TM_PREP, NSA_Q), lambda bi, si: (bi, si, 0)),
        pl.BlockSpec((None, TM_PREP, 2 * kv_w), lambda bi, si: (bi, si, 0)),
        pl.BlockSpec((None, TM_PREP, kv_w), lambda bi, si: (bi, si, 0)),
        pl.BlockSpec((None, NSA_GROUPS, None, VROWS, TM_PREP), lambda bi, si: (bi, 0, si, 0, 0)),
        pl.BlockSpec((None, NSA_GROUPS, TM_PREP // LANES, VROWS, LANES), lambda bi, si: (bi, 0, si, 0, 0)),
        pl.BlockSpec((None, LANES, TM_PREP), lambda bi, si: (bi, 0, si)),
    )
    return pl.pallas_call(
        _prep_kernel,
        out_shape=out_shape,
        grid=(b, nt),
        in_specs=[
            tokb(NSA_Q, PA_NQ // NSA_Q),
            tokb(kv_w, PB_KS // kv_w), tokb(kv_w, PB_KW // kv_w),
            tokb(NSA_KV, PB_VS // NSA_KV), tokb(NSA_KV, PB_VW // NSA_KV),
            tokb(LANES, PA_NG // LANES),
            tab, tab, tab, vec, vec, vec,
            pl.BlockSpec((LANES, LANES), lambda bi, si: (0, 0)),
        ],
        out_specs=out_specs,
        compiler_params=_cparams(("parallel", "parallel")),
        name="nsa_prep",
    )(pa, pb, pb, pb, pb, pa, cs, sm, sp, qw, ksw, kww, bd)


def _compress_kernel(x_ref, w1p_ref, w1a_ref, w1b_ref, pe_ref, w2_ref, cs_ref, sm_ref, sp_ref, nw_ref, o_ref,
                     *, is_key, n):
    a = bb = None
    for j in range(CMP_STRIDE):
        xj = x_ref[pl.ds(j, n, stride=CMP_STRIDE), :].astype(BF16)
        da = _dot(xj, w1p_ref[j])
        db = _dot(xj, w1p_ref[CMP_STRIDE + j])
        a = da if a is None else a + da
        bb = db if bb is None else bb + db
    pe_hi, pe_lo = _split_bf16(pe_ref[...])
    w1 = jnp.concatenate([w1a_ref[...], w1b_ref[...]], axis=0)
    pe_term = (_dot(pe_hi, w1) + _dot(pe_lo, w1))[0:1, :]
    hid = a + pltpu.roll(bb, n - 1, 0) + pe_term
    t = hid * (0.7978845608028654 * (1.0 + 0.044715 * hid * hid))
    act = 0.5 * hid * (1.0 + jnp.tanh(t))
    y = _dot(act.astype(BF16), w2_ref[...])
    if is_key:
        y = y * lax.rsqrt(jnp.mean(y * y, axis=-1, keepdims=True) + NORM_EPS) * nw_ref[...]
        o_ref[...] = _rope16(y, cs_ref[...], sm_ref[...], sp_ref[...]).astype(BF16)
    else:
        o_ref[...] = jnp.concatenate([y.T[0:HEAD_DIM, :], _ones_row_pad(n)], axis=0).astype(BF16)


def _compress(pa, w1p, w1a, w1b, pe, w2, tabs, nw, is_key, b, s):
    g, n = NSA_GROUPS, s // CMP_STRIDE
    cs, sm, sp = tabs
    full2 = lambda shp: pl.BlockSpec(shp, lambda bi, gi: (0, 0))
    out_block = (None, None, n, LANES) if is_key else (None, None, VROWS, n)
    out_shape = (b, g, n, LANES) if is_key else (b, g, VROWS, n)
    return pl.pallas_call(
        functools.partial(_compress_kernel, is_key=is_key, n=n),
        out_shape=jax.ShapeDtypeStruct(out_shape, BF16),
        grid=(b, g),
        in_specs=[
            pl.BlockSpec((s, LANES), lambda bi, gi: (bi, PA_C // LANES + (0 if is_key else 1))),
            pl.BlockSpec((None, CMP_LEN, LANES, CMP_HIDDEN), lambda bi, gi: (gi, 0, 0, 0)),
            full2(w1a.shape), full2(w1b.shape), full2(pe.shape), full2(w2.shape),
            full2(cs.shape), full2(sm.shape), full2(sp.shape), full2(nw.shape),
        ],
        out_specs=pl.BlockSpec(out_block, lambda bi, gi: (bi, gi, 0, 0)),
        compiler_params=_cparams(("parallel", "parallel")),
        name="compress_k" if is_key else "compress_v",
    )(pa, w1p, w1a, w1b, pe, w2, cs, sm, sp, nw)


def _weighted_values(vt, e):
    acc = _dot(vt, e.astype(BF16))
    return acc[0:HEAD_DIM, :], acc[HEAD_DIM:HEAD_DIM + 1, :]


def _attn_kernel(q_ref, kc_ref, vct_ref, ksa_ref, vst_ref, kw_ref, vwt_ref, gt_ref, ovt_ref,
                 o_ref, s_ref, cm_ref, qa_ref):
    gi = pl.program_id(1)
    q0 = pl.program_id(2) * TQ
    n_cmp = kc_ref.shape[0]
    ncols = GROUP_HEADS * TQ

    low = lax.broadcasted_iota(I32, (TQ, LANES), 1) < HEAD_DIM
    zero = jnp.zeros((TQ, LANES), BF16)
    parts = []
    for h in range(GROUP_HEADS):
        slab = q_ref[:, (h // 2) * LANES:(h // 2 + 1) * LANES]
        parts.append(jnp.where(low, slab, zero) if h % 2 == 0 else jnp.where(low, zero, slab))
    qs = jnp.concatenate(parts, axis=0)
    t_row = q0 + lax.broadcasted_iota(I32, (1, TQ), 1)

    def all_heads(bias):
        return jnp.concatenate([bias] * GROUP_HEADS, axis=1)

    def produce(keys, queries, slot, keep_max):
        st = _dot_nt(keys, queries)
        s_ref[slot, 0:keys.shape[0], :] = st
        if keep_max:
            cm_ref[slot] = jnp.broadcast_to(jnp.max(st, axis=0, keepdims=True), (8, ncols))

    def consume(slot, nk, vt, carry, bias):
        m_old, a_old = carry
        st = s_ref[slot, 0:nk, :]
        if bias is None:
            cm = cm_ref[slot][0:1, :]
        else:
            st = st + all_heads(bias)
            cm = jnp.max(st, axis=0, keepdims=True)
        m_new = jnp.maximum(m_old, cm)
        alpha = jnp.exp2(m_old - m_new)
        e = jnp.exp2(st - m_new)
        return m_new, alpha * a_old + _dot(vt, e.astype(BF16))

    init = (jnp.full((1, ncols), NEG, F32), jnp.zeros((VROWS, ncols), F32))

    def finish(acc):
        return acc[0:HEAD_DIM, :] * (1.0 / acc[HEAD_DIM:HEAD_DIM + 1, :])

    w0 = jnp.maximum(q0 - WINDOW, 0)
    wt0 = lax.shift_right_logical(w0, int(np.log2(LANES)))
    n_wb = WIN_TILES * LANES - TK
    produce(kc_ref[...], qs, 1, False)
    produce(kw_ref[pl.ds(pl.multiple_of(w0, LANES), TK), :], qs, 0, False)

    cmp_end = lax.broadcasted_iota(I32, (n_cmp, 1), 0) * CMP_STRIDE + (CMP_LEN - 1)
    st = s_ref[1, 0:n_cmp, :] + all_heads(jnp.where(cmp_end <= t_row, 0.0, NEG))
    m = jnp.max(st, axis=0, keepdims=True)
    e = jnp.exp2(st - jnp.where(m > 0.5 * NEG, m, 0.0))
    oc_all, l = _weighted_values(vct_ref[...], e)
    rl = 1.0 / jnp.where(l > 0.0, l, 1.0)
    oc_all = oc_all * rl
    psum = e[:, 0:TQ] * rl[:, 0:TQ]
    for h in range(1, GROUP_HEADS):
        psum = psum + e[:, h * TQ:(h + 1) * TQ] * rl[:, h * TQ:(h + 1) * TQ]
    ph, plo = _split_bf16(psum)
    imp_t = _dot(ovt_ref[...], ph) + _dot(ovt_ref[...], plo)

    def wbias(start, nk):
        kpos = start + lax.broadcasted_iota(I32, (nk, 1), 0)
        return jnp.where((kpos <= t_row) & (kpos > t_row - WINDOW), 0.0, NEG)

    produce(kw_ref[pl.ds(pl.multiple_of(w0 + TK, LANES), n_wb), :], qs, 1, False)
    vwa = jnp.concatenate([vwt_ref[wt0 + r] for r in range(TK // LANES)], axis=1)
    carry_w = consume(0, TK, vwa, init, wbias(w0, TK))
    produce(ksa_ref[0:TK, 0:LANES], qs, 0, False)

    jrow = lax.broadcasted_iota(I32, (LANES, TQ), 0)
    cur = lax.shift_right_logical(q0 + lax.broadcasted_iota(I32, (LANES, TQ), 1), int(np.log2(SLC_BLOCK)))
    forced = (jrow == 0) | (jrow == cur) | (jrow == cur - 1)
    valid = jrow <= cur
    bias = jnp.where(forced & valid, 0.0, NEG)
    val = jnp.where(valid & jnp.logical_not(forced), imp_t, -jnp.inf)
    for _ in range(SLC_TOPK - 3):
        mx = jnp.max(val, axis=0, keepdims=True)
        first = jnp.min(jnp.where(val == mx, jrow, LANES), axis=0, keepdims=True)
        pick = jrow == first
        bias = jnp.where(pick & valid, 0.0, bias)
        val = jnp.where(pick, -jnp.inf, val)

    vwb = jnp.concatenate([vwt_ref[wt0 + TK // LANES + r] for r in range(n_wb // LANES)], axis=1)
    _, a_w = consume(1, n_wb, vwb, carry_w, wbias(w0 + TK, n_wb))
    ow_all = finish(a_w)

    selb = bias.T.astype(BF16)
    qa_ref[...] = jnp.concatenate([qs, jnp.concatenate([selb] * GROUP_HEADS, axis=0)], axis=1)

    def sel_scores(ti, slot):
        produce(ksa_ref[pl.ds(pl.multiple_of(ti * TK, TK), TK), :], qa_ref[...], slot, True)

    def sel_consume(ti, slot, carry, causal):
        cb = None
        if causal:
            cb = jnp.where(ti * TK + lax.broadcasted_iota(I32, (TK, 1), 0) <= t_row, 0.0, NEG)
        return consume(slot, TK, vst_ref[ti], carry, cb)

    n_full = lax.shift_right_logical(q0, int(np.log2(TK)))
    rows_per_tile = TK // SLC_BLOCK
    bias0 = jnp.concatenate([jnp.broadcast_to(bias[r:r + 1, :], (SLC_BLOCK, TQ)) for r in range(rows_per_tile)],
                            axis=0)
    st0 = s_ref[0] + all_heads(bias0)
    s_ref[0] = st0
    cm_ref[0] = jnp.broadcast_to(jnp.max(st0, axis=0, keepdims=True), (8, ncols))

    def two_tiles(j, carry):
        sel_scores(2 * j + 1, 1)
        carry = sel_consume(2 * j, 0, carry, False)
        sel_scores(2 * j + 2, 0)
        return sel_consume(2 * j + 1, 1, carry, False)

    carry = lax.fori_loop(0, lax.shift_right_logical(n_full, 1), two_tiles, init)

    def odd_tail(c):
        sel_scores(n_full, 1)
        return sel_consume(n_full, 1, sel_consume(n_full - 1, 0, c, False), True)

    _, a_s = lax.cond((n_full & 1) == 1, odd_tail, lambda c: sel_consume(n_full, 0, c, True), carry)
    os_all = finish(a_s)

    for p in range(GROUP_HEADS // 2):
        halves = []
        for h in (2 * p, 2 * p + 1):
            c = slice(h * TQ, (h + 1) * TQ)
            gbase = (gi * GROUP_HEADS + h) * 3
            halves.append(gt_ref[pl.ds(gbase, 1), :] * oc_all[:, c]
                          + gt_ref[pl.ds(gbase + 1, 1), :] * os_all[:, c]
                          + gt_ref[pl.ds(gbase + 2, 1), :] * ow_all[:, c])
        o_ref[:, p * LANES:(p + 1) * LANES] = jnp.concatenate(halves, axis=0).T.astype(BF16)


def _attention(q, kc, vct, ksa, vst, kw, vwt, gt, ovt, b, s):
    nq = s // TQ
    n_cmp = kc.shape[2]
    gw = GROUP_HEADS * HEAD_DIM
    ncols = GROUP_HEADS * TQ
    return pl.pallas_call(
        _attn_kernel,
        out_shape=jax.ShapeDtypeStruct((b, s, NSA_Q), BF16),
        grid=(b, NSA_GROUPS, nq),
        in_specs=[
            pl.BlockSpec((None, TQ, gw), lambda bi, gi, qi: (bi, qi, gi)),
            pl.BlockSpec((None, None, n_cmp, LANES), lambda bi, gi, qi: (bi, gi, 0, 0)),
            pl.BlockSpec((None, None, VROWS, n_cmp), lambda bi, gi, qi: (bi, gi, 0, 0)),
            pl.BlockSpec((None, s, 2 * LANES), lambda bi, gi, qi: (bi, 0, gi)),
            pl.BlockSpec((None, None, s // TK, VROWS, TK), lambda bi, gi, qi: (bi, gi, 0, 0, 0)),
            pl.BlockSpec((None, s, LANES), lambda bi, gi, qi: (bi, 0, gi)),
            pl.BlockSpec((None, None, s // LANES, VROWS, LANES), lambda bi, gi, qi: (bi, gi, 0, 0, 0)),
            pl.BlockSpec((None, LANES, TQ), lambda bi, gi, qi: (bi, 0, qi)),
            pl.BlockSpec((LANES, n_cmp), lambda bi, gi, qi: (0, 0)),
        ],
        out_specs=pl.BlockSpec((None, TQ, gw), lambda bi, gi, qi: (bi, qi, gi)),
        scratch_shapes=[
            pltpu.VMEM((2, TK, ncols), F32),
            pltpu.VMEM((2, 8, ncols), F32),
            pltpu.VMEM((ncols, 2 * LANES), BF16),
        ],
        compiler_params=_cparams(("parallel", "parallel", "arbitrary")),
        name="nsa_attention",
    )(q, kc, vct, ksa, vst, kw, vwt, gt, ovt)


def _router_kernel(x_ref, nw_ref, wt_ref, b_ref, tri_ref, h_ref, e_ref, w_ref, r_ref, cnt_ref, base_ref):
    @pl.when(pl.program_id(0) == 0)
    def _():
        base_ref[...] = jnp.zeros_like(base_ref)

    x = x_ref[...]
    h = x * lax.rsqrt(jnp.mean(x * x, axis=-1, keepdims=True) + NORM_EPS) * nw_ref[...]
    _store_row_tiles(h_ref, h)
    h_hi, h_lo = _split_bf16(h)
    w_hi, w_lo = _split_bf16(wt_ref[...])
    lg = _dot_nt(w_hi, h_hi) + _dot_nt(w_hi, h_lo) + _dot_nt(w_lo, h_hi) + b_ref[...]
    erow = lax.broadcasted_iota(I32, lg.shape, 0)
    vals, hots = [], []
    for _ in range(TOP_K):
        mx = jnp.max(lg, axis=0, keepdims=True)
        first = jnp.min(jnp.where(lg == mx, erow, N_EXPERTS), axis=0, keepdims=True)
        hot = erow == first
        vals.append(mx)
        hots.append(hot)
        lg = jnp.where(hot, -jnp.inf, lg)
    ex = [jnp.exp(v - vals[0]) for v in vals]
    den = ex[0] + ex[1] + ex[2] + ex[3]
    onehots = [h.astype(F32) for h in hots]
    cnt = onehots[0] + onehots[1] + onehots[2] + onehots[3]
    pref = _dot(cnt.astype(BF16), tri_ref[...]) + base_ref[:, 0:1]
    erow_f = erow.astype(F32)
    idxs = [jnp.sum(oh * erow_f, axis=0, keepdims=True) for oh in onehots]
    ranks = [jnp.sum(oh * pref, axis=0, keepdims=True) for oh in onehots]
    base_ref[...] = base_ref[...] + jnp.sum(cnt, axis=1, keepdims=True)
    cnt_ref[...] = base_ref[...]
    pad_f = [jnp.zeros_like(den)] * (8 - TOP_K)
    e_ref[...] = jnp.concatenate(idxs + pad_f, axis=0).astype(I32)
    r_ref[...] = jnp.concatenate(ranks + pad_f, axis=0).astype(I32)
    w_ref[...] = jnp.concatenate([e / den for e in ex] + pad_f, axis=0)


def _router(x2, nw, wt, bcol, tri):
    n = x2.shape[0]
    small = pl.BlockSpec((8, TM_PROJ), lambda i: (0, i))
    return pl.pallas_call(
        _router_kernel,
        out_shape=(jax.ShapeDtypeStruct(_tiled(n), F32),
                   jax.ShapeDtypeStruct((8, n), I32),
                   jax.ShapeDtypeStruct((8, n), F32),
                   jax.ShapeDtypeStruct((8, n), I32),
                   jax.ShapeDtypeStruct((N_EXPERTS, LANES), F32)),
        grid=(n // TM_PROJ,),
        in_specs=[
            pl.BlockSpec((TM_PROJ, D_MODEL), lambda i: (i, 0)),
            pl.BlockSpec((1, D_MODEL), lambda i: (0, 0)),
            pl.BlockSpec((N_EXPERTS, D_MODEL), lambda i: (0, 0)),
            pl.BlockSpec((N_EXPERTS, 1), lambda i: (0, 0)),
            pl.BlockSpec((TM_PROJ, TM_PROJ), lambda i: (0, 0)),
        ],
        out_specs=(pl.BlockSpec(_tiled(TM_PROJ), lambda i: (i, 0)), small, small, small,
                   pl.BlockSpec((N_EXPERTS, LANES), lambda i: (0, 0))),
        scratch_shapes=[pltpu.VMEM((N_EXPERTS, LANES), F32)],
        compiler_params=_cparams(("arbitrary",)),
        name="router",
    )(x2, nw, wt, bcol, tri)


def _row_copy(src, src_row, dst, dst_row, sem):
    tile = lambda ref, r: ref.at[pl.ds(pl.multiple_of(r * SUB, SUB), SUB), :]
    return pltpu.make_async_copy(tile(src, src_row), tile(dst, dst_row), sem)


def _issue_rows(nrow, copy_of):
    def body(g, c):
        for j in range(8):
            copy_of(g * 8 + j).start(priority=j % 2)
        return c

    lax.fori_loop(0, nrow // 8, body, 0)


def _wait_rows(nrow, copy_of):
    def body(g, c):
        for j in range(8):
            copy_of(0).wait()
        return c

    lax.fori_loop(0, nrow // 8, body, 0)


def _store_row_tiles(ref, x):
    for c in range(SUB):
        ref[pl.ds(c, x.shape[0], stride=SUB), :] = x[:, c * LANES:(c + 1) * LANES]


def _load_row_tiles(ref, first_row, nrows):
    return jnp.concatenate([ref[pl.ds(first_row * SUB + c, nrows, stride=SUB), :] for c in range(SUB)], axis=1)


def _dispatch_kernel(zinfo_ref, dest_ref, h_ref, x_hbm, zbuf, sem, zsem, *, n_blocks):
    def zero_block(row):
        return pltpu.make_async_copy(
            zbuf, x_hbm.at[pl.ds(pl.multiple_of(row * SUB, TM_MOE * SUB), TM_MOE * SUB), :], zsem)

    @pl.when(pl.program_id(0) == 0)
    def _():
        zbuf[...] = jnp.zeros_like(zbuf)
        for e in range(N_EXPERTS):
            zero_block(zinfo_ref[e]).start()
        for e in range(N_EXPERTS):
            zero_block(0).wait()
        nvalid = zinfo_ref[N_EXPERTS]

        def ztail(bk, c):
            zero_block(bk * TM_MOE).start()
            return c

        lax.fori_loop(nvalid, n_blocks, ztail, 0)

        def zwait(bk, c):
            zero_block(0).wait()
            return c

        lax.fori_loop(nvalid, n_blocks, zwait, 0)

    nrow = TOP_K * TM_CMB
    _issue_rows(nrow, lambda r: _row_copy(h_ref, r & (TM_CMB - 1), x_hbm, dest_ref[0, 0, r], sem))
    _wait_rows(nrow, lambda r: _row_copy(h_ref, 0, x_hbm, 0, sem))


def _dispatch(zinfo, dest_b, h3, rows):
    n = h3.shape[0] // SUB
    grid_spec = pltpu.PrefetchScalarGridSpec(
        num_scalar_prefetch=1,
        grid=(n // TM_CMB,),
        in_specs=[
            pl.BlockSpec((1, 1, TOP_K * TM_CMB), lambda i, z: (i, 0, 0), memory_space=pltpu.SMEM),
            pl.BlockSpec(_tiled(TM_CMB), lambda i, z: (i, 0)),
        ],
        out_specs=pl.BlockSpec(memory_space=pl.ANY),
        scratch_shapes=[pltpu.VMEM(_tiled(TM_MOE), F32), pltpu.SemaphoreType.DMA(()),
                        pltpu.SemaphoreType.DMA(())],
    )
    return pl.pallas_call(
        functools.partial(_dispatch_kernel, n_blocks=rows // TM_MOE),
        out_shape=jax.ShapeDtypeStruct(_tiled(rows), F32),
        grid_spec=grid_spec,
        compiler_params=_cparams(("arbitrary",)),
        name="dispatch",
    )(zinfo, dest_b, h3)


def _expert_kernel(blk_e_ref, nvalid_ref, x_ref, wgu_ref, bgu_ref, wd_ref, bd_ref, y_ref, wgu_bf, wd_bf):
    i = pl.program_id(0)
    used = i < nvalid_ref[0]

    @pl.when(jnp.logical_not(used))
    def _():
        y_ref[...] = jnp.zeros_like(y_ref)

    @pl.when(used & ((i == 0) | (blk_e_ref[i] != blk_e_ref[jnp.maximum(i - 1, 0)])))
    def _():
        wgu_bf[...] = wgu_ref[...].astype(BF16)
        wd_bf[...] = wd_ref[...].astype(BF16)

    @pl.when(used)
    def _():
        gu = _dot(_load_row_tiles(x_ref, 0, TM_MOE).astype(BF16), wgu_bf[...]) + bgu_ref[...]
        gate = jnp.minimum(gu[:, :D_FF], SWIGLU_LIMIT)
        up = jnp.clip(gu[:, D_FF:], -SWIGLU_LIMIT, SWIGLU_LIMIT)
        glu = gate * jax.nn.sigmoid(gate * SWIGLU_ALPHA)
        _store_row_tiles(y_ref, _dot(((up + 1.0) * glu).astype(BF16), wd_bf[...]) + bd_ref[...])


def _experts(blk_e, nvalid, xbuf, wgu, bgu, wd, bd, layer):
    n_blocks = blk_e.shape[0]
    wblk = lambda i, be, nv: (be[i], 0, 0)
    wblk4 = lambda i, be, nv: (layer, be[i], 0, 0)
    grid_spec = pltpu.PrefetchScalarGridSpec(
        num_scalar_prefetch=2,
        grid=(n_blocks,),
        in_specs=[
            pl.BlockSpec(_tiled(TM_MOE), lambda i, be, nv: (jnp.minimum(i, nv[0] - 1), 0)),
            pl.BlockSpec((None, None, D_MODEL, 2 * D_FF), wblk4),
            pl.BlockSpec((None, 1, 2 * D_FF), wblk),
            pl.BlockSpec((None, None, D_FF, D_MODEL), wblk4),
            pl.BlockSpec((None, 1, D_MODEL), wblk),
        ],
        out_specs=pl.BlockSpec(_tiled(TM_MOE), lambda i, be, nv: (i, 0)),
        scratch_shapes=[pltpu.VMEM((D_MODEL, 2 * D_FF), BF16), pltpu.VMEM((D_FF, D_MODEL), BF16)],
    )
    return pl.pallas_call(
        _expert_kernel,
        out_shape=jax.ShapeDtypeStruct(_tiled(n_blocks * TM_MOE), F32),
        grid_spec=grid_spec,
        compiler_params=_cparams(("arbitrary",)),
        name="experts",
    )(blk_e, nvalid, xbuf, wgu, bgu, wd, bd)


def _combine_kernel(dcur_ref, dnext_ref, x_ref, w_ref, y_hbm, o_ref, ybuf, sem):
    i = pl.program_id(0)
    nrow = TOP_K * TM_CMB
    slot = i & 1

    def issue(d_ref, s):
        _issue_rows(nrow, lambda r: _row_copy(y_hbm, d_ref[0, 0, r], ybuf.at[s], r, sem.at[s]))

    @pl.when(i == 0)
    def _():
        issue(dcur_ref, 0)

    @pl.when(i + 1 < pl.num_programs(0))
    def _():
        issue(dnext_ref, 1 - slot)

    _wait_rows(nrow, lambda r: _row_copy(y_hbm, 0, ybuf.at[slot], 0, sem.at[slot]))
    acc = x_ref[...]
    w = w_ref[...]
    for k in range(TOP_K):
        acc = acc + w[:, k:k + 1] * _load_row_tiles(ybuf.at[slot], k * TM_CMB, TM_CMB)
    o_ref[...] = acc


def _combine(dest_b, x2, wts, ybuf):
    n = x2.shape[0]
    nt = n // TM_CMB
    return pl.pallas_call(
        _combine_kernel,
        out_shape=jax.ShapeDtypeStruct((n, D_MODEL), F32),
        grid=(nt,),
        in_specs=[
            pl.BlockSpec((1, 1, TOP_K * TM_CMB), lambda i: (i, 0, 0), memory_space=pltpu.SMEM),
            pl.BlockSpec((1, 1, TOP_K * TM_CMB), lambda i: (jnp.minimum(i + 1, nt - 1), 0, 0),
                         memory_space=pltpu.SMEM),
            pl.BlockSpec((TM_CMB, D_MODEL), lambda i: (i, 0)),
            pl.BlockSpec((TM_CMB, TOP_K), lambda i: (i, 0)),
            pl.BlockSpec(memory_space=pl.ANY),
        ],
        out_specs=pl.BlockSpec((TM_CMB, D_MODEL), lambda i: (i, 0)),
        scratch_shapes=[pltpu.VMEM((2,) + _tiled(TOP_K * TM_CMB), F32), pltpu.SemaphoreType.DMA((2,))],
        compiler_params=_cparams(("arbitrary",)),
        name="combine",
    )(dest_b, dest_b, x2, wts, ybuf)


def _route(e_t, r_t, cnt, n):
    counts = cnt[:, 0].astype(I32)
    padded = (counts + TM_MOE - 1) // TM_MOE * TM_MOE
    pend = jnp.cumsum(padded)
    pstart = pend - padded
    n_blocks = n * TOP_K // TM_MOE + N_EXPERTS
    rows = n_blocks * TM_MOE
    dest = r_t[:TOP_K]
    for e in range(N_EXPERTS):
        dest = dest + jnp.where(e_t[:TOP_K] == e, pstart[e].astype(I32), 0)
    nt = n // TM_CMB
    dest_b = dest.reshape(TOP_K, nt, TM_CMB).transpose(1, 0, 2).reshape(nt, 1, TOP_K * TM_CMB)
    blk_row = jnp.arange(n_blocks, dtype=I32)[:, None] * TM_MOE
    blk_e = jnp.minimum(jnp.sum((pend[None, :] <= blk_row).astype(I32), axis=1), N_EXPERTS - 1)
    nvalid = (pend[-1:] // TM_MOE).astype(I32)
    zrow = jnp.clip(pstart + padded - TM_MOE, 0, rows - TM_MOE).astype(I32)
    return dest_b, blk_e, nvalid, jnp.concatenate([zrow, nvalid]), rows


def _layer_weights(l, w_in, norm1_w, ret_gn_w, qk_norm_w, cmp_pos, cmp_w1, cmp_w2, w_o_ret, w_o_nsa, w_out,
                   norm2_w, router_w, router_b, w_gate_up, b_gate_up, w_down, b_down):
    w = w_in[l]
    o = np.cumsum((0, RET_QK, RET_QK, RET_V, RET_V, NSA_Q) + (NSA_KV,) * 6 + (NSA_GATE, 2 * D_MODEL))
    rq, rk, rv, rg, nq = (w[:, o[i]:o[i + 1]] for i in range(5))
    kc, vc, ksl, vsl, kwi, vwi = (w[:, o[5 + i]:o[6 + i]] for i in range(6))
    ng, mg = w[:, o[11]:o[12]], w[:, o[12]:o[13]]

    def dup(t):
        t = t.reshape(D_MODEL, NSA_GROUPS, 1, HEAD_DIM)
        return jnp.broadcast_to(t, (D_MODEL, NSA_GROUPS, 2, HEAD_DIM)).reshape(D_MODEL, NSA_GROUPS * LANES)

    ng_pad = jnp.pad(ng, ((0, 0), (0, LANES - NSA_GATE)))
    wa = jnp.concatenate([mg, rg, nq, kc, vc, ng_pad], axis=1).astype(BF16)
    wb = jnp.concatenate([rq, rk, rv, dup(ksl), dup(kwi), vsl, vwi], axis=1).astype(BF16)
    qk = qk_norm_w[l]
    tile = lambda v: jnp.tile(v, LANES // HEAD_DIM)[None, :]
    w1 = cmp_w1[l].astype(BF16)
    half = CMP_STRIDE * HEAD_DIM
    w1r = w1.reshape(2, 1, CMP_LEN, 1, HEAD_DIM, CMP_HIDDEN)
    gsel = jnp.eye(NSA_GROUPS, dtype=BF16).reshape(1, NSA_GROUPS, 1, NSA_GROUPS, 1, 1)
    w1p = (w1r * gsel).reshape(2, NSA_GROUPS, CMP_LEN, LANES, CMP_HIDDEN)
    w2 = jnp.concatenate([cmp_w2[l], cmp_w2[l]], axis=-1).astype(BF16)
    pe = jnp.broadcast_to(cmp_pos[l].reshape(2, 1, CMP_LEN * HEAD_DIM), (2, 8, CMP_LEN * HEAD_DIM))
    return dict(
        nw1=norm1_w[l][None, :], wa=wa, wb=wb, gnw=ret_gn_w[l][None, :],
        qw=tile(qk[0]), kcw=tile(qk[1]), ksw=tile(qk[2]), kww=tile(qk[3]),
        w1a=w1[:, :half], w1b=w1[:, half:], w1p=w1p, w2=w2, pe=pe,
        wr=w_o_ret[l].astype(BF16), wn=w_o_nsa[l].astype(BF16), wo=w_out[l].astype(BF16),
        nw2=norm2_w[l][None, :], rwt=router_w[l].T, rb=router_b[l][:, None],
        bgu=b_gate_up[l][:, None, :], bd=b_down[l][:, None, :],
    )


def kernel(x, norm1_w, w_in, ret_gn_w, qk_norm_w, cmp_pos, cmp_w1, cmp_w2, w_o_ret, w_o_nsa, w_out, norm2_w,
           router_w, router_b, w_gate_up, b_gate_up, w_down, b_down):
    b, s, _ = x.shape
    depth = w_in.shape[0]
    n = b * s
    n_slc = s // SLC_BLOCK
    assert s % TC_RET == 0 and s % TK == 0 and s >= WIN_TILES * LANES and n % TM_PROJ == 0
    assert n_slc <= LANES
    assert s // CMP_STRIDE <= TK

    pos = jnp.arange(s)
    inv = RET_THETA ** (-jnp.arange(0, RET_DK, 2, dtype=F32) / RET_DK)
    ang = pos.astype(F32)[:, None] * inv[None, :]
    ret_cos = jnp.concatenate([jnp.cos(ang), jnp.cos(ang)], axis=1)
    ret_sin = jnp.concatenate([-jnp.sin(ang), jnp.sin(ang)], axis=1)
    dec, zeta_b, xi_b, chunk_decay = _ret_consts()
    ret_tabs = (ret_cos, ret_sin, jnp.asarray(dec), jnp.asarray(zeta_b), jnp.asarray(xi_b), chunk_decay)
    tok_tabs = _rope_half_tables(pos, ROPE_DIM, ROPE_THETA, HEAD_DIM)
    n_cmp = s // CMP_STRIDE
    cmp_tabs = _rope_half_tables(jnp.arange(n_cmp) * CMP_STRIDE + CMP_LEN - 1, ROPE_DIM, ROPE_THETA, HEAD_DIM)
    li = np.arange(LANES)
    bd = jnp.asarray((li[:, None] // HEAD_DIM == li[None, :] // HEAD_DIM).astype(np.float32), BF16)
    ci, sj = np.arange(n_cmp)[None, :], np.arange(LANES)[:, None]
    ovt = ((ci * CMP_STRIDE < (sj + 1) * SLC_BLOCK) & (ci * CMP_STRIDE + CMP_LEN > sj * SLC_BLOCK)
           & (ci < n_cmp - CMP_LEN // CMP_STRIDE + 1) & (sj < n_slc))
    ovt = jnp.asarray(ovt.astype(np.float32), BF16)
    ti = np.arange(TM_PROJ)
    tri = jnp.asarray((ti[:, None] < ti[None, :]).astype(np.float32), BF16)

    x2 = x.reshape(n, D_MODEL)
    for l in range(depth):
        p = _layer_weights(l, w_in, norm1_w, ret_gn_w, qk_norm_w, cmp_pos, cmp_w1, cmp_w2, w_o_ret, w_o_nsa,
                           w_out, norm2_w, router_w, router_b, w_gate_up, b_gate_up, w_down, b_down)
        pa, pb = _inproj(x2, p["nw1"], p["wa"], p["wb"])
        ret = _retention(pa, pb, p["gnw"], ret_tabs, b, s)
        q, ksa, kw, vst, vwt, gt = _nsa_prep(pa, pb, tok_tabs, p["qw"], p["ksw"], p["kww"], bd, b, s)
        kcmp, vcmp_t = (_compress(pa, p["w1p"][kv], p["w1a"][kv], p["w1b"][kv], p["pe"][kv], p["w2"][kv],
                                  cmp_tabs, p["kcw"], kv == 0, b, s) for kv in range(2))
        att = _attention(q, kcmp, vcmp_t, ksa, vst, kw, vwt, gt, ovt, b, s)
        x2 = _outproj(x2, ret, att.reshape(n, NSA_Q), pa, p["wr"], p["wn"], p["wo"])
        h2, e_t, w_t, r_t, cnt = _router(x2, p["nw2"], p["rwt"], p["rb"], tri)
        dest_b, blk_e, nvalid, zinfo, rows = _route(e_t, r_t, cnt, n)
        xbuf = _dispatch(zinfo, dest_b, h2, rows)
        ybuf = _experts(blk_e, nvalid, xbuf, w_gate_up, p["bgu"], w_down, p["bd"], l)
        x2 = _combine(dest_b, x2, w_t[:TOP_K].T, ybuf)
    return x2.reshape(b, s, D_MODEL)
```

```python
import functools

import numpy as np
import jax
import jax.numpy as jnp
from jax import lax
from jax.experimental import pallas as pl
from jax.experimental.pallas import tpu as pltpu

F32 = jnp.float32
BF16 = jnp.bfloat16
I32 = jnp.int32

D_MODEL = 1024
RET_HEADS, RET_DK, RET_DV, RET_CHUNK, RET_THETA = 4, 128, 256, 128, 10000.0
NSA_HEADS, NSA_GROUPS, HEAD_DIM = 16, 2, 64
GROUP_HEADS = NSA_HEADS // NSA_GROUPS
ROPE_DIM, ROPE_THETA = HEAD_DIM // 4, 500000.0
CMP_LEN, CMP_STRIDE, CMP_HIDDEN = 32, 16, 4 * HEAD_DIM
SLC_BLOCK, SLC_TOPK, WINDOW = 64, 16, 512
N_EXPERTS, TOP_K, D_FF = 32, 4, D_MODEL
SWIGLU_LIMIT, SWIGLU_ALPHA = 7.0, 1.702
NORM_EPS = 1e-6
RET_QK, RET_V = RET_HEADS * RET_DK, RET_HEADS * RET_DV
NSA_Q, NSA_KV, NSA_GATE = NSA_HEADS * HEAD_DIM, NSA_GROUPS * HEAD_DIM, NSA_HEADS * 3

LANES = 128
NEG = -1e30
LOG2E = 1.4426950408889634
VMEM_LIMIT = 56 * 1024 * 1024

TM_PROJ = 256
TC_RET = 1024
TM_PREP = 512
TQ = 128
TK = 512
WIN_TILES = WINDOW // LANES + 1
VROWS = HEAD_DIM + 16
TM_MOE = 512
TM_CMB = 256
SUB = 8


def _tiled(nrows):
    return (nrows * SUB, LANES)

PA_MG, PA_RG, PA_NQ, PA_C, PA_NG = 0, 2048, 3072, 4096, 4352
PA_W = 4480
PB_RQ, PB_RK, PB_RV, PB_KS, PB_KW, PB_VS, PB_VW = 0, 512, 1024, 2048, 2304, 2560, 2688
PB_W = 2816


def _cparams(sem):
    return pltpu.CompilerParams(dimension_semantics=sem, vmem_limit_bytes=VMEM_LIMIT)


def _dot(a, b):
    return jnp.dot(a, b, preferred_element_type=F32)


def _dot_nt(a, b):
    return lax.dot_general(a, b, (((1,), (1,)), ((), ())), preferred_element_type=F32)


def _split_bf16(x):
    hi = x.astype(BF16)
    lo = (x - hi.astype(F32)).astype(BF16)
    return hi, lo


def _inproj_kernel(x_ref, nw_ref, wa_ref, wb_ref, pa_ref, pb_ref):
    x = x_ref[...]
    h = x * lax.rsqrt(jnp.mean(x * x, axis=-1, keepdims=True) + NORM_EPS) * nw_ref[...]
    h = h.astype(BF16)
    for c in range(0, PA_W, 512):
        w = min(512, PA_W - c)
        pa_ref[:, c:c + w] = _dot(h, wa_ref[:, c:c + w])
    for c in range(0, PB_W, 512):
        w = min(512, PB_W - c)
        pb_ref[:, c:c + w] = _dot(h, wb_ref[:, c:c + w]).astype(BF16)


def _inproj(x2, nw, wa, wb):
    n = x2.shape[0]
    return pl.pallas_call(
        _inproj_kernel,
        out_shape=(jax.ShapeDtypeStruct((n, PA_W), F32), jax.ShapeDtypeStruct((n, PB_W), BF16)),
        grid=(n // TM_PROJ,),
        in_specs=[
            pl.BlockSpec((TM_PROJ, D_MODEL), lambda i: (i, 0)),
            pl.BlockSpec((1, D_MODEL), lambda i: (0, 0)),
            pl.BlockSpec((D_MODEL, PA_W), lambda i: (0, 0)),
            pl.BlockSpec((D_MODEL, PB_W), lambda i: (0, 0)),
        ],
        out_specs=(pl.BlockSpec((TM_PROJ, PA_W), lambda i: (i, 0)),
                   pl.BlockSpec((TM_PROJ, PB_W), lambda i: (i, 0))),
        compiler_params=_cparams(("parallel",)),
        name="inproj",
    )(x2, nw, wa, wb)


def _outproj_kernel(x_ref, ret_ref, att_ref, mg_ref, wr_ref, wn_ref, wo_ref, o_ref):
    mg = mg_ref[...]
    a = jax.nn.sigmoid(mg[:, :D_MODEL]) * _dot(ret_ref[...], wr_ref[...])
    b = jax.nn.sigmoid(mg[:, D_MODEL:]) * _dot(att_ref[...], wn_ref[...])
    o_ref[...] = x_ref[...] + _dot((a + b).astype(BF16), wo_ref[...])


def _outproj(x2, ret, att, pa, wr, wn, wo):
    n = x2.shape[0]
    row = lambda i: (i, 0)
    full = lambda i: (0, 0)
    return pl.pallas_call(
        _outproj_kernel,
        out_shape=jax.ShapeDtypeStruct((n, D_MODEL), F32),
        grid=(n // TM_PROJ,),
        in_specs=[
            pl.BlockSpec((TM_PROJ, D_MODEL), row),
            pl.BlockSpec((TM_PROJ, RET_V), row),
            pl.BlockSpec((TM_PROJ, NSA_Q), row),
            pl.BlockSpec((TM_PROJ, 2 * D_MODEL), lambda i: (i, PA_MG // (2 * D_MODEL))),
            pl.BlockSpec((RET_V, D_MODEL), full),
            pl.BlockSpec((NSA_Q, D_MODEL), full),
            pl.BlockSpec((D_MODEL, D_MODEL), full),
        ],
        out_specs=pl.BlockSpec((TM_PROJ, D_MODEL), row),
        compiler_params=_cparams(("parallel",)),
        name="outproj",
    )(x2, ret, att, pa, wr, wn, wo)


def _ret_consts():
    h = np.arange(RET_HEADS, dtype=np.float32)
    log_g = np.log1p(-np.exp2(-5.0 - h)).astype(np.float32)
    n = np.arange(RET_CHUNK, dtype=np.float32)
    diff = n[:, None] - n[None, :]
    decay_in = np.where(diff >= 0, np.exp(log_g[:, None, None] * np.maximum(diff, 0.0)), 0.0).astype(np.float32)
    zeta = np.exp(log_g[:, None] * (RET_CHUNK - 1 - n)[None, :]).astype(np.float32)
    xi = np.exp(log_g[:, None] * (n + 1)[None, :]).astype(np.float32)
    chunk_decay = np.exp(log_g * RET_CHUNK).astype(np.float32)
    zeta_b = np.broadcast_to(zeta[:, :, None], (RET_HEADS, RET_CHUNK, RET_DK)).copy()
    xi_b = np.broadcast_to(xi[:, :, None], (RET_HEADS, RET_CHUNK, RET_DV)).copy()
    return decay_in, zeta_b, xi_b, [float(v) for v in chunk_decay]


def _rope_half_tables(pos, rot_dim, theta, period):
    inv = theta ** (-jnp.arange(0, rot_dim, 2, dtype=F32) / rot_dim)
    ang = pos.astype(F32)[:, None] * inv[None, :]
    c, s = jnp.cos(ang), jnp.sin(ang)
    half = rot_dim // 2
    p = pos.shape[0]
    one = jnp.ones((p, period - rot_dim), F32)
    zero_h = jnp.zeros((p, half), F32)
    zero_r = jnp.zeros((p, period - rot_dim), F32)
    cos_t = jnp.concatenate([c, c, one], axis=1)
    sm = jnp.concatenate([-s, zero_h, zero_r], axis=1)
    sp = jnp.concatenate([zero_h, s, zero_r], axis=1)
    rep = LANES // period
    return tuple(jnp.tile(t, (1, rep)) for t in (cos_t, sm, sp))


def _ret_kernel(q_ref, k_ref, v_ref, g_ref, cos_ref, sin_ref, dec_ref, zeta_ref, xi_ref, gnw_ref,
                o_ref, state_ref, *, chunk_decay):
    @pl.when(pl.program_id(1) == 0)
    def _():
        state_ref[...] = jnp.zeros_like(state_ref)

    n_chunks = TC_RET // RET_CHUNK
    for h in range(RET_HEADS):
        dec = dec_ref[h]
        zeta = zeta_ref[h]
        xi = xi_ref[h]
        gnw = gnw_ref[:, h * RET_DV:(h + 1) * RET_DV]
        for c in range(n_chunks):
            rows = slice(c * RET_CHUNK, (c + 1) * RET_CHUNK)
            cs = cos_ref[rows, :]
            sn = sin_ref[rows, :]
            q = q_ref[rows, h * RET_DK:(h + 1) * RET_DK].astype(F32)
            k = k_ref[rows, h * RET_DK:(h + 1) * RET_DK].astype(F32)
            q = q * cs + pltpu.roll(q, RET_DK // 2, 1) * sn
            k = (k * cs + pltpu.roll(k, RET_DK // 2, 1) * sn) * (RET_DK ** -0.5)
            v = v_ref[rows, h * RET_DV:(h + 1) * RET_DV]
            qb = q.astype(BF16)
            s = _dot_nt(qb, k.astype(BF16)) * dec
            inner = _dot(s.astype(BF16), v)
            r = state_ref[h]
            cross = _dot(qb, r.astype(BF16)) * xi
            kzt = (k * zeta).T.astype(BF16)
            state_ref[h] = chunk_decay[h] * r + _dot(kzt, v)
            o = inner + cross
            mu = jnp.mean(o, axis=-1, keepdims=True)
            d = o - mu
            var = jnp.mean(d * d, axis=-1, keepdims=True)
            on = d * lax.rsqrt(var + NORM_EPS) * gnw
            g = g_ref[rows, h * RET_DV:(h + 1) * RET_DV]
            o_ref[rows, h * RET_DV:(h + 1) * RET_DV] = (g * jax.nn.sigmoid(g) * on).astype(BF16)


def _retention(pa, pb, gnw, tabs, b, s):
    cos_t, sin_t, dec, zeta_b, xi_b, chunk_decay = tabs
    nt = s // TC_RET
    kern = functools.partial(_ret_kernel, chunk_decay=chunk_decay)
    tok = lambda w, j: pl.BlockSpec((TC_RET, w), lambda bi, si, j=j: (bi * nt + si, j))
    cst3 = lambda shp: pl.BlockSpec(shp, lambda bi, si: (0, 0, 0))
    return pl.pallas_call(
        kern,
        out_shape=jax.ShapeDtypeStruct((b * s, RET_V), BF16),
        grid=(b, nt),
        in_specs=[
            tok(RET_QK, PB_RQ // RET_QK),
            tok(RET_QK, PB_RK // RET_QK),
            tok(RET_V, PB_RV // RET_V),
            tok(RET_V, PA_RG // RET_V),
            pl.BlockSpec((TC_RET, LANES), lambda bi, si: (si, 0)),
            pl.BlockSpec((TC_RET, LANES), lambda bi, si: (si, 0)),
            cst3((RET_HEADS, RET_CHUNK, RET_CHUNK)),
            cst3((RET_HEADS, RET_CHUNK, RET_DK)),
            cst3((RET_HEADS, RET_CHUNK, RET_DV)),
            pl.BlockSpec((1, RET_V), lambda bi, si: (0, 0)),
        ],
        out_specs=pl.BlockSpec((TC_RET, RET_V), lambda bi, si: (bi * nt + si, 0)),
        scratch_shapes=[pltpu.VMEM((RET_HEADS, RET_DK, RET_DV), F32)],
        compiler_params=_cparams(("parallel", "arbitrary")),
        name="retention",
    )(pb, pb, pb, pa, cos_t, sin_t, dec, zeta_b, xi_b, gnw)


def _ones_row_pad(n):
    pad_rows = 16
    return jnp.where(lax.broadcasted_iota(I32, (pad_rows, n), 0) == 0, 1.0, 0.0)


def _rope16(x, cs, sm, sp):
    half = ROPE_DIM // 2
    return x * cs + pltpu.roll(x, LANES - half, 1) * sm + pltpu.roll(x, half, 1) * sp


def _prep_kernel(nq_ref, ks_ref, kw_ref, vs_ref, vw_ref, ng_ref, cs_ref, sm_ref, sp_ref,
                 qw_ref, ksw_ref, kww_ref, bd_ref,
                 q_ref, ksa_ref, kwo_ref, vst_ref, vwt_ref, gt_ref):
    cs, sm, sp = cs_ref[...], sm_ref[...], sp_ref[...]
    bd = bd_ref[...]
    for p in range(NSA_Q // LANES):
        x = nq_ref[:, p * LANES:(p + 1) * LANES]
        hi, lo = _split_bf16(x * x)
        ms = (_dot(hi, bd) + _dot(lo, bd)) * (1.0 / HEAD_DIM)
        y = x * lax.rsqrt(ms + NORM_EPS) * qw_ref[...]
        y = _rope16(y, cs, sm, sp) * (HEAD_DIM ** -0.5 * LOG2E)
        q_ref[:, p * LANES:(p + 1) * LANES] = y.astype(BF16)
    tok = pl.program_id(1) * TM_PREP + lax.broadcasted_iota(I32, (TM_PREP, LANES), 0)
    blk = lax.shift_right_logical(tok, int(np.log2(SLC_BLOCK)))
    onehot = jnp.where(lax.broadcasted_iota(I32, (TM_PREP, LANES), 1) == blk, 1.0, 0.0).astype(BF16)
    for g in range(NSA_GROUPS):
        sl = slice(g * LANES, (g + 1) * LANES)
        x = ks_ref[:, sl].astype(F32)
        y = x * lax.rsqrt(jnp.mean(x * x, axis=-1, keepdims=True) + NORM_EPS) * ksw_ref[...]
        ksa_ref[:, 2 * g * LANES:(2 * g + 1) * LANES] = _rope16(y, cs, sm, sp).astype(BF16)
        ksa_ref[:, (2 * g + 1) * LANES:(2 * g + 2) * LANES] = onehot
        x = kw_ref[:, sl].astype(F32)
        y = x * lax.rsqrt(jnp.mean(x * x, axis=-1, keepdims=True) + NORM_EPS) * kww_ref[...]
        kwo_ref[:, sl] = _rope16(y, cs, sm, sp).astype(BF16)
    vt = vs_ref[...].astype(F32).T
    wt = vw_ref[...].astype(F32).T
    for g in range(NSA_GROUPS):
        rows = slice(g * HEAD_DIM, (g + 1) * HEAD_DIM)
        vst_ref[g, 0:HEAD_DIM, :] = vt[rows, :].astype(BF16)
        vst_ref[g, HEAD_DIM:VROWS, :] = _ones_row_pad(TM_PREP).astype(BF16)
        for r in range(TM_PREP // LANES):
            vwt_ref[g, r, 0:HEAD_DIM, :] = wt[rows, r * LANES:(r + 1) * LANES].astype(BF16)
            vwt_ref[g, r, HEAD_DIM:VROWS, :] = _ones_row_pad(LANES).astype(BF16)
    gt_ref[...] = jax.nn.sigmoid(ng_ref[...]).T


def _nsa_prep(pa, pb, tabs, qw, ksw, kww, bd, b, s):
    cs, sm, sp = tabs
    nt = s // TM_PREP
    tokb = lambda w, j: pl.BlockSpec((TM_PREP, w), lambda bi, si, j=j: (bi * nt + si, j))
    tab = pl.BlockSpec((TM_PREP, LANES), lambda bi, si: (si, 0))
    vec = pl.BlockSpec((1, LANES), lambda bi, si: (0, 0))
    kv_w = NSA_GROUPS * LANES
    out_shape = (
        jax.ShapeDtypeStruct((b, s, NSA_Q), BF16),
        jax.ShapeDtypeStruct((b, s, 2 * kv_w), BF16),
        jax.ShapeDtypeStruct((b, s, kv_w), BF16),
        jax.ShapeDtypeStruct((b, NSA_GROUPS, nt, VROWS, TM_PREP), BF16),
        jax.ShapeDtypeStruct((b, NSA_GROUPS, s // LANES, VROWS, LANES), BF16),
        jax.ShapeDtypeStruct((b, LANES, s), F32),
    )
    out_specs = (
        pl.BlockSpec((None, TM_PREP, NSA_Q), lambda bi, si: (bi, si, 0)),
        pl.BlockSpec((None, TM_PREP, 2 * kv_w), lambda bi, si: (bi, si, 0)),
        pl.BlockSpec((None, TM_PREP, kv_w), lambda bi, si: (bi, si, 0)),
        pl.BlockSpec((None, NSA_GROUPS, None, VROWS, TM_PREP), lambda bi, si: (bi, 0, si, 0, 0)),
        pl.BlockSpec((None, NSA_GROUPS, TM_PREP // LANES, VROWS, LANES), lambda bi, si: (bi, 0, si, 0, 0)),
        pl.BlockSpec((None, LANES, TM_PREP), lambda bi, si: (bi, 0, si)),
    )
    return pl.pallas_call(
        _prep_kernel,
        out_shape=out_shape,
        grid=(b, nt),
        in_specs=[
            tokb(NSA_Q, PA_NQ // NSA_Q),
            tokb(kv_w, PB_KS // kv_w), tokb(kv_w, PB_KW // kv_w),
            tokb(NSA_KV, PB_VS // NSA_KV), tokb(NSA_KV, PB_VW // NSA_KV),
            tokb(LANES, PA_NG // LANES),
            tab, tab, tab, vec, vec, vec,
            pl.BlockSpec((LANES, LANES), lambda bi, si: (0, 0)),
        ],
        out_specs=out_specs,
        compiler_params=_cparams(("parallel", "parallel")),
        name="nsa_prep",
    )(pa, pb, pb, pb, pb, pa, cs, sm, sp, qw, ksw, kww, bd)


def _compress_kernel(x_ref, w1p_ref, w1a_ref, w1b_ref, pe_ref, w2_ref, cs_ref, sm_ref, sp_ref, nw_ref, o_ref,
                     *, is_key, n):
    a = bb = None
    for j in range(CMP_STRIDE):
        xj = x_ref[pl.ds(j, n, stride=CMP_STRIDE), :].astype(BF16)
        da = _dot(xj, w1p_ref[j])
        db = _dot(xj, w1p_ref[CMP_STRIDE + j])
        a = da if a is None else a + da
        bb = db if bb is None else bb + db
    pe_hi, pe_lo = _split_bf16(pe_ref[...])
    w1 = jnp.concatenate([w1a_ref[...], w1b_ref[...]], axis=0)
    pe_term = (_dot(pe_hi, w1) + _dot(pe_lo, w1))[0:1, :]
    hid = a + pltpu.roll(bb, n - 1, 0) + pe_term
    t = hid * (0.7978845608028654 * (1.0 + 0.044715 * hid * hid))
    act = 0.5 * hid * (1.0 + jnp.tanh(t))
    y = _dot(act.astype(BF16), w2_ref[...])
    if is_key:
        y = y * lax.rsqrt(jnp.mean(y * y, axis=-1, keepdims=True) + NORM_EPS) * nw_ref[...]
        o_ref[...] = _rope16(y, cs_ref[...], sm_ref[...], sp_ref[...]).astype(BF16)
    else:
        o_ref[...] = jnp.concatenate([y.T[0:HEAD_DIM, :], _ones_row_pad(n)], axis=0).astype(BF16)


def _compress(pa, w1p, w1a, w1b, pe, w2, tabs, nw, is_key, b, s):
    g, n = NSA_GROUPS, s // CMP_STRIDE
    cs, sm, sp = tabs
    full2 = lambda shp: pl.BlockSpec(shp, lambda bi, gi: (0, 0))
    out_block = (None, None, n, LANES) if is_key else (None, None, VROWS, n)
    out_shape = (b, g, n, LANES) if is_key else (b, g, VROWS, n)
    return pl.pallas_call(
        functools.partial(_compress_kernel, is_key=is_key, n=n),
        out_shape=jax.ShapeDtypeStruct(out_shape, BF16),
        grid=(b, g),
        in_specs=[
            pl.BlockSpec((s, LANES), lambda bi, gi: (bi, PA_C // LANES + (0 if is_key else 1))),
            pl.BlockSpec((None, CMP_LEN, LANES, CMP_HIDDEN), lambda bi, gi: (gi, 0, 0, 0)),
            full2(w1a.shape), full2(w1b.shape), full2(pe.shape), full2(w2.shape),
            full2(cs.shape), full2(sm.shape), full2(sp.shape), full2(nw.shape),
        ],
        out_specs=pl.BlockSpec(out_block, lambda bi, gi: (bi, gi, 0, 0)),
        compiler_params=_cparams(("parallel", "parallel")),
        name="compress_k" if is_key else "compress_v",
    )(pa, w1p, w1a, w1b, pe, w2, cs, sm, sp, nw)


def _weighted_values(vt, e):
    acc = _dot(vt, e.astype(BF16))
    return acc[0:HEAD_DIM, :], acc[HEAD_DIM:HEAD_DIM + 1, :]


def _attn_kernel(q_ref, kc_ref, vct_ref, ksa_ref, vst_ref, kw_ref, vwt_ref, gt_ref, ovt_ref,
                 o_ref, s_ref, cm_ref, qa_ref):
    gi = pl.program_id(1)
    q0 = pl.program_id(2) * TQ
    n_cmp = kc_ref.shape[0]
    ncols = GROUP_HEADS * TQ

    low = lax.broadcasted_iota(I32, (TQ, LANES), 1) < HEAD_DIM
    zero = jnp.zeros((TQ, LANES), BF16)
    parts = []
    for h in range(GROUP_HEADS):
        slab = q_ref[:, (h // 2) * LANES:(h // 2 + 1) * LANES]
        parts.append(jnp.where(low, slab, zero) if h % 2 == 0 else jnp.where(low, zero, slab))
    qs = jnp.concatenate(parts, axis=0)
    t_row = q0 + lax.broadcasted_iota(I32, (1, TQ), 1)

    def all_heads(bias):
        return jnp.concatenate([bias] * GROUP_HEADS, axis=1)

    def produce(keys, queries, slot, keep_max):
        st = _dot_nt(keys, queries)
        s_ref[slot, 0:keys.shape[0], :] = st
        if keep_max:
            cm_ref[slot] = jnp.broadcast_to(jnp.max(st, axis=0, keepdims=True), (8, ncols))

    def consume(slot, nk, vt, carry, bias):
        m_old, a_old = carry
        st = s_ref[slot, 0:nk, :]
        if bias is None:
            cm = cm_ref[slot][0:1, :]
        else:
            st = st + all_heads(bias)
            cm = jnp.max(st, axis=0, keepdims=True)
        m_new = jnp.maximum(m_old, cm)
        alpha = jnp.exp2(m_old - m_new)
        e = jnp.exp2(st - m_new)
        return m_new, alpha * a_old + _dot(vt, e.astype(BF16))

    init = (jnp.full((1, ncols), NEG, F32), jnp.zeros((VROWS, ncols), F32))

    def finish(acc):
        return acc[0:HEAD_DIM, :] * (1.0 / acc[HEAD_DIM:HEAD_DIM + 1, :])

    w0 = jnp.maximum(q0 - WINDOW, 0)
    wt0 = lax.shift_right_logical(w0, int(np.log2(LANES)))
    n_wb = WIN_TILES * LANES - TK
    produce(kc_ref[...], qs, 1, False)
    produce(kw_ref[pl.ds(pl.multiple_of(w0, LANES), TK), :], qs, 0, False)

    cmp_end = lax.broadcasted_iota(I32, (n_cmp, 1), 0) * CMP_STRIDE + (CMP_LEN - 1)
    st = s_ref[1, 0:n_cmp, :] + all_heads(jnp.where(cmp_end <= t_row, 0.0, NEG))
    m = jnp.max(st, axis=0, keepdims=True)
    e = jnp.exp2(st - jnp.where(m > 0.5 * NEG, m, 0.0))
    oc_all, l = _weighted_values(vct_ref[...], e)
    rl = 1.0 / jnp.where(l > 0.0, l, 1.0)
    oc_all = oc_all * rl
    psum = e[:, 0:TQ] * rl[:, 0:TQ]
    for h in range(1, GROUP_HEADS):
        psum = psum + e[:, h * TQ:(h + 1) * TQ] * rl[:, h * TQ:(h + 1) * TQ]
    ph, plo = _split_bf16(psum)
    imp_t = _dot(ovt_ref[...], ph) + _dot(ovt_ref[...], plo)

    def wbias(start, nk):
        kpos = start + lax.broadcasted_iota(I32, (nk, 1), 0)
        return jnp.where((kpos <= t_row) & (kpos > t_row - WINDOW), 0.0, NEG)

    produce(kw_ref[pl.ds(pl.multiple_of(w0 + TK, LANES), n_wb), :], qs, 1, False)
    vwa = jnp.concatenate([vwt_ref[wt0 + r] for r in range(TK // LANES)], axis=1)
    carry_w = consume(0, TK, vwa, init, wbias(w0, TK))
    produce(ksa_ref[0:TK, 0:LANES], qs, 0, False)

    jrow = lax.broadcasted_iota(I32, (LANES, TQ), 0)
    cur = lax.shift_right_logical(q0 + lax.broadcasted_iota(I32, (LANES, TQ), 1), int(np.log2(SLC_BLOCK)))
    forced = (jrow == 0) | (jrow == cur) | (jrow == cur - 1)
    valid = jrow <= cur
    bias = jnp.where(forced & valid, 0.0, NEG)
    val = jnp.where(valid & jnp.logical_not(forced), imp_t, -jnp.inf)
    for _ in range(SLC_TOPK - 3):
        mx = jnp.max(val, axis=0, keepdims=True)
        first = jnp.min(jnp.where(val == mx, jrow, LANES), axis=0, keepdims=True)
        pick = jrow == first
        bias = jnp.where(pick & valid, 0.0, bias)
        val = jnp.where(pick, -jnp.inf, val)

    vwb = jnp.concatenate([vwt_ref[wt0 + TK // LANES + r] for r in range(n_wb // LANES)], axis=1)
    _, a_w = consume(1, n_wb, vwb, carry_w, wbias(w0 + TK, n_wb))
    ow_all = finish(a_w)

    selb = bias.T.astype(BF16)
    qa_ref[...] = jnp.concatenate([qs, jnp.concatenate([selb] * GROUP_HEADS, axis=0)], axis=1)

    def sel_scores(ti, slot):
        produce(ksa_ref[pl.ds(pl.multiple_of(ti * TK, TK), TK), :], qa_ref[...], slot, True)

    def sel_consume(ti, slot, carry, causal):
        cb = None
        if causal:
            cb = jnp.where(ti * TK + lax.broadcasted_iota(I32, (TK, 1), 0) <= t_row, 0.0, NEG)
        return consume(slot, TK, vst_ref[ti], carry, cb)

    n_full = lax.shift_right_logical(q0, int(np.log2(TK)))
    rows_per_tile = TK // SLC_BLOCK
    bias0 = jnp.concatenate([jnp.broadcast_to(bias[r:r + 1, :], (SLC_BLOCK, TQ)) for r in range(rows_per_tile)],
                            axis=0)
    st0 = s_ref[0] + all_heads(bias0)
    s_ref[0] = st0
    cm_ref[0] = jnp.broadcast_to(jnp.max(st0, axis=0, keepdims=True), (8, ncols))

    def two_tiles(j, carry):
        sel_scores(2 * j + 1, 1)
        carry = sel_consume(2 * j, 0, carry, False)
        sel_scores(2 * j + 2, 0)
        return sel_consume(2 * j + 1, 1, carry, False)

    n_pairs = lax.shift_right_logical(n_full, 1)
    n_quads = lax.shift_right_logical(n_pairs, 1)
    carry = lax.fori_loop(0, n_quads, lambda j, c: two_tiles(2 * j + 1, two_tiles(2 * j, c)), init)
    carry = lax.fori_loop(2 * n_quads, n_pairs, two_tiles, carry)

    def odd_tail(c):
        sel_scores(n_full, 1)
        return sel_consume(n_full, 1, sel_consume(n_full - 1, 0, c, False), True)

    _, a_s = lax.cond((n_full & 1) == 1, odd_tail, lambda c: sel_consume(n_full, 0, c, True), carry)
    os_all = finish(a_s)

    for p in range(GROUP_HEADS // 2):
        halves = []
        for h in (2 * p, 2 * p + 1):
            c = slice(h * TQ, (h + 1) * TQ)
            gbase = (gi * GROUP_HEADS + h) * 3
            halves.append(gt_ref[pl.ds(gbase, 1), :] * oc_all[:, c]
                          + gt_ref[pl.ds(gbase + 1, 1), :] * os_all[:, c]
                          + gt_ref[pl.ds(gbase + 2, 1), :] * ow_all[:, c])
        o_ref[:, p * LANES:(p + 1) * LANES] = jnp.concatenate(halves, axis=0).T.astype(BF16)


def _attention(q, kc, vct, ksa, vst, kw, vwt, gt, ovt, b, s):
    nq = s // TQ
    n_cmp = kc.shape[2]
    gw = GROUP_HEADS * HEAD_DIM
    ncols = GROUP_HEADS * TQ
    return pl.pallas_call(
        _attn_kernel,
        out_shape=jax.ShapeDtypeStruct((b, s, NSA_Q), BF16),
        grid=(b, NSA_GROUPS, nq),
        in_specs=[
            pl.BlockSpec((None, TQ, gw), lambda bi, gi, qi: (bi, qi, gi)),
            pl.BlockSpec((None, None, n_cmp, LANES), lambda bi, gi, qi: (bi, gi, 0, 0)),
            pl.BlockSpec((None, None, VROWS, n_cmp), lambda bi, gi, qi: (bi, gi, 0, 0)),
            pl.BlockSpec((None, s, 2 * LANES), lambda bi, gi, qi: (bi, 0, gi)),
            pl.BlockSpec((None, None, s // TK, VROWS, TK), lambda bi, gi, qi: (bi, gi, 0, 0, 0)),
            pl.BlockSpec((None, s, LANES), lambda bi, gi, qi: (bi, 0, gi)),
            pl.BlockSpec((None, None, s // LANES, VROWS, LANES), lambda bi, gi, qi: (bi, gi, 0, 0, 0)),
            pl.BlockSpec((None, LANES, TQ), lambda bi, gi, qi: (bi, 0, qi)),
            pl.BlockSpec((LANES, n_cmp), lambda bi, gi, qi: (0, 0)),
        ],
        out_specs=pl.BlockSpec((None, TQ, gw), lambda bi, gi, qi: (bi, qi, gi)),
        scratch_shapes=[
            pltpu.VMEM((2, TK, ncols), F32),
            pltpu.VMEM((2, 8, ncols), F32),
            pltpu.VMEM((ncols, 2 * LANES), BF16),
        ],
        compiler_params=_cparams(("parallel", "parallel", "arbitrary")),
        name="nsa_attention",
    )(q, kc, vct, ksa, vst, kw, vwt, gt, ovt)


def _router_kernel(x_ref, nw_ref, wt_ref, b_ref, tri_ref, h_ref, e_ref, w_ref, r_ref, cnt_ref, base_ref):
    @pl.when(pl.program_id(0) == 0)
    def _():
        base_ref[...] = jnp.zeros_like(base_ref)

    x = x_ref[...]
    h = x * lax.rsqrt(jnp.mean(x * x, axis=-1, keepdims=True) + NORM_EPS) * nw_ref[...]
    _store_row_tiles(h_ref, h)
    h_hi, h_lo = _split_bf16(h)
    w_hi, w_lo = _split_bf16(wt_ref[...])
    lg = _dot_nt(w_hi, h_hi) + _dot_nt(w_hi, h_lo) + _dot_nt(w_lo, h_hi) + b_ref[...]
    erow = lax.broadcasted_iota(I32, lg.shape, 0)
    vals, hots = [], []
    for _ in range(TOP_K):
        mx = jnp.max(lg, axis=0, keepdims=True)
        first = jnp.min(jnp.where(lg == mx, erow, N_EXPERTS), axis=0, keepdims=True)
        hot = erow == first
        vals.append(mx)
        hots.append(hot)
        lg = jnp.where(hot, -jnp.inf, lg)
    ex = [jnp.exp(v - vals[0]) for v in vals]
    den = ex[0] + ex[1] + ex[2] + ex[3]
    onehots = [h.astype(F32) for h in hots]
    cnt = onehots[0] + onehots[1] + onehots[2] + onehots[3]
    pref = _dot(cnt.astype(BF16), tri_ref[...]) + base_ref[:, 0:1]
    erow_f = erow.astype(F32)
    idxs = [jnp.sum(oh * erow_f, axis=0, keepdims=True) for oh in onehots]
    ranks = [jnp.sum(oh * pref, axis=0, keepdims=True) for oh in onehots]
    base_ref[...] = base_ref[...] + jnp.sum(cnt, axis=1, keepdims=True)
    cnt_ref[...] = base_ref[...]
    pad_f = [jnp.zeros_like(den)] * (8 - TOP_K)
    e_ref[...] = jnp.concatenate(idxs + pad_f, axis=0).astype(I32)
    r_ref[...] = jnp.concatenate(ranks + pad_f, axis=0).astype(I32)
    w_ref[...] = jnp.concatenate([e / den for e in ex] + pad_f, axis=0)


def _router(x2, nw, wt, bcol, tri):
    n = x2.shape[0]
    small = pl.BlockSpec((8, TM_PROJ), lambda i: (0, i))
    return pl.pallas_call(
        _router_kernel,
        out_shape=(jax.ShapeDtypeStruct(_tiled(n), F32),
                   jax.ShapeDtypeStruct((8, n), I32),
                   jax.ShapeDtypeStruct((8, n), F32),
                   jax.ShapeDtypeStruct((8, n), I32),
                   jax.ShapeDtypeStruct((N_EXPERTS, LANES), F32)),
        grid=(n // TM_PROJ,),
        in_specs=[
            pl.BlockSpec((TM_PROJ, D_MODEL), lambda i: (i, 0)),
            pl.BlockSpec((1, D_MODEL), lambda i: (0, 0)),
            pl.BlockSpec((N_EXPERTS, D_MODEL), lambda i: (0, 0)),
            pl.BlockSpec((N_EXPERTS, 1), lambda i: (0, 0)),
            pl.BlockSpec((TM_PROJ, TM_PROJ), lambda i: (0, 0)),
        ],
        out_specs=(pl.BlockSpec(_tiled(TM_PROJ), lambda i: (i, 0)), small, small, small,
                   pl.BlockSpec((N_EXPERTS, LANES), lambda i: (0, 0))),
        scratch_shapes=[pltpu.VMEM((N_EXPERTS, LANES), F32)],
        compiler_params=_cparams(("arbitrary",)),
        name="router",
    )(x2, nw, wt, bcol, tri)


def _row_copy(src, src_row, dst, dst_row, sem):
    tile = lambda ref, r: ref.at[pl.ds(pl.multiple_of(r * SUB, SUB), SUB), :]
    return pltpu.make_async_copy(tile(src, src_row), tile(dst, dst_row), sem)


def _issue_rows(nrow, copy_of):
    def body(g, c):
        for j in range(8):
            copy_of(g * 8 + j).start(priority=j % 2)
        return c

    lax.fori_loop(0, nrow // 8, body, 0)


def _wait_rows(nrow, copy_of):
    def body(g, c):
        for j in range(8):
            copy_of(0).wait()
        return c

    lax.fori_loop(0, nrow // 8, body, 0)


def _store_row_tiles(ref, x):
    for c in range(SUB):
        ref[pl.ds(c, x.shape[0], stride=SUB), :] = x[:, c * LANES:(c + 1) * LANES]


def _load_row_tiles(ref, first_row, nrows):
    return jnp.concatenate([ref[pl.ds(first_row * SUB + c, nrows, stride=SUB), :] for c in range(SUB)], axis=1)


def _dispatch_kernel(zinfo_ref, dest_ref, h_ref, x_hbm, zbuf, sem, zsem, *, n_blocks):
    def zero_block(row):
        return pltpu.make_async_copy(
            zbuf, x_hbm.at[pl.ds(pl.multiple_of(row * SUB, TM_MOE * SUB), TM_MOE * SUB), :], zsem)

    @pl.when(pl.program_id(0) == 0)
    def _():
        zbuf[...] = jnp.zeros_like(zbuf)
        for e in range(N_EXPERTS):
            zero_block(zinfo_ref[e]).start()
        for e in range(N_EXPERTS):
            zero_block(0).wait()
        nvalid = zinfo_ref[N_EXPERTS]

        def ztail(bk, c):
            zero_block(bk * TM_MOE).start()
            return c

        lax.fori_loop(nvalid, n_blocks, ztail, 0)

        def zwait(bk, c):
            zero_block(0).wait()
            return c

        lax.fori_loop(nvalid, n_blocks, zwait, 0)

    nrow = TOP_K * TM_CMB
    _issue_rows(nrow, lambda r: _row_copy(h_ref, r & (TM_CMB - 1), x_hbm, dest_ref[0, 0, r], sem))
    _wait_rows(nrow, lambda r: _row_copy(h_ref, 0, x_hbm, 0, sem))


def _dispatch(zinfo, dest_b, h3, rows):
    n = h3.shape[0] // SUB
    grid_spec = pltpu.PrefetchScalarGridSpec(
        num_scalar_prefetch=1,
        grid=(n // TM_CMB,),
        in_specs=[
            pl.BlockSpec((1, 1, TOP_K * TM_CMB), lambda i, z: (i, 0, 0), memory_space=pltpu.SMEM),
            pl.BlockSpec(_tiled(TM_CMB), lambda i, z: (i, 0)),
        ],
        out_specs=pl.BlockSpec(memory_space=pl.ANY),
        scratch_shapes=[pltpu.VMEM(_tiled(TM_MOE), F32), pltpu.SemaphoreType.DMA(()),
                        pltpu.SemaphoreType.DMA(())],
    )
    return pl.pallas_call(
        functools.partial(_dispatch_kernel, n_blocks=rows // TM_MOE),
        out_shape=jax.ShapeDtypeStruct(_tiled(rows), F32),
        grid_spec=grid_spec,
        compiler_params=_cparams(("arbitrary",)),
        name="dispatch",
    )(zinfo, dest_b, h3)


def _expert_kernel(blk_e_ref, nvalid_ref, x_ref, wgu_ref, bgu_ref, wd_ref, bd_ref, y_ref, wgu_bf, wd_bf):
    i = pl.program_id(0)
    used = i < nvalid_ref[0]

    @pl.when(jnp.logical_not(used))
    def _():
        y_ref[...] = jnp.zeros_like(y_ref)

    @pl.when(used & ((i == 0) | (blk_e_ref[i] != blk_e_ref[jnp.maximum(i - 1, 0)])))
    def _():
        wgu_bf[...] = wgu_ref[...].astype(BF16)
        wd_bf[...] = wd_ref[...].astype(BF16)

    @pl.when(used)
    def _():
        gu = _dot(_load_row_tiles(x_ref, 0, TM_MOE).astype(BF16), wgu_bf[...]) + bgu_ref[...]
        gate = jnp.minimum(gu[:, :D_FF], SWIGLU_LIMIT)
        up = jnp.clip(gu[:, D_FF:], -SWIGLU_LIMIT, SWIGLU_LIMIT)
        glu = gate * jax.nn.sigmoid(gate * SWIGLU_ALPHA)
        _store_row_tiles(y_ref, _dot(((up + 1.0) * glu).astype(BF16), wd_bf[...]) + bd_ref[...])


def _experts(blk_e, nvalid, xbuf, wgu, bgu, wd, bd, layer):
    n_blocks = blk_e.shape[0]
    wblk = lambda i, be, nv: (be[i], 0, 0)
    wblk4 = lambda i, be, nv: (layer, be[i], 0, 0)
    grid_spec = pltpu.PrefetchScalarGridSpec(
        num_scalar_prefetch=2,
        grid=(n_blocks,),
        in_specs=[
            pl.BlockSpec(_tiled(TM_MOE), lambda i, be, nv: (jnp.minimum(i, nv[0] - 1), 0)),
            pl.BlockSpec((None, None, D_MODEL, 2 * D_FF), wblk4),
            pl.BlockSpec((None, 1, 2 * D_FF), wblk),
            pl.BlockSpec((None, None, D_FF, D_MODEL), wblk4),
            pl.BlockSpec((None, 1, D_MODEL), wblk),
        ],
        out_specs=pl.BlockSpec(_tiled(TM_MOE), lambda i, be, nv: (i, 0)),
        scratch_shapes=[pltpu.VMEM((D_MODEL, 2 * D_FF), BF16), pltpu.VMEM((D_FF, D_MODEL), BF16)],
    )
    return pl.pallas_call(
        _expert_kernel,
        out_shape=jax.ShapeDtypeStruct(_tiled(n_blocks * TM_MOE), F32),
        grid_spec=grid_spec,
        compiler_params=_cparams(("arbitrary",)),
        name="experts",
    )(blk_e, nvalid, xbuf, wgu, bgu, wd, bd)


def _combine_kernel(dcur_ref, dnext_ref, x_ref, w_ref, y_hbm, o_ref, ybuf, sem):
    i = pl.program_id(0)
    nrow = TOP_K * TM_CMB
    slot = i & 1

    def issue(d_ref, s):
        _issue_rows(nrow, lambda r: _row_copy(y_hbm, d_ref[0, 0, r], ybuf.at[s], r, sem.at[s]))

    @pl.when(i == 0)
    def _():
        issue(dcur_ref, 0)

    @pl.when(i + 1 < pl.num_programs(0))
    def _():
        issue(dnext_ref, 1 - slot)

    _wait_rows(nrow, lambda r: _row_copy(y_hbm, 0, ybuf.at[slot], 0, sem.at[slot]))
    acc = x_ref[...]
    w = w_ref[...]
    for k in range(TOP_K):
        acc = acc + w[:, k:k + 1] * _load_row_tiles(ybuf.at[slot], k * TM_CMB, TM_CMB)
    o_ref[...] = acc


def _combine(dest_b, x2, wts, ybuf):
    n = x2.shape[0]
    nt = n // TM_CMB
    return pl.pallas_call(
        _combine_kernel,
        out_shape=jax.ShapeDtypeStruct((n, D_MODEL), F32),
        grid=(nt,),
        in_specs=[
            pl.BlockSpec((1, 1, TOP_K * TM_CMB), lambda i: (i, 0, 0), memory_space=pltpu.SMEM),
            pl.BlockSpec((1, 1, TOP_K * TM_CMB), lambda i: (jnp.minimum(i + 1, nt - 1), 0, 0),
                         memory_space=pltpu.SMEM),
            pl.BlockSpec((TM_CMB, D_MODEL), lambda i: (i, 0)),
            pl.BlockSpec((TM_CMB, TOP_K), lambda i: (i, 0)),
            pl.BlockSpec(memory_space=pl.ANY),
        ],
        out_specs=pl.BlockSpec((TM_CMB, D_MODEL), lambda i: (i, 0)),
        scratch_shapes=[pltpu.VMEM((2,) + _tiled(TOP_K * TM_CMB), F32), pltpu.SemaphoreType.DMA((2,))],
        compiler_params=_cparams(("arbitrary",)),
        name="combine",
    )(dest_b, dest_b, x2, wts, ybuf)


def _route(e_t, r_t, cnt, n):
    counts = cnt[:, 0].astype(I32)
    padded = (counts + TM_MOE - 1) // TM_MOE * TM_MOE
    pend = jnp.cumsum(padded)
    pstart = pend - padded
    n_blocks = n * TOP_K // TM_MOE + N_EXPERTS
    rows = n_blocks * TM_MOE
    dest = r_t[:TOP_K]
    for e in range(N_EXPERTS):
        dest = dest + jnp.where(e_t[:TOP_K] == e, pstart[e].astype(I32), 0)
    nt = n // TM_CMB
    dest_b = dest.reshape(TOP_K, nt, TM_CMB).transpose(1, 0, 2).reshape(nt, 1, TOP_K * TM_CMB)
    blk_row = jnp.arange(n_blocks, dtype=I32)[:, None] * TM_MOE
    blk_e = jnp.minimum(jnp.sum((pend[None, :] <= blk_row).astype(I32), axis=1), N_EXPERTS - 1)
    nvalid = (pend[-1:] // TM_MOE).astype(I32)
    zrow = jnp.clip(pstart + padded - TM_MOE, 0, rows - TM_MOE).astype(I32)
    return dest_b, blk_e, nvalid, jnp.concatenate([zrow, nvalid]), rows


def _layer_weights(l, w_in, norm1_w, ret_gn_w, qk_norm_w, cmp_pos, cmp_w1, cmp_w2, w_o_ret, w_o_nsa, w_out,
                   norm2_w, router_w, router_b, w_gate_up, b_gate_up, w_down, b_down):
    w = w_in[l]
    o = np.cumsum((0, RET_QK, RET_QK, RET_V, RET_V, NSA_Q) + (NSA_KV,) * 6 + (NSA_GATE, 2 * D_MODEL))
    rq, rk, rv, rg, nq = (w[:, o[i]:o[i + 1]] for i in range(5))
    kc, vc, ksl, vsl, kwi, vwi = (w[:, o[5 + i]:o[6 + i]] for i in range(6))
    ng, mg = w[:, o[11]:o[12]], w[:, o[12]:o[13]]

    def dup(t):
        t = t.reshape(D_MODEL, NSA_GROUPS, 1, HEAD_DIM)
        return jnp.broadcast_to(t, (D_MODEL, NSA_GROUPS, 2, HEAD_DIM)).reshape(D_MODEL, NSA_GROUPS * LANES)

    ng_pad = jnp.pad(ng, ((0, 0), (0, LANES - NSA_GATE)))
    wa = jnp.concatenate([mg, rg, nq, kc, vc, ng_pad], axis=1).astype(BF16)
    wb = jnp.concatenate([rq, rk, rv, dup(ksl), dup(kwi), vsl, vwi], axis=1).astype(BF16)
    qk = qk_norm_w[l]
    tile = lambda v: jnp.tile(v, LANES // HEAD_DIM)[None, :]
    w1 = cmp_w1[l].astype(BF16)
    half = CMP_STRIDE * HEAD_DIM
    w1r = w1.reshape(2, 1, CMP_LEN, 1, HEAD_DIM, CMP_HIDDEN)
    gsel = jnp.eye(NSA_GROUPS, dtype=BF16).reshape(1, NSA_GROUPS, 1, NSA_GROUPS, 1, 1)
    w1p = (w1r * gsel).reshape(2, NSA_GROUPS, CMP_LEN, LANES, CMP_HIDDEN)
    w2 = jnp.concatenate([cmp_w2[l], cmp_w2[l]], axis=-1).astype(BF16)
    pe = jnp.broadcast_to(cmp_pos[l].reshape(2, 1, CMP_LEN * HEAD_DIM), (2, 8, CMP_LEN * HEAD_DIM))
    return dict(
        nw1=norm1_w[l][None, :], wa=wa, wb=wb, gnw=ret_gn_w[l][None, :],
        qw=tile(qk[0]), kcw=tile(qk[1]), ksw=tile(qk[2]), kww=tile(qk[3]),
        w1a=w1[:, :half], w1b=w1[:, half:], w1p=w1p, w2=w2, pe=pe,
        wr=w_o_ret[l].astype(BF16), wn=w_o_nsa[l].astype(BF16), wo=w_out[l].astype(BF16),
        nw2=norm2_w[l][None, :], rwt=router_w[l].T, rb=router_b[l][:, None],
        bgu=b_gate_up[l][:, None, :], bd=b_down[l][:, None, :],
    )


def kernel(x, norm1_w, w_in, ret_gn_w, qk_norm_w, cmp_pos, cmp_w1, cmp_w2, w_o_ret, w_o_nsa, w_out, norm2_w,
           router_w, router_b, w_gate_up, b_gate_up, w_down, b_down):
    b, s, _ = x.shape
    depth = w_in.shape[0]
    n = b * s
    n_slc = s // SLC_BLOCK
    assert s % TC_RET == 0 and s % TK == 0 and s >= WIN_TILES * LANES and n % TM_PROJ == 0
    assert n_slc <= LANES
    assert s // CMP_STRIDE <= TK

    pos = jnp.arange(s)
    inv = RET_THETA ** (-jnp.arange(0, RET_DK, 2, dtype=F32) / RET_DK)
    ang = pos.astype(F32)[:, None] * inv[None, :]
    ret_cos = jnp.concatenate([jnp.cos(ang), jnp.cos(ang)], axis=1)
    ret_sin = jnp.concatenate([-jnp.sin(ang), jnp.sin(ang)], axis=1)
    dec, zeta_b, xi_b, chunk_decay = _ret_consts()
    ret_tabs = (ret_cos, ret_sin, jnp.asarray(dec), jnp.asarray(zeta_b), jnp.asarray(xi_b), chunk_decay)
    tok_tabs = _rope_half_tables(pos, ROPE_DIM, ROPE_THETA, HEAD_DIM)
    n_cmp = s // CMP_STRIDE
    cmp_tabs = _rope_half_tables(jnp.arange(n_cmp) * CMP_STRIDE + CMP_LEN - 1, ROPE_DIM, ROPE_THETA, HEAD_DIM)
    li = np.arange(LANES)
    bd = jnp.asarray((li[:, None] // HEAD_DIM == li[None, :] // HEAD_DIM).astype(np.float32), BF16)
    ci, sj = np.arange(n_cmp)[None, :], np.arange(LANES)[:, None]
    ovt = ((ci * CMP_STRIDE < (sj + 1) * SLC_BLOCK) & (ci * CMP_STRIDE + CMP_LEN > sj * SLC_BLOCK)
           & (ci < n_cmp - CMP_LEN // CMP_STRIDE + 1) & (sj < n_slc))
    ovt = jnp.asarray(ovt.astype(np.float32), BF16)
    ti = np.arange(TM_PROJ)
    tri = jnp.asarray((ti[:, None] < ti[None, :]).astype(np.float32), BF16)

    x2 = x.reshape(n, D_MODEL)
    for l in range(depth):
        p = _layer_weights(l, w_in, norm1_w, ret_gn_w, qk_norm_w, cmp_pos, cmp_w1, cmp_w2, w_o_ret, w_o_nsa,
                           w_out, norm2_w, router_w, router_b, w_gate_up, b_gate_up, w_down, b_down)
        pa, pb = _inproj(x2, p["nw1"], p["wa"], p["wb"])
        ret = _retention(pa, pb, p["gnw"], ret_tabs, b, s)
        q, ksa, kw, vst, vwt, gt = _nsa_prep(pa, pb, tok_tabs, p["qw"], p["ksw"], p["kww"], bd, b, s)
        kcmp, vcmp_t = (_compress(pa, p["w1p"][kv], p["w1a"][kv], p["w1b"][kv], p["pe"][kv], p["w2"][kv],
                                  cmp_tabs, p["kcw"], kv == 0, b, s) for kv in range(2))
        att = _attention(q, kcmp, vcmp_t, ksa, vst, kw, vwt, gt, ovt, b, s)
        x2 = _outproj(x2, ret, att.reshape(n, NSA_Q), pa, p["wr"], p["wn"], p["wo"])
        h2, e_t, w_t, r_t, cnt = _router(x2, p["nw2"], p["rwt"], p["rb"], tri)
        dest_b, blk_e, nvalid, zinfo, rows = _route(e_t, r_t, cnt, n)
        xbuf = _dispatch(zinfo, dest_b, h2, rows)
        ybuf = _experts(blk_e, nvalid, xbuf, w_gate_up, p["bgu"], w_down, p["bd"], l)
        x2 = _combine(dest_b, x2, w_t[:TOP_K].T, ybuf)
    return x2.reshape(b, s, D_MODEL)
```

```python
import functools

import numpy as np
import jax
import jax.numpy as jnp
from jax import lax
from jax.experimental import pallas as pl
from jax.experimental.pallas import tpu as pltpu

F32 = jnp.float32
BF16 = jnp.bfloat16
I32 = jnp.int32

D_MODEL = 1024
RET_HEADS, RET_DK, RET_DV, RET_CHUNK, RET_THETA = 4, 128, 256, 128, 10000.0
NSA_HEADS, NSA_GROUPS, HEAD_DIM = 16, 2, 64
GROUP_HEADS = NSA_HEADS // NSA_GROUPS
ROPE_DIM, ROPE_THETA = HEAD_DIM // 4, 500000.0
CMP_LEN, CMP_STRIDE, CMP_HIDDEN = 32, 16, 4 * HEAD_DIM
SLC_BLOCK, SLC_TOPK, WINDOW = 64, 16, 512
N_EXPERTS, TOP_K, D_FF = 32, 4, D_MODEL
SWIGLU_LIMIT, SWIGLU_ALPHA = 7.0, 1.702
NORM_EPS = 1e-6
RET_QK, RET_V = RET_HEADS * RET_DK, RET_HEADS * RET_DV
NSA_Q, NSA_KV, NSA_GATE = NSA_HEADS * HEAD_DIM, NSA_GROUPS * HEAD_DIM, NSA_HEADS * 3

LANES = 128
NEG = -1e30
LOG2E = 1.4426950408889634
VMEM_LIMIT = 56 * 1024 * 1024

TM_PROJ = 256
TC_RET = 1024
TM_PREP = 512
TQ = 128
TK = 512
WIN_TILES = WINDOW // LANES + 1
VROWS = HEAD_DIM + 16
TM_MOE = 512
TM_CMB = 256
SUB = 8


def _tiled(nrows):
    return (nrows * SUB, LANES)

PA_MG, PA_RG, PA_NQ, PA_C, PA_NG = 0, 2048, 3072, 4096, 4352
PA_W = 4480
PB_RQ, PB_RK, PB_RV, PB_KS, PB_KW, PB_VS, PB_VW = 0, 512, 1024, 2048, 2304, 2560, 2688
PB_W = 2816


def _cparams(sem):
    return pltpu.CompilerParams(dimension_semantics=sem, vmem_limit_bytes=VMEM_LIMIT)


def _dot(a, b):
    return jnp.dot(a, b, preferred_element_type=F32)


def _dot_nt(a, b):
    return lax.dot_general(a, b, (((1,), (1,)), ((), ())), preferred_element_type=F32)


def _split_bf16(x):
    hi = x.astype(BF16)
    lo = (x - hi.astype(F32)).astype(BF16)
    return hi, lo


def _inproj_kernel(x_ref, nw_ref, wa_ref, wb_ref, pa_ref, pb_ref):
    x = x_ref[...]
    h = x * lax.rsqrt(jnp.mean(x * x, axis=-1, keepdims=True) + NORM_EPS) * nw_ref[...]
    h = h.astype(BF16)
    for c in range(0, PA_W, 512):
        w = min(512, PA_W - c)
        pa_ref[:, c:c + w] = _dot(h, wa_ref[:, c:c + w])
    for c in range(0, PB_W, 512):
        w = min(512, PB_W - c)
        pb_ref[:, c:c + w] = _dot(h, wb_ref[:, c:c + w]).astype(BF16)


def _inproj(x2, nw, wa, wb):
    n = x2.shape[0]
    return pl.pallas_call(
        _inproj_kernel,
        out_shape=(jax.ShapeDtypeStruct((n, PA_W), F32), jax.ShapeDtypeStruct((n, PB_W), BF16)),
        grid=(n // TM_PROJ,),
        in_specs=[
            pl.BlockSpec((TM_PROJ, D_MODEL), lambda i: (i, 0)),
            pl.BlockSpec((1, D_MODEL), lambda i: (0, 0)),
            pl.BlockSpec((D_MODEL, PA_W), lambda i: (0, 0)),
            pl.BlockSpec((D_MODEL, PB_W), lambda i: (0, 0)),
        ],
        out_specs=(pl.BlockSpec((TM_PROJ, PA_W), lambda i: (i, 0)),
                   pl.BlockSpec((TM_PROJ, PB_W), lambda i: (i, 0))),
        compiler_params=_cparams(("parallel",)),
        name="inproj",
    )(x2, nw, wa, wb)


def _outproj_kernel(x_ref, ret_ref, att_ref, mg_ref, wr_ref, wn_ref, wo_ref, o_ref):
    mg = mg_ref[...]
    a = jax.nn.sigmoid(mg[:, :D_MODEL]) * _dot(ret_ref[...], wr_ref[...])
    b = jax.nn.sigmoid(mg[:, D_MODEL:]) * _dot(att_ref[...], wn_ref[...])
    o_ref[...] = x_ref[...] + _dot((a + b).astype(BF16), wo_ref[...])


def _outproj(x2, ret, att, pa, wr, wn, wo):
    n = x2.shape[0]
    row = lambda i: (i, 0)
    full = lambda i: (0, 0)
    return pl.pallas_call(
        _outproj_kernel,
        out_shape=jax.ShapeDtypeStruct((n, D_MODEL), F32),
        grid=(n // TM_PROJ,),
        in_specs=[
            pl.BlockSpec((TM_PROJ, D_MODEL), row),
            pl.BlockSpec((TM_PROJ, RET_V), row),
            pl.BlockSpec((TM_PROJ, NSA_Q), row),
            pl.BlockSpec((TM_PROJ, 2 * D_MODEL), lambda i: (i, PA_MG // (2 * D_MODEL))),
            pl.BlockSpec((RET_V, D_MODEL), full),
            pl.BlockSpec((NSA_Q, D_MODEL), full),
            pl.BlockSpec((D_MODEL, D_MODEL), full),
        ],
        out_specs=pl.BlockSpec((TM_PROJ, D_MODEL), row),
        compiler_params=_cparams(("parallel",)),
        name="outproj",
    )(x2, ret, att, pa, wr, wn, wo)


def _ret_consts():
    h = np.arange(RET_HEADS, dtype=np.float32)
    log_g = np.log1p(-np.exp2(-5.0 - h)).astype(np.float32)
    n = np.arange(RET_CHUNK, dtype=np.float32)
    diff = n[:, None] - n[None, :]
    decay_in = np.where(diff >= 0, np.exp(log_g[:, None, None] * np.maximum(diff, 0.0)), 0.0).astype(np.float32)
    zeta = np.exp(log_g[:, None] * (RET_CHUNK - 1 - n)[None, :]).astype(np.float32)
    xi = np.exp(log_g[:, None] * (n + 1)[None, :]).astype(np.float32)
    chunk_decay = np.exp(log_g * RET_CHUNK).astype(np.float32)
    zeta_b = np.broadcast_to(zeta[:, :, None], (RET_HEADS, RET_CHUNK, RET_DK)).copy()
    xi_b = np.broadcast_to(xi[:, :, None], (RET_HEADS, RET_CHUNK, RET_DV)).copy()
    return decay_in, zeta_b, xi_b, [float(v) for v in chunk_decay]


def _rope_half_tables(pos, rot_dim, theta, period):
    inv = theta ** (-jnp.arange(0, rot_dim, 2, dtype=F32) / rot_dim)
    ang = pos.astype(F32)[:, None] * inv[None, :]
    c, s = jnp.cos(ang), jnp.sin(ang)
    half = rot_dim // 2
    p = pos.shape[0]
    one = jnp.ones((p, period - rot_dim), F32)
    zero_h = jnp.zeros((p, half), F32)
    zero_r = jnp.zeros((p, period - rot_dim), F32)
    cos_t = jnp.concatenate([c, c, one], axis=1)
    sm = jnp.concatenate([-s, zero_h, zero_r], axis=1)
    sp = jnp.concatenate([zero_h, s, zero_r], axis=1)
    rep = LANES // period
    return tuple(jnp.tile(t, (1, rep)) for t in (cos_t, sm, sp))


def _ret_kernel(q_ref, k_ref, v_ref, g_ref, cos_ref, sin_ref, dec_ref, zeta_ref, xi_ref, gnw_ref,
                o_ref, state_ref, *, chunk_decay):
    @pl.when(pl.program_id(1) == 0)
    def _():
        state_ref[...] = jnp.zeros_like(state_ref)

    n_chunks = TC_RET // RET_CHUNK
    for h in range(RET_HEADS):
        dec = dec_ref[h]
        zeta = zeta_ref[h]
        xi = xi_ref[h]
        gnw = gnw_ref[:, h * RET_DV:(h + 1) * RET_DV]
        for c in range(n_chunks):
            rows = slice(c * RET_CHUNK, (c + 1) * RET_CHUNK)
            cs = cos_ref[rows, :]
            sn = sin_ref[rows, :]
            q = q_ref[rows, h * RET_DK:(h + 1) * RET_DK].astype(F32)
            k = k_ref[rows, h * RET_DK:(h + 1) * RET_DK].astype(F32)
            q = q * cs + pltpu.roll(q, RET_DK // 2, 1) * sn
            k = (k * cs + pltpu.roll(k, RET_DK // 2, 1) * sn) * (RET_DK ** -0.5)
            v = v_ref[rows, h * RET_DV:(h + 1) * RET_DV]
            qb = q.astype(BF16)
            s = _dot_nt(qb, k.astype(BF16)) * dec
            inner = _dot(s.astype(BF16), v)
            r = state_ref[h]
            cross = _dot(qb, r.astype(BF16)) * xi
            kzt = (k * zeta).T.astype(BF16)
            state_ref[h] = chunk_decay[h] * r + _dot(kzt, v)
            o = inner + cross
            mu = jnp.mean(o, axis=-1, keepdims=True)
            d = o - mu
            var = jnp.mean(d * d, axis=-1, keepdims=True)
            on = d * lax.rsqrt(var + NORM_EPS) * gnw
            g = g_ref[rows, h * RET_DV:(h + 1) * RET_DV]
            o_ref[rows, h * RET_DV:(h + 1) * RET_DV] = (g * jax.nn.sigmoid(g) * on).astype(BF16)


def _retention(pa, pb, gnw, tabs, b, s):
    cos_t, sin_t, dec, zeta_b, xi_b, chunk_decay = tabs
    nt = s // TC_RET
    kern = functools.partial(_ret_kernel, chunk_decay=chunk_decay)
    tok = lambda w, j: pl.BlockSpec((TC_RET, w), lambda bi, si, j=j: (bi * nt + si, j))
    cst3 = lambda shp: pl.BlockSpec(shp, lambda bi, si: (0, 0, 0))
    return pl.pallas_call(
        kern,
        out_shape=jax.ShapeDtypeStruct((b * s, RET_V), BF16),
        grid=(b, nt),
        in_specs=[
            tok(RET_QK, PB_RQ // RET_QK),
            tok(RET_QK, PB_RK // RET_QK),
            tok(RET_V, PB_RV // RET_V),
            tok(RET_V, PA_RG // RET_V),
            pl.BlockSpec((TC_RET, LANES), lambda bi, si: (si, 0)),
            pl.BlockSpec((TC_RET, LANES), lambda bi, si: (si, 0)),
            cst3((RET_HEADS, RET_CHUNK, RET_CHUNK)),
            cst3((RET_HEADS, RET_CHUNK, RET_DK)),
            cst3((RET_HEADS, RET_CHUNK, RET_DV)),
            pl.BlockSpec((1, RET_V), lambda bi, si: (0, 0)),
        ],
        out_specs=pl.BlockSpec((TC_RET, RET_V), lambda bi, si: (bi * nt + si, 0)),
        scratch_shapes=[pltpu.VMEM((RET_HEADS, RET_DK, RET_DV), F32)],
        compiler_params=_cparams(("parallel", "arbitrary")),
        name="retention",
    )(pb, pb, pb, pa, cos_t, sin_t, dec, zeta_b, xi_b, gnw)


def _ones_row_pad(n):
    pad_rows = 16
    return jnp.where(lax.broadcasted_iota(I32, (pad_rows, n), 0) == 0, 1.0, 0.0)


def _rope16(x, cs, sm, sp):
    half = ROPE_DIM // 2
    return x * cs + pltpu.roll(x, LANES - half, 1) * sm + pltpu.roll(x, half, 1) * sp


def _prep_kernel(nq_ref, ks_ref, kw_ref, vs_ref, vw_ref, ng_ref, cs_ref, sm_ref, sp_ref,
                 qw_ref, ksw_ref, kww_ref, bd_ref,
                 q_ref, ksa_ref, kwo_ref, vst_ref, vwt_ref, gt_ref):
    cs, sm, sp = cs_ref[...], sm_ref[...], sp_ref[...]
    bd = bd_ref[...]
    for p in range(NSA_Q // LANES):
        x = nq_ref[:, p * LANES:(p + 1) * LANES]
        hi, lo = _split_bf16(x * x)
        ms = (_dot(hi, bd) + _dot(lo, bd)) * (1.0 / HEAD_DIM)
        y = x * lax.rsqrt(ms + NORM_EPS) * qw_ref[...]
        y = _rope16(y, cs, sm, sp) * (HEAD_DIM ** -0.5 * LOG2E)
        q_ref[:, p * LANES:(p + 1) * LANES] = y.astype(BF16)
    tok = pl.program_id(1) * TM_PREP + lax.broadcasted_iota(I32, (TM_PREP, LANES), 0)
    blk = lax.shift_right_logical(tok, int(np.log2(SLC_BLOCK)))
    onehot = jnp.where(lax.broadcasted_iota(I32, (TM_PREP, LANES), 1) == blk, 1.0, 0.0).astype(BF16)
    for g in range(NSA_GROUPS):
        sl = slice(g * LANES, (g + 1) * LANES)
        x = ks_ref[:, sl].astype(F32)
        y = x * lax.rsqrt(jnp.mean(x * x, axis=-1, keepdims=True) + NORM_EPS) * ksw_ref[...]
        ksa_ref[:, 2 * g * LANES:(2 * g + 1) * LANES] = _rope16(y, cs, sm, sp).astype(BF16)
        ksa_ref[:, (2 * g + 1) * LANES:(2 * g + 2) * LANES] = onehot
        x = kw_ref[:, sl].astype(F32)
        y = x * lax.rsqrt(jnp.mean(x * x, axis=-1, keepdims=True) + NORM_EPS) * kww_ref[...]
        kwo_ref[:, sl] = _rope16(y, cs, sm, sp).astype(BF16)
    vt = vs_ref[...].astype(F32).T
    wt = vw_ref[...].astype(F32).T
    for g in range(NSA_GROUPS):
        rows = slice(g * HEAD_DIM, (g + 1) * HEAD_DIM)
        vst_ref[g, 0:HEAD_DIM, :] = vt[rows, :].astype(BF16)
        vst_ref[g, HEAD_DIM:VROWS, :] = _ones_row_pad(TM_PREP).astype(BF16)
        for r in range(TM_PREP // LANES):
            vwt_ref[g, r, 0:HEAD_DIM, :] = wt[rows, r * LANES:(r + 1) * LANES].astype(BF16)
            vwt_ref[g, r, HEAD_DIM:VROWS, :] = _ones_row_pad(LANES).astype(BF16)
    gt_ref[...] = jax.nn.sigmoid(ng_ref[...]).T


def _nsa_prep(pa, pb, tabs, qw, ksw, kww, bd, b, s):
    cs, sm, sp = tabs
    nt = s // TM_PREP
    tokb = lambda w, j: pl.BlockSpec((TM_PREP, w), lambda bi, si, j=j: (bi * nt + si, j))
    tab = pl.BlockSpec((TM_PREP, LANES), lambda bi, si: (si, 0))
    vec = pl.BlockSpec((1, LANES), lambda bi, si: (0, 0))
    kv_w = NSA_GROUPS * LANES
    out_shape = (
        jax.ShapeDtypeStruct((b, s, NSA_Q), BF16),
        jax.ShapeDtypeStruct((b, s, 2 * kv_w), BF16),
        jax.ShapeDtypeStruct((b, s, kv_w), BF16),
        jax.ShapeDtypeStruct((b, NSA_GROUPS, nt, VROWS, TM_PREP), BF16),
        jax.ShapeDtypeStruct((b, NSA_GROUPS, s // LANES, VROWS, LANES), BF16),
        jax.ShapeDtypeStruct((b, LANES, s), F32),
    )
    out_specs = (
        pl.BlockSpec((None, TM_PREP, NSA_Q), lambda bi, si: (bi, si, 0)),
        pl.BlockSpec((None, TM_PREP, 2 * kv_w), lambda bi, si: (bi, si, 0)),
        pl.BlockSpec((None, TM_PREP, kv_w), lambda bi, si: (bi, si, 0)),
        pl.BlockSpec((None, NSA_GROUPS, None, VROWS, TM_PREP), lambda bi, si: (bi, 0, si, 0, 0)),
        pl.BlockSpec((None, NSA_GROUPS, TM_PREP // LANES, VROWS, LANES), lambda bi, si: (bi, 0, si, 0, 0)),
        pl.BlockSpec((None, LANES, TM_PREP), lambda bi, si: (bi, 0, si)),
    )
    return pl.pallas_call(
        _prep_kernel,
        out_shape=out_shape,
        grid=(b, nt),
        in_specs=[
            tokb(NSA_Q, PA_NQ // NSA_Q),
            tokb(kv_w, PB_KS // kv_w), tokb(kv_w, PB_KW // kv_w),
            tokb(NSA_KV, PB_VS // NSA_KV), tokb(NSA_KV, PB_VW // NSA_KV),
            tokb(LANES, PA_NG // LANES),
            tab, tab, tab, vec, vec, vec,
            pl.BlockSpec((LANES, LANES), lambda bi, si: (0, 0)),
        ],
        out_specs=out_specs,
        compiler_params=_cparams(("parallel", "parallel")),
        name="nsa_prep",
    )(pa, pb, pb, pb, pb, pa, cs, sm, sp, qw, ksw, kww, bd)


def _compress_kernel(x_ref, w1p_ref, w1a_ref, w1b_ref, pe_ref, w2_ref, cs_ref, sm_ref, sp_ref, nw_ref, o_ref,
                     *, is_key, n):
    a = bb = None
    for j in range(CMP_STRIDE):
        xj = x_ref[pl.ds(j, n, stride=CMP_STRIDE), :].astype(BF16)
        da = _dot(xj, w1p_ref[j])
        db = _dot(xj, w1p_ref[CMP_STRIDE + j])
        a = da if a is None else a + da
        bb = db if bb is None else bb + db
    pe_hi, pe_lo = _split_bf16(pe_ref[...])
    w1 = jnp.concatenate([w1a_ref[...], w1b_ref[...]], axis=0)
    pe_term = (_dot(pe_hi, w1) + _dot(pe_lo, w1))[0:1, :]
    hid = a + pltpu.roll(bb, n - 1, 0) + pe_term
    t = hid * (0.7978845608028654 * (1.0 + 0.044715 * hid * hid))
    act = 0.5 * hid * (1.0 + jnp.tanh(t))
    y = _dot(act.astype(BF16), w2_ref[...])
    if is_key:
        y = y * lax.rsqrt(jnp.mean(y * y, axis=-1, keepdims=True) + NORM_EPS) * nw_ref[...]
        o_ref[...] = _rope16(y, cs_ref[...], sm_ref[...], sp_ref[...]).astype(BF16)
    else:
        o_ref[...] = jnp.concatenate([y.T[0:HEAD_DIM, :], _ones_row_pad(n)], axis=0).astype(BF16)


def _compress(pa, w1p, w1a, w1b, pe, w2, tabs, nw, is_key, b, s):
    g, n = NSA_GROUPS, s // CMP_STRIDE
    cs, sm, sp = tabs
    full2 = lambda shp: pl.BlockSpec(shp, lambda bi, gi: (0, 0))
    out_block = (None, None, n, LANES) if is_key else (None, None, VROWS, n)
    out_shape = (b, g, n, LANES) if is_key else (b, g, VROWS, n)
    return pl.pallas_call(
        functools.partial(_compress_kernel, is_key=is_key, n=n),
        out_shape=jax.ShapeDtypeStruct(out_shape, BF16),
        grid=(b, g),
        in_specs=[
            pl.BlockSpec((s, LANES), lambda bi, gi: (bi, PA_C // LANES + (0 if is_key else 1))),
            pl.BlockSpec((None, CMP_LEN, LANES, CMP_HIDDEN), lambda bi, gi: (gi, 0, 0, 0)),
            full2(w1a.shape), full2(w1b.shape), full2(pe.shape), full2(w2.shape),
            full2(cs.shape), full2(sm.shape), full2(sp.shape), full2(nw.shape),
        ],
        out_specs=pl.BlockSpec(out_block, lambda bi, gi: (bi, gi, 0, 0)),
        compiler_params=_cparams(("parallel", "parallel")),
        name="compress_k" if is_key else "compress_v",
    )(pa, w1p, w1a, w1b, pe, w2, cs, sm, sp, nw)


def _weighted_values(vt, e):
    acc = _dot(vt, e.astype(BF16))
    return acc[0:HEAD_DIM, :], acc[HEAD_DIM:HEAD_DIM + 1, :]


def _attn_kernel(q_ref, kc_ref, vct_ref, ksa_ref, vst_ref, kw_ref, vwt_ref, gt_ref, ovt_ref,
                 o_ref, s_ref, cm_ref, qa_ref):
    gi = pl.program_id(1)
    q0 = pl.program_id(2) * TQ
    n_cmp = kc_ref.shape[0]
    ncols = GROUP_HEADS * TQ

    low = lax.broadcasted_iota(I32, (TQ, LANES), 1) < HEAD_DIM
    zero = jnp.zeros((TQ, LANES), BF16)
    parts = []
    for h in range(GROUP_HEADS):
        slab = q_ref[:, (h // 2) * LANES:(h // 2 + 1) * LANES]
        parts.append(jnp.where(low, slab, zero) if h % 2 == 0 else jnp.where(low, zero, slab))
    qs = jnp.concatenate(parts, axis=0)
    t_row = q0 + lax.broadcasted_iota(I32, (1, TQ), 1)

    def all_heads(bias):
        return jnp.concatenate([bias] * GROUP_HEADS, axis=1)

    def produce(keys, queries, slot, keep_max):
        st = _dot_nt(keys, queries)
        s_ref[slot, 0:keys.shape[0], :] = st
        if keep_max:
            cm_ref[slot] = jnp.broadcast_to(jnp.max(st, axis=0, keepdims=True), (8, ncols))

    def consume(slot, nk, vt, carry, bias):
        m_old, a_old = carry
        st = s_ref[slot, 0:nk, :]
        if bias is None:
            cm = cm_ref[slot][0:1, :]
        else:
            st = st + all_heads(bias)
            cm = jnp.max(st, axis=0, keepdims=True)
        m_new = jnp.maximum(m_old, cm)
        alpha = jnp.exp2(m_old - m_new)
        e = jnp.exp2(st - m_new)
        return m_new, alpha * a_old + _dot(vt, e.astype(BF16))

    init = (jnp.full((1, ncols), NEG, F32), jnp.zeros((VROWS, ncols), F32))

    def finish(acc):
        return acc[0:HEAD_DIM, :] * (1.0 / acc[HEAD_DIM:HEAD_DIM + 1, :])

    w0 = jnp.maximum(q0 - WINDOW, 0)
    wt0 = lax.shift_right_logical(w0, int(np.log2(LANES)))
    n_wb = WIN_TILES * LANES - TK
    produce(kc_ref[...], qs, 1, False)
    produce(kw_ref[pl.ds(pl.multiple_of(w0, LANES), TK), :], qs, 0, False)

    cmp_end = lax.broadcasted_iota(I32, (n_cmp, 1), 0) * CMP_STRIDE + (CMP_LEN - 1)
    st = s_ref[1, 0:n_cmp, :] + all_heads(jnp.where(cmp_end <= t_row, 0.0, NEG))
    m = jnp.max(st, axis=0, keepdims=True)
    e = jnp.exp2(st - jnp.where(m > 0.5 * NEG, m, 0.0))
    oc_all, l = _weighted_values(vct_ref[...], e)
    rl = 1.0 / jnp.where(l > 0.0, l, 1.0)
    oc_all = oc_all * rl
    psum = e[:, 0:TQ] * rl[:, 0:TQ]
    for h in range(1, GROUP_HEADS):
        psum = psum + e[:, h * TQ:(h + 1) * TQ] * rl[:, h * TQ:(h + 1) * TQ]
    ph, plo = _split_bf16(psum)
    imp_t = _dot(ovt_ref[...], ph) + _dot(ovt_ref[...], plo)

    def wbias(start, nk):
        kpos = start + lax.broadcasted_iota(I32, (nk, 1), 0)
        return jnp.where((kpos <= t_row) & (kpos > t_row - WINDOW), 0.0, NEG)

    produce(kw_ref[pl.ds(pl.multiple_of(w0 + TK, LANES), n_wb), :], qs, 1, False)
    vwa = jnp.concatenate([vwt_ref[wt0 + r] for r in range(TK // LANES)], axis=1)
    carry_w = consume(0, TK, vwa, init, wbias(w0, TK))
    produce(ksa_ref[0:TK, 0:LANES], qs, 0, False)

    jrow = lax.broadcasted_iota(I32, (LANES, TQ), 0)
    cur = lax.shift_right_logical(q0 + lax.broadcasted_iota(I32, (LANES, TQ), 1), int(np.log2(SLC_BLOCK)))
    forced = (jrow == 0) | (jrow == cur) | (jrow == cur - 1)
    valid = jrow <= cur
    bias = jnp.where(forced & valid, 0.0, NEG)
    val = jnp.where(valid & jnp.logical_not(forced), imp_t, -jnp.inf)
    for _ in range(SLC_TOPK - 3):
        mx = jnp.max(val, axis=0, keepdims=True)
        first = jnp.min(jnp.where(val == mx, jrow, LANES), axis=0, keepdims=True)
        pick = jrow == first
        bias = jnp.where(pick & valid, 0.0, bias)
        val = jnp.where(pick, -jnp.inf, val)

    vwb = jnp.concatenate([vwt_ref[wt0 + TK // LANES + r] for r in range(n_wb // LANES)], axis=1)
    _, a_w = consume(1, n_wb, vwb, carry_w, wbias(w0 + TK, n_wb))
    ow_all = finish(a_w)

    selb = bias.T.astype(BF16)
    qa_ref[...] = jnp.concatenate([qs, jnp.concatenate([selb] * GROUP_HEADS, axis=0)], axis=1)

    def sel_scores(ti, slot):
        produce(ksa_ref[pl.ds(pl.multiple_of(ti * TK, TK), TK), :], qa_ref[...], slot, True)

    def sel_consume(ti, slot, carry, causal):
        cb = None
        if causal:
            cb = jnp.where(ti * TK + lax.broadcasted_iota(I32, (TK, 1), 0) <= t_row, 0.0, NEG)
        return consume(slot, TK, vst_ref[ti], carry, cb)

    n_full = lax.shift_right_logical(q0, int(np.log2(TK)))
    rows_per_tile = TK // SLC_BLOCK
    bias0 = jnp.concatenate([jnp.broadcast_to(bias[r:r + 1, :], (SLC_BLOCK, TQ)) for r in range(rows_per_tile)],
                            axis=0)
    st0 = s_ref[0] + all_heads(bias0)
    s_ref[0] = st0
    cm_ref[0] = jnp.broadcast_to(jnp.max(st0, axis=0, keepdims=True), (8, ncols))

    def two_tiles(j, carry):
        sel_scores(2 * j + 1, 1)
        carry = sel_consume(2 * j, 0, carry, False)
        sel_scores(2 * j + 2, 0)
        return sel_consume(2 * j + 1, 1, carry, False)

    n_pairs = lax.shift_right_logical(n_full, 1)
    n_quads = lax.shift_right_logical(n_pairs, 1)
    n_octs = lax.shift_right_logical(n_quads, 1)
    four_tiles = lambda j, c: two_tiles(2 * j + 1, two_tiles(2 * j, c))
    carry = lax.fori_loop(0, n_octs, lambda j, c: four_tiles(2 * j + 1, four_tiles(2 * j, c)), init)
    carry = lax.fori_loop(2 * n_octs, n_quads, four_tiles, carry)
    carry = lax.fori_loop(2 * n_quads, n_pairs, two_tiles, carry)

    def odd_tail(c):
        sel_scores(n_full, 1)
        return sel_consume(n_full, 1, sel_consume(n_full - 1, 0, c, False), True)

    _, a_s = lax.cond((n_full & 1) == 1, odd_tail, lambda c: sel_consume(n_full, 0, c, True), carry)
    os_all = finish(a_s)

    for p in range(GROUP_HEADS // 2):
        halves = []
        for h in (2 * p, 2 * p + 1):
            c = slice(h * TQ, (h + 1) * TQ)
            gbase = (gi * GROUP_HEADS + h) * 3
            halves.append(gt_ref[pl.ds(gbase, 1), :] * oc_all[:, c]
                          + gt_ref[pl.ds(gbase + 1, 1), :] * os_all[:, c]
                          + gt_ref[pl.ds(gbase + 2, 1), :] * ow_all[:, c])
        o_ref[:, p * LANES:(p + 1) * LANES] = jnp.concatenate(halves, axis=0).T.astype(BF16)


def _attention(q, kc, vct, ksa, vst, kw, vwt, gt, ovt, b, s):
    nq = s // TQ
    n_cmp = kc.shape[2]
    gw = GROUP_HEADS * HEAD_DIM
    ncols = GROUP_HEADS * TQ
    return pl.pallas_call(
        _attn_kernel,
        out_shape=jax.ShapeDtypeStruct((b, s, NSA_Q), BF16),
        grid=(b, NSA_GROUPS, nq),
        in_specs=[
            pl.BlockSpec((None, TQ, gw), lambda bi, gi, qi: (bi, qi, gi)),
            pl.BlockSpec((None, None, n_cmp, LANES), lambda bi, gi, qi: (bi, gi, 0, 0)),
            pl.BlockSpec((None, None, VROWS, n_cmp), lambda bi, gi, qi: (bi, gi, 0, 0)),
            pl.BlockSpec((None, s, 2 * LANES), lambda bi, gi, qi: (bi, 0, gi)),
            pl.BlockSpec((None, None, s // TK, VROWS, TK), lambda bi, gi, qi: (bi, gi, 0, 0, 0)),
            pl.BlockSpec((None, s, LANES), lambda bi, gi, qi: (bi, 0, gi)),
            pl.BlockSpec((None, None, s // LANES, VROWS, LANES), lambda bi, gi, qi: (bi, gi, 0, 0, 0)),
            pl.BlockSpec((None, LANES, TQ), lambda bi, gi, qi: (bi, 0, qi)),
            pl.BlockSpec((LANES, n_cmp), lambda bi, gi, qi: (0, 0)),
        ],
        out_specs=pl.BlockSpec((None, TQ, gw), lambda bi, gi, qi: (bi, qi, gi)),
        scratch_shapes=[
            pltpu.VMEM((2, TK, ncols), F32),
            pltpu.VMEM((2, 8, ncols), F32),
            pltpu.VMEM((ncols, 2 * LANES), BF16),
        ],
        compiler_params=_cparams(("parallel", "parallel", "arbitrary")),
        name="nsa_attention",
    )(q, kc, vct, ksa, vst, kw, vwt, gt, ovt)


def _router_kernel(x_ref, nw_ref, wt_ref, b_ref, tri_ref, h_ref, e_ref, w_ref, r_ref, cnt_ref, base_ref):
    @pl.when(pl.program_id(0) == 0)
    def _():
        base_ref[...] = jnp.zeros_like(base_ref)

    x = x_ref[...]
    h = x * lax.rsqrt(jnp.mean(x * x, axis=-1, keepdims=True) + NORM_EPS) * nw_ref[...]
    _store_row_tiles(h_ref, h)
    h_hi, h_lo = _split_bf16(h)
    w_hi, w_lo = _split_bf16(wt_ref[...])
    lg = _dot_nt(w_hi, h_hi) + _dot_nt(w_hi, h_lo) + _dot_nt(w_lo, h_hi) + b_ref[...]
    erow = lax.broadcasted_iota(I32, lg.shape, 0)
    vals, hots = [], []
    for _ in range(TOP_K):
        mx = jnp.max(lg, axis=0, keepdims=True)
        first = jnp.min(jnp.where(lg == mx, erow, N_EXPERTS), axis=0, keepdims=True)
        hot = erow == first
        vals.append(mx)
        hots.append(hot)
        lg = jnp.where(hot, -jnp.inf, lg)
    ex = [jnp.exp(v - vals[0]) for v in vals]
    den = ex[0] + ex[1] + ex[2] + ex[3]
    onehots = [h.astype(F32) for h in hots]
    cnt = onehots[0] + onehots[1] + onehots[2] + onehots[3]
    pref = _dot(cnt.astype(BF16), tri_ref[...]) + base_ref[:, 0:1]
    erow_f = erow.astype(F32)
    idxs = [jnp.sum(oh * erow_f, axis=0, keepdims=True) for oh in onehots]
    ranks = [jnp.sum(oh * pref, axis=0, keepdims=True) for oh in onehots]
    base_ref[...] = base_ref[...] + jnp.sum(cnt, axis=1, keepdims=True)
    cnt_ref[...] = base_ref[...]
    pad_f = [jnp.zeros_like(den)] * (8 - TOP_K)
    e_ref[...] = jnp.concatenate(idxs + pad_f, axis=0).astype(I32)
    r_ref[...] = jnp.concatenate(ranks + pad_f, axis=0).astype(I32)
    w_ref[...] = jnp.concatenate([e / den for e in ex] + pad_f, axis=0)


def _router(x2, nw, wt, bcol, tri):
    n = x2.shape[0]
    small = pl.BlockSpec((8, TM_PROJ), lambda i: (0, i))
    return pl.pallas_call(
        _router_kernel,
        out_shape=(jax.ShapeDtypeStruct(_tiled(n), F32),
                   jax.ShapeDtypeStruct((8, n), I32),
                   jax.ShapeDtypeStruct((8, n), F32),
                   jax.ShapeDtypeStruct((8, n), I32),
                   jax.ShapeDtypeStruct((N_EXPERTS, LANES), F32)),
        grid=(n // TM_PROJ,),
        in_specs=[
            pl.BlockSpec((TM_PROJ, D_MODEL), lambda i: (i, 0)),
            pl.BlockSpec((1, D_MODEL), lambda i: (0, 0)),
            pl.BlockSpec((N_EXPERTS, D_MODEL), lambda i: (0, 0)),
            pl.BlockSpec((N_EXPERTS, 1), lambda i: (0, 0)),
            pl.BlockSpec((TM_PROJ, TM_PROJ), lambda i: (0, 0)),
        ],
        out_specs=(pl.BlockSpec(_tiled(TM_PROJ), lambda i: (i, 0)), small, small, small,
                   pl.BlockSpec((N_EXPERTS, LANES), lambda i: (0, 0))),
        scratch_shapes=[pltpu.VMEM((N_EXPERTS, LANES), F32)],
        compiler_params=_cparams(("arbitrary",)),
        name="router",
    )(x2, nw, wt, bcol, tri)


def _row_copy(src, src_row, dst, dst_row, sem):
    tile = lambda ref, r: ref.at[pl.ds(pl.multiple_of(r * SUB, SUB), SUB), :]
    return pltpu.make_async_copy(tile(src, src_row), tile(dst, dst_row), sem)


def _issue_rows(nrow, copy_of):
    def body(g, c):
        for j in range(8):
            copy_of(g * 8 + j).start(priority=j % 2)
        return c

    lax.fori_loop(0, nrow // 8, body, 0)


def _wait_rows(nrow, copy_of):
    def body(g, c):
        for j in range(8):
            copy_of(0).wait()
        return c

    lax.fori_loop(0, nrow // 8, body, 0)


def _store_row_tiles(ref, x):
    for c in range(SUB):
        ref[pl.ds(c, x.shape[0], stride=SUB), :] = x[:, c * LANES:(c + 1) * LANES]


def _load_row_tiles(ref, first_row, nrows):
    return jnp.concatenate([ref[pl.ds(first_row * SUB + c, nrows, stride=SUB), :] for c in range(SUB)], axis=1)


def _dispatch_kernel(zinfo_ref, dest_ref, h_ref, x_hbm, zbuf, sem, zsem, *, n_blocks):
    def zero_block(row):
        return pltpu.make_async_copy(
            zbuf, x_hbm.at[pl.ds(pl.multiple_of(row * SUB, TM_MOE * SUB), TM_MOE * SUB), :], zsem)

    @pl.when(pl.program_id(0) == 0)
    def _():
        zbuf[...] = jnp.zeros_like(zbuf)
        for e in range(N_EXPERTS):
            zero_block(zinfo_ref[e]).start()
        for e in range(N_EXPERTS):
            zero_block(0).wait()
        nvalid = zinfo_ref[N_EXPERTS]

        def ztail(bk, c):
            zero_block(bk * TM_MOE).start()
            return c

        lax.fori_loop(nvalid, n_blocks, ztail, 0)

        def zwait(bk, c):
            zero_block(0).wait()
            return c

        lax.fori_loop(nvalid, n_blocks, zwait, 0)

    nrow = TOP_K * TM_CMB
    _issue_rows(nrow, lambda r: _row_copy(h_ref, r & (TM_CMB - 1), x_hbm, dest_ref[0, 0, r], sem))
    _wait_rows(nrow, lambda r: _row_copy(h_ref, 0, x_hbm, 0, sem))


def _dispatch(zinfo, dest_b, h3, rows):
    n = h3.shape[0] // SUB
    grid_spec = pltpu.PrefetchScalarGridSpec(
        num_scalar_prefetch=1,
        grid=(n // TM_CMB,),
        in_specs=[
            pl.BlockSpec((1, 1, TOP_K * TM_CMB), lambda i, z: (i, 0, 0), memory_space=pltpu.SMEM),
            pl.BlockSpec(_tiled(TM_CMB), lambda i, z: (i, 0)),
        ],
        out_specs=pl.BlockSpec(memory_space=pl.ANY),
        scratch_shapes=[pltpu.VMEM(_tiled(TM_MOE), F32), pltpu.SemaphoreType.DMA(()),
                        pltpu.SemaphoreType.DMA(())],
    )
    return pl.pallas_call(
        functools.partial(_dispatch_kernel, n_blocks=rows // TM_MOE),
        out_shape=jax.ShapeDtypeStruct(_tiled(rows), F32),
        grid_spec=grid_spec,
        compiler_params=_cparams(("arbitrary",)),
        name="dispatch",
    )(zinfo, dest_b, h3)


def _expert_kernel(blk_e_ref, nvalid_ref, x_ref, wgu_ref, bgu_ref, wd_ref, bd_ref, y_ref, wgu_bf, wd_bf):
    i = pl.program_id(0)
    used = i < nvalid_ref[0]

    @pl.when(jnp.logical_not(used))
    def _():
        y_ref[...] = jnp.zeros_like(y_ref)

    @pl.when(used & ((i == 0) | (blk_e_ref[i] != blk_e_ref[jnp.maximum(i - 1, 0)])))
    def _():
        wgu_bf[...] = wgu_ref[...].astype(BF16)
        wd_bf[...] = wd_ref[...].astype(BF16)

    @pl.when(used)
    def _():
        gu = _dot(_load_row_tiles(x_ref, 0, TM_MOE).astype(BF16), wgu_bf[...]) + bgu_ref[...]
        gate = jnp.minimum(gu[:, :D_FF], SWIGLU_LIMIT)
        up = jnp.clip(gu[:, D_FF:], -SWIGLU_LIMIT, SWIGLU_LIMIT)
        glu = gate * jax.nn.sigmoid(gate * SWIGLU_ALPHA)
        _store_row_tiles(y_ref, _dot(((up + 1.0) * glu).astype(BF16), wd_bf[...]) + bd_ref[...])


def _experts(blk_e, nvalid, xbuf, wgu, bgu, wd, bd, layer):
    n_blocks = blk_e.shape[0]
    wblk = lambda i, be, nv: (be[i], 0, 0)
    wblk4 = lambda i, be, nv: (layer, be[i], 0, 0)
    grid_spec = pltpu.PrefetchScalarGridSpec(
        num_scalar_prefetch=2,
        grid=(n_blocks,),
        in_specs=[
            pl.BlockSpec(_tiled(TM_MOE), lambda i, be, nv: (jnp.minimum(i, nv[0] - 1), 0)),
            pl.BlockSpec((None, None, D_MODEL, 2 * D_FF), wblk4),
            pl.BlockSpec((None, 1, 2 * D_FF), wblk),
            pl.BlockSpec((None, None, D_FF, D_MODEL), wblk4),
            pl.BlockSpec((None, 1, D_MODEL), wblk),
        ],
        out_specs=pl.BlockSpec(_tiled(TM_MOE), lambda i, be, nv: (i, 0)),
        scratch_shapes=[pltpu.VMEM((D_MODEL, 2 * D_FF), BF16), pltpu.VMEM((D_FF, D_MODEL), BF16)],
    )
    return pl.pallas_call(
        _expert_kernel,
        out_shape=jax.ShapeDtypeStruct(_tiled(n_blocks * TM_MOE), F32),
        grid_spec=grid_spec,
        compiler_params=_cparams(("arbitrary",)),
        name="experts",
    )(blk_e, nvalid, xbuf, wgu, bgu, wd, bd)


def _combine_kernel(dcur_ref, dnext_ref, x_ref, w_ref, y_hbm, o_ref, ybuf, sem):
    i = pl.program_id(0)
    nrow = TOP_K * TM_CMB
    slot = i & 1

    def issue(d_ref, s):
        _issue_rows(nrow, lambda r: _row_copy(y_hbm, d_ref[0, 0, r], ybuf.at[s], r, sem.at[s]))

    @pl.when(i == 0)
    def _():
        issue(dcur_ref, 0)

    @pl.when(i + 1 < pl.num_programs(0))
    def _():
        issue(dnext_ref, 1 - slot)

    _wait_rows(nrow, lambda r: _row_copy(y_hbm, 0, ybuf.at[slot], 0, sem.at[slot]))
    acc = x_ref[...]
    w = w_ref[...]
    for k in range(TOP_K):
        acc = acc + w[:, k:k + 1] * _load_row_tiles(ybuf.at[slot], k * TM_CMB, TM_CMB)
    o_ref[...] = acc


def _combine(dest_b, x2, wts, ybuf):
    n = x2.shape[0]
    nt = n // TM_CMB
    return pl.pallas_call(
        _combine_kernel,
        out_shape=jax.ShapeDtypeStruct((n, D_MODEL), F32),
        grid=(nt,),
        in_specs=[
            pl.BlockSpec((1, 1, TOP_K * TM_CMB), lambda i: (i, 0, 0), memory_space=pltpu.SMEM),
            pl.BlockSpec((1, 1, TOP_K * TM_CMB), lambda i: (jnp.minimum(i + 1, nt - 1), 0, 0),
                         memory_space=pltpu.SMEM),
            pl.BlockSpec((TM_CMB, D_MODEL), lambda i: (i, 0)),
            pl.BlockSpec((TM_CMB, TOP_K), lambda i: (i, 0)),
            pl.BlockSpec(memory_space=pl.ANY),
        ],
        out_specs=pl.BlockSpec((TM_CMB, D_MODEL), lambda i: (i, 0)),
        scratch_shapes=[pltpu.VMEM((2,) + _tiled(TOP_K * TM_CMB), F32), pltpu.SemaphoreType.DMA((2,))],
        compiler_params=_cparams(("arbitrary",)),
        name="combine",
    )(dest_b, dest_b, x2, wts, ybuf)


def _route(e_t, r_t, cnt, n):
    counts = cnt[:, 0].astype(I32)
    padded = (counts + TM_MOE - 1) // TM_MOE * TM_MOE
    pend = jnp.cumsum(padded)
    pstart = pend - padded
    n_blocks = n * TOP_K // TM_MOE + N_EXPERTS
    rows = n_blocks * TM_MOE
    dest = r_t[:TOP_K]
    for e in range(N_EXPERTS):
        dest = dest + jnp.where(e_t[:TOP_K] == e, pstart[e].astype(I32), 0)
    nt = n // TM_CMB
    dest_b = dest.reshape(TOP_K, nt, TM_CMB).transpose(1, 0, 2).reshape(nt, 1, TOP_K * TM_CMB)
    blk_row = jnp.arange(n_blocks, dtype=I32)[:, None] * TM_MOE
    blk_e = jnp.minimum(jnp.sum((pend[None, :] <= blk_row).astype(I32), axis=1), N_EXPERTS - 1)
    nvalid = (pend[-1:] // TM_MOE).astype(I32)
    zrow = jnp.clip(pstart + padded - TM_MOE, 0, rows - TM_MOE).astype(I32)
    return dest_b, blk_e, nvalid, jnp.concatenate([zrow, nvalid]), rows


def _layer_weights(l, w_in, norm1_w, ret_gn_w, qk_norm_w, cmp_pos, cmp_w1, cmp_w2, w_o_ret, w_o_nsa, w_out,
                   norm2_w, router_w, router_b, w_gate_up, b_gate_up, w_down, b_down):
    w = w_in[l]
    o = np.cumsum((0, RET_QK, RET_QK, RET_V, RET_V, NSA_Q) + (NSA_KV,) * 6 + (NSA_GATE, 2 * D_MODEL))
    rq, rk, rv, rg, nq = (w[:, o[i]:o[i + 1]] for i in range(5))
    kc, vc, ksl, vsl, kwi, vwi = (w[:, o[5 + i]:o[6 + i]] for i in range(6))
    ng, mg = w[:, o[11]:o[12]], w[:, o[12]:o[13]]

    def dup(t):
        t = t.reshape(D_MODEL, NSA_GROUPS, 1, HEAD_DIM)
        return jnp.broadcast_to(t, (D_MODEL, NSA_GROUPS, 2, HEAD_DIM)).reshape(D_MODEL, NSA_GROUPS * LANES)

    ng_pad = jnp.pad(ng, ((0, 0), (0, LANES - NSA_GATE)))
    wa = jnp.concatenate([mg, rg, nq, kc, vc, ng_pad], axis=1).astype(BF16)
    wb = jnp.concatenate([rq, rk, rv, dup(ksl), dup(kwi), vsl, vwi], axis=1).astype(BF16)
    qk = qk_norm_w[l]
    tile = lambda v: jnp.tile(v, LANES // HEAD_DIM)[None, :]
    w1 = cmp_w1[l].astype(BF16)
    half = CMP_STRIDE * HEAD_DIM
    w1r = w1.reshape(2, 1, CMP_LEN, 1, HEAD_DIM, CMP_HIDDEN)
    gsel = jnp.eye(NSA_GROUPS, dtype=BF16).reshape(1, NSA_GROUPS, 1, NSA_GROUPS, 1, 1)
    w1p = (w1r * gsel).reshape(2, NSA_GROUPS, CMP_LEN, LANES, CMP_HIDDEN)
    w2 = jnp.concatenate([cmp_w2[l], cmp_w2[l]], axis=-1).astype(BF16)
    pe = jnp.broadcast_to(cmp_pos[l].reshape(2, 1, CMP_LEN * HEAD_DIM), (2, 8, CMP_LEN * HEAD_DIM))
    return dict(
        nw1=norm1_w[l][None, :], wa=wa, wb=wb, gnw=ret_gn_w[l][None, :],
        qw=tile(qk[0]), kcw=tile(qk[1]), ksw=tile(qk[2]), kww=tile(qk[3]),
        w1a=w1[:, :half], w1b=w1[:, half:], w1p=w1p, w2=w2, pe=pe,
        wr=w_o_ret[l].astype(BF16), wn=w_o_nsa[l].astype(BF16), wo=w_out[l].astype(BF16),
        nw2=norm2_w[l][None, :], rwt=router_w[l].T, rb=router_b[l][:, None],
        bgu=b_gate_up[l][:, None, :], bd=b_down[l][:, None, :],
    )


def kernel(x, norm1_w, w_in, ret_gn_w, qk_norm_w, cmp_pos, cmp_w1, cmp_w2, w_o_ret, w_o_nsa, w_out, norm2_w,
           router_w, router_b, w_gate_up, b_gate_up, w_down, b_down):
    b, s, _ = x.shape
    depth = w_in.shape[0]
    n = b * s
    n_slc = s // SLC_BLOCK
    assert s % TC_RET == 0 and s % TK == 0 and s >= WIN_TILES * LANES and n % TM_PROJ == 0
    assert n_slc <= LANES
    assert s // CMP_STRIDE <= TK

    pos = jnp.arange(s)
    inv = RET_THETA ** (-jnp.arange(0, RET_DK, 2, dtype=F32) / RET_DK)
    ang = pos.astype(F32)[:, None] * inv[None, :]
    ret_cos = jnp.concatenate([jnp.cos(ang), jnp.cos(ang)], axis=1)
    ret_sin = jnp.concatenate([-jnp.sin(ang), jnp.sin(ang)], axis=1)
    dec, zeta_b, xi_b, chunk_decay = _ret_consts()
    ret_tabs = (ret_cos, ret_sin, jnp.asarray(dec), jnp.asarray(zeta_b), jnp.asarray(xi_b), chunk_decay)
    tok_tabs = _rope_half_tables(pos, ROPE_DIM, ROPE_THETA, HEAD_DIM)
    n_cmp = s // CMP_STRIDE
    cmp_tabs = _rope_half_tables(jnp.arange(n_cmp) * CMP_STRIDE + CMP_LEN - 1, ROPE_DIM, ROPE_THETA, HEAD_DIM)
    li = np.arange(LANES)
    bd = jnp.asarray((li[:, None] // HEAD_DIM == li[None, :] // HEAD_DIM).astype(np.float32), BF16)
    ci, sj = np.arange(n_cmp)[None, :], np.arange(LANES)[:, None]
    ovt = ((ci * CMP_STRIDE < (sj + 1) * SLC_BLOCK) & (ci * CMP_STRIDE + CMP_LEN > sj * SLC_BLOCK)
           & (ci < n_cmp - CMP_LEN // CMP_STRIDE + 1) & (sj < n_slc))
    ovt = jnp.asarray(ovt.astype(np.float32), BF16)
    ti = np.arange(TM_PROJ)
    tri = jnp.asarray((ti[:, None] < ti[None, :]).astype(np.float32), BF16)

    x2 = x.reshape(n, D_MODEL)
    for l in range(depth):
        p = _layer_weights(l, w_in, norm1_w, ret_gn_w, qk_norm_w, cmp_pos, cmp_w1, cmp_w2, w_o_ret, w_o_nsa,
                           w_out, norm2_w, router_w, router_b, w_gate_up, b_gate_up, w_down, b_down)
        pa, pb = _inproj(x2, p["nw1"], p["wa"], p["wb"])
        ret = _retention(pa, pb, p["gnw"], ret_tabs, b, s)
        q, ksa, kw, vst, vwt, gt = _nsa_prep(pa, pb, tok_tabs, p["qw"], p["ksw"], p["kww"], bd, b, s)
        kcmp, vcmp_t = (_compress(pa, p["w1p"][kv], p["w1a"][kv], p["w1b"][kv], p["pe"][kv], p["w2"][kv],
                                  cmp_tabs, p["kcw"], kv == 0, b, s) for kv in range(2))
        att = _attention(q, kcmp, vcmp_t, ksa, vst, kw, vwt, gt, ovt, b, s)
        x2 = _outproj(x2, ret, att.reshape(n, NSA_Q), pa, p["wr"], p["wn"], p["wo"])
        h2, e_t, w_t, r_t, cnt = _router(x2, p["nw2"], p["rwt"], p["rb"], tri)
        dest_b, blk_e, nvalid, zinfo, rows = _route(e_t, r_t, cnt, n)
        xbuf = _dispatch(zinfo, dest_b, h2, rows)
        ybuf = _experts(blk_e, nvalid, xbuf, w_gate_up, p["bgu"], w_down, p["bd"], l)
        x2 = _combine(dest_b, x2, w_t[:TOP_K].T, ybuf)
    return x2.reshape(b, s, D_MODEL)
```

```python
import functools

import numpy as np
import jax
import jax.numpy as jnp
from jax import lax
from jax.experimental import pallas as pl
from jax.experimental.pallas import tpu as pltpu

F32 = jnp.float32
BF16 = jnp.bfloat16
I32 = jnp.int32

D_MODEL = 1024
RET_HEADS, RET_DK, RET_DV, RET_CHUNK, RET_THETA = 4, 128, 256, 128, 10000.0
NSA_HEADS, NSA_GROUPS, HEAD_DIM = 16, 2, 64
GROUP_HEADS = NSA_HEADS // NSA_GROUPS
ROPE_DIM, ROPE_THETA = HEAD_DIM // 4, 500000.0
CMP_LEN, CMP_STRIDE, CMP_HIDDEN = 32, 16, 4 * HEAD_DIM
SLC_BLOCK, SLC_TOPK, WINDOW = 64, 16, 512
N_EXPERTS, TOP_K, D_FF = 32, 4, D_MODEL
SWIGLU_LIMIT, SWIGLU_ALPHA = 7.0, 1.702
NORM_EPS = 1e-6
RET_QK, RET_V = RET_HEADS * RET_DK, RET_HEADS * RET_DV
NSA_Q, NSA_KV, NSA_GATE = NSA_HEADS * HEAD_DIM, NSA_GROUPS * HEAD_DIM, NSA_HEADS * 3

LANES = 128
NEG = -1e30
LOG2E = 1.4426950408889634
VMEM_LIMIT = 56 * 1024 * 1024

TM_PROJ = 256
TC_RET = 1024
TM_PREP = 512
TQ = 128
TK = 512
WIN_TILES = WINDOW // LANES + 1
VROWS = HEAD_DIM + 16
TM_MOE = 512
TM_CMB = 256
SUB = 8


def _tiled(nrows):
    return (nrows * SUB, LANES)

PA_MG, PA_RG, PA_NQ, PA_C, PA_NG = 0, 2048, 3072, 4096, 4352
PA_W = 4480
PB_RQ, PB_RK, PB_RV, PB_KS, PB_KW, PB_VS, PB_VW = 0, 512, 1024, 2048, 2304, 2560, 2688
PB_W = 2816


def _cparams(sem):
    return pltpu.CompilerParams(dimension_semantics=sem, vmem_limit_bytes=VMEM_LIMIT)


def _dot(a, b):
    return jnp.dot(a, b, preferred_element_type=F32)


def _dot_nt(a, b):
    return lax.dot_general(a, b, (((1,), (1,)), ((), ())), preferred_element_type=F32)


def _split_bf16(x):
    hi = x.astype(BF16)
    lo = (x - hi.astype(F32)).astype(BF16)
    return hi, lo


def _inproj_kernel(x_ref, nw_ref, wa_ref, wb_ref, pa_ref, pb_ref):
    x = x_ref[...]
    h = x * lax.rsqrt(jnp.mean(x * x, axis=-1, keepdims=True) + NORM_EPS) * nw_ref[...]
    h = h.astype(BF16)
    for c in range(0, PA_W, 512):
        w = min(512, PA_W - c)
        pa_ref[:, c:c + w] = _dot(h, wa_ref[:, c:c + w])
    for c in range(0, PB_W, 512):
        w = min(512, PB_W - c)
        pb_ref[:, c:c + w] = _dot(h, wb_ref[:, c:c + w]).astype(BF16)


def _inproj(x2, nw, wa, wb):
    n = x2.shape[0]
    return pl.pallas_call(
        _inproj_kernel,
        out_shape=(jax.ShapeDtypeStruct((n, PA_W), F32), jax.ShapeDtypeStruct((n, PB_W), BF16)),
        grid=(n // TM_PROJ,),
        in_specs=[
            pl.BlockSpec((TM_PROJ, D_MODEL), lambda i: (i, 0)),
            pl.BlockSpec((1, D_MODEL), lambda i: (0, 0)),
            pl.BlockSpec((D_MODEL, PA_W), lambda i: (0, 0)),
            pl.BlockSpec((D_MODEL, PB_W), lambda i: (0, 0)),
        ],
        out_specs=(pl.BlockSpec((TM_PROJ, PA_W), lambda i: (i, 0)),
                   pl.BlockSpec((TM_PROJ, PB_W), lambda i: (i, 0))),
        compiler_params=_cparams(("parallel",)),
        name="inproj",
    )(x2, nw, wa, wb)


def _outproj_kernel(x_ref, ret_ref, att_ref, mg_ref, wr_ref, wn_ref, wo_ref, o_ref):
    mg = mg_ref[...]
    a = jax.nn.sigmoid(mg[:, :D_MODEL]) * _dot(ret_ref[...], wr_ref[...])
    b = jax.nn.sigmoid(mg[:, D_MODEL:]) * _dot(att_ref[...], wn_ref[...])
    o_ref[...] = x_ref[...] + _dot((a + b).astype(BF16), wo_ref[...])


def _outproj(x2, ret, att, pa, wr, wn, wo):
    n = x2.shape[0]
    row = lambda i: (i, 0)
    full = lambda i: (0, 0)
    return pl.pallas_call(
        _outproj_kernel,
        out_shape=jax.ShapeDtypeStruct((n, D_MODEL), F32),
        grid=(n // TM_PROJ,),
        in_specs=[
            pl.BlockSpec((TM_PROJ, D_MODEL), row),
            pl.BlockSpec((TM_PROJ, RET_V), row),
            pl.BlockSpec((TM_PROJ, NSA_Q), row),
            pl.BlockSpec((TM_PROJ, 2 * D_MODEL), lambda i: (i, PA_MG // (2 * D_MODEL))),
            pl.BlockSpec((RET_V, D_MODEL), full),
            pl.BlockSpec((NSA_Q, D_MODEL), full),
            pl.BlockSpec((D_MODEL, D_MODEL), full),
        ],
        out_specs=pl.BlockSpec((TM_PROJ, D_MODEL), row),
        compiler_params=_cparams(("parallel",)),
        name="outproj",
    )(x2, ret, att, pa, wr, wn, wo)


def _ret_consts():
    h = np.arange(RET_HEADS, dtype=np.float32)
    log_g = np.log1p(-np.exp2(-5.0 - h)).astype(np.float32)
    n = np.arange(RET_CHUNK, dtype=np.float32)
    diff = n[:, None] - n[None, :]
    decay_in = np.where(diff >= 0, np.exp(log_g[:, None, None] * np.maximum(diff, 0.0)), 0.0).astype(np.float32)
    zeta = np.exp(log_g[:, None] * (RET_CHUNK - 1 - n)[None, :]).astype(np.float32)
    xi = np.exp(log_g[:, None] * (n + 1)[None, :]).astype(np.float32)
    chunk_decay = np.exp(log_g * RET_CHUNK).astype(np.float32)
    zeta_b = np.broadcast_to(zeta[:, :, None], (RET_HEADS, RET_CHUNK, RET_DK)).copy()
    xi_b = np.broadcast_to(xi[:, :, None], (RET_HEADS, RET_CHUNK, RET_DV)).copy()
    return decay_in, zeta_b, xi_b, [float(v) for v in chunk_decay]


def _rope_half_tables(pos, rot_dim, theta, period):
    inv = theta ** (-jnp.arange(0, rot_dim, 2, dtype=F32) / rot_dim)
    ang = pos.astype(F32)[:, None] * inv[None, :]
    c, s = jnp.cos(ang), jnp.sin(ang)
    half = rot_dim // 2
    p = pos.shape[0]
    one = jnp.ones((p, period - rot_dim), F32)
    zero_h = jnp.zeros((p, half), F32)
    zero_r = jnp.zeros((p, period - rot_dim), F32)
    cos_t = jnp.concatenate([c, c, one], axis=1)
    sm = jnp.concatenate([-s, zero_h, zero_r], axis=1)
    sp = jnp.concatenate([zero_h, s, zero_r], axis=1)
    rep = LANES // period
    return tuple(jnp.tile(t, (1, rep)) for t in (cos_t, sm, sp))


def _ret_kernel(q_ref, k_ref, v_ref, g_ref, cos_ref, sin_ref, dec_ref, zeta_ref, xi_ref, gnw_ref,
                o_ref, state_ref, *, chunk_decay):
    @pl.when(pl.program_id(1) == 0)
    def _():
        state_ref[...] = jnp.zeros_like(state_ref)

    n_chunks = TC_RET // RET_CHUNK
    for h in range(RET_HEADS):
        dec = dec_ref[h]
        zeta = zeta_ref[h]
        xi = xi_ref[h]
        gnw = gnw_ref[:, h * RET_DV:(h + 1) * RET_DV]
        for c in range(n_chunks):
            rows = slice(c * RET_CHUNK, (c + 1) * RET_CHUNK)
            cs = cos_ref[rows, :]
            sn = sin_ref[rows, :]
            q = q_ref[rows, h * RET_DK:(h + 1) * RET_DK].astype(F32)
            k = k_ref[rows, h * RET_DK:(h + 1) * RET_DK].astype(F32)
            q = q * cs + pltpu.roll(q, RET_DK // 2, 1) * sn
            k = (k * cs + pltpu.roll(k, RET_DK // 2, 1) * sn) * (RET_DK ** -0.5)
            v = v_ref[rows, h * RET_DV:(h + 1) * RET_DV]
            qb = q.astype(BF16)
            s = _dot_nt(qb, k.astype(BF16)) * dec
            inner = _dot(s.astype(BF16), v)
            r = state_ref[h]
            cross = _dot(qb, r.astype(BF16)) * xi
            kzt = (k * zeta).T.astype(BF16)
            state_ref[h] = chunk_decay[h] * r + _dot(kzt, v)
            o = inner + cross
            mu = jnp.mean(o, axis=-1, keepdims=True)
            d = o - mu
            var = jnp.mean(d * d, axis=-1, keepdims=True)
            on = d * lax.rsqrt(var + NORM_EPS) * gnw
            g = g_ref[rows, h * RET_DV:(h + 1) * RET_DV]
            o_ref[rows, h * RET_DV:(h + 1) * RET_DV] = (g * jax.nn.sigmoid(g) * on).astype(BF16)


def _retention(pa, pb, gnw, tabs, b, s):
    cos_t, sin_t, dec, zeta_b, xi_b, chunk_decay = tabs
    nt = s // TC_RET
    kern = functools.partial(_ret_kernel, chunk_decay=chunk_decay)
    tok = lambda w, j: pl.BlockSpec((TC_RET, w), lambda bi, si, j=j: (bi * nt + si, j))
    cst3 = lambda shp: pl.BlockSpec(shp, lambda bi, si: (0, 0, 0))
    return pl.pallas_call(
        kern,
        out_shape=jax.ShapeDtypeStruct((b * s, RET_V), BF16),
        grid=(b, nt),
        in_specs=[
            tok(RET_QK, PB_RQ // RET_QK),
            tok(RET_QK, PB_RK // RET_QK),
            tok(RET_V, PB_RV // RET_V),
            tok(RET_V, PA_RG // RET_V),
            pl.BlockSpec((TC_RET, LANES), lambda bi, si: (si, 0)),
            pl.BlockSpec((TC_RET, LANES), lambda bi, si: (si, 0)),
            cst3((RET_HEADS, RET_CHUNK, RET_CHUNK)),
            cst3((RET_HEADS, RET_CHUNK, RET_DK)),
            cst3((RET_HEADS, RET_CHUNK, RET_DV)),
            pl.BlockSpec((1, RET_V), lambda bi, si: (0, 0)),
        ],
        out_specs=pl.BlockSpec((TC_RET, RET_V), lambda bi, si: (bi * nt + si, 0)),
        scratch_shapes=[pltpu.VMEM((RET_HEADS, RET_DK, RET_DV), F32)],
        compiler_params=_cparams(("parallel", "arbitrary")),
        name="retention",
    )(pb, pb, pb, pa, cos_t, sin_t, dec, zeta_b, xi_b, gnw)


def _ones_row_pad(n):
    pad_rows = 16
    return jnp.where(lax.broadcasted_iota(I32, (pad_rows, n), 0) == 0, 1.0, 0.0)


def _rope16(x, cs, sm, sp):
    half = ROPE_DIM // 2
    return x * cs + pltpu.roll(x, LANES - half, 1) * sm + pltpu.roll(x, half, 1) * sp


def _prep_kernel(nq_ref, ks_ref, kw_ref, vs_ref, vw_ref, ng_ref, cs_ref, sm_ref, sp_ref,
                 qw_ref, ksw_ref, kww_ref, bd_ref,
                 q_ref, ksa_ref, kwo_ref, vst_ref, vwt_ref, gt_ref):
    cs, sm, sp = cs_ref[...], sm_ref[...], sp_ref[...]
    bd = bd_ref[...]
    for p in range(NSA_Q // LANES):
        x = nq_ref[:, p * LANES:(p + 1) * LANES]
        hi, lo = _split_bf16(x * x)
        ms = (_dot(hi, bd) + _dot(lo, bd)) * (1.0 / HEAD_DIM)
        y = x * lax.rsqrt(ms + NORM_EPS) * qw_ref[...]
        y = _rope16(y, cs, sm, sp) * (HEAD_DIM ** -0.5 * LOG2E)
        q_ref[:, p * LANES:(p + 1) * LANES] = y.astype(BF16)
    tok = pl.program_id(1) * TM_PREP + lax.broadcasted_iota(I32, (TM_PREP, LANES), 0)
    blk = lax.shift_right_logical(tok, int(np.log2(SLC_BLOCK)))
    onehot = jnp.where(lax.broadcasted_iota(I32, (TM_PREP, LANES), 1) == blk, 1.0, 0.0).astype(BF16)
    for g in range(NSA_GROUPS):
        sl = slice(g * LANES, (g + 1) * LANES)
        x = ks_ref[:, sl].astype(F32)
        y = x * lax.rsqrt(jnp.mean(x * x, axis=-1, keepdims=True) + NORM_EPS) * ksw_ref[...]
        ksa_ref[:, 2 * g * LANES:(2 * g + 1) * LANES] = _rope16(y, cs, sm, sp).astype(BF16)
        ksa_ref[:, (2 * g + 1) * LANES:(2 * g + 2) * LANES] = onehot
        x = kw_ref[:, sl].astype(F32)
        y = x * lax.rsqrt(jnp.mean(x * x, axis=-1, keepdims=True) + NORM_EPS) * kww_ref[...]
        kwo_ref[:, sl] = _rope16(y, cs, sm, sp).astype(BF16)
    vt = vs_ref[...].astype(F32).T
    wt = vw_ref[...].astype(F32).T
    for g in range(NSA_GROUPS):
        rows = slice(g * HEAD_DIM, (g + 1) * HEAD_DIM)
        vst_ref[g, 0:HEAD_DIM, :] = vt[rows, :].astype(BF16)
        vst_ref[g, HEAD_DIM:VROWS, :] = _ones_row_pad(TM_PREP).astype(BF16)
        for r in range(TM_PREP // LANES):
            vwt_ref[g, r, 0:HEAD_DIM, :] = wt[rows, r * LANES:(r + 1) * LANES].astype(BF16)
            vwt_ref[g, r, HEAD_DIM:VROWS, :] = _ones_row_pad(LANES).astype(BF16)
    gt_ref[...] = jax.nn.sigmoid(ng_ref[...]).T


def _nsa_prep(pa, pb, tabs, qw, ksw, kww, bd, b, s):
    cs, sm, sp = tabs
    nt = s // TM_PREP
    tokb = lambda w, j: pl.BlockSpec((TM_PREP, w), lambda bi, si, j=j: (bi * nt + si, j))
    tab = pl.BlockSpec((TM_PREP, LANES), lambda bi, si: (si, 0))
    vec = pl.BlockSpec((1, LANES), lambda bi, si: (0, 0))
    kv_w = NSA_GROUPS * LANES
    out_shape = (
        jax.ShapeDtypeStruct((b, s, NSA_Q), BF16),
        jax.ShapeDtypeStruct((b, s, 2 * kv_w), BF16),
        jax.ShapeDtypeStruct((b, s, kv_w), BF16),
        jax.ShapeDtypeStruct((b, NSA_GROUPS, nt, VROWS, TM_PREP), BF16),
        jax.ShapeDtypeStruct((b, NSA_GROUPS, s // LANES, VROWS, LANES), BF16),
        jax.ShapeDtypeStruct((b, LANES, s), F32),
    )
    out_specs = (
        pl.BlockSpec((None, TM_PREP, NSA_Q), lambda bi, si: (bi, si, 0)),
        pl.BlockSpec((None, TM_PREP, 2 * kv_w), lambda bi, si: (bi, si, 0)),
        pl.BlockSpec((None, TM_PREP, kv_w), lambda bi, si: (bi, si, 0)),
        pl.BlockSpec((None, NSA_GROUPS, None, VROWS, TM_PREP), lambda bi, si: (bi, 0, si, 0, 0)),
        pl.BlockSpec((None, NSA_GROUPS, TM_PREP // LANES, VROWS, LANES), lambda bi, si: (bi, 0, si, 0, 0)),
        pl.BlockSpec((None, LANES, TM_PREP), lambda bi, si: (bi, 0, si)),
    )
    return pl.pallas_call(
        _prep_kernel,
        out_shape=out_shape,
        grid=(b, nt),
        in_specs=[
            tokb(NSA_Q, PA_NQ // NSA_Q),
            tokb(kv_w, PB_KS // kv_w), tokb(kv_w, PB_KW // kv_w),
            tokb(NSA_KV, PB_VS // NSA_KV), tokb(NSA_KV, PB_VW // NSA_KV),
            tokb(LANES, PA_NG // LANES),
            tab, tab, tab, vec, vec, vec,
            pl.BlockSpec((LANES, LANES), lambda bi, si: (0, 0)),
        ],
        out_specs=out_specs,
        compiler_params=_cparams(("parallel", "parallel")),
        name="nsa_prep",
    )(pa, pb, pb, pb, pb, pa, cs, sm, sp, qw, ksw, kww, bd)


def _compress_kernel(x_ref, w1p_ref, w1a_ref, w1b_ref, pe_ref, w2_ref, cs_ref, sm_ref, sp_ref, nw_ref, o_ref,
                     *, is_key, n):
    a = bb = None
    for j in range(CMP_STRIDE):
        xj = x_ref[pl.ds(j, n, stride=CMP_STRIDE), :].astype(BF16)
        da = _dot(xj, w1p_ref[j])
        db = _dot(xj, w1p_ref[CMP_STRIDE + j])
        a = da if a is None else a + da
        bb = db if bb is None else bb + db
    pe_hi, pe_lo = _split_bf16(pe_ref[...])
    w1 = jnp.concatenate([w1a_ref[...], w1b_ref[...]], axis=0)
    pe_term = (_dot(pe_hi, w1) + _dot(pe_lo, w1))[0:1, :]
    hid = a + pltpu.roll(bb, n - 1, 0) + pe_term
    t = hid * (0.7978845608028654 * (1.0 + 0.044715 * hid * hid))
    act = 0.5 * hid * (1.0 + jnp.tanh(t))
    y = _dot(act.astype(BF16), w2_ref[...])
    if is_key:
        y = y * lax.rsqrt(jnp.mean(y * y, axis=-1, keepdims=True) + NORM_EPS) * nw_ref[...]
        o_ref[...] = _rope16(y, cs_ref[...], sm_ref[...], sp_ref[...]).astype(BF16)
    else:
        o_ref[...] = jnp.concatenate([y.T[0:HEAD_DIM, :], _ones_row_pad(n)], axis=0).astype(BF16)


def _compress(pa, w1p, w1a, w1b, pe, w2, tabs, nw, is_key, b, s):
    g, n = NSA_GROUPS, s // CMP_STRIDE
    cs, sm, sp = tabs
    full2 = lambda shp: pl.BlockSpec(shp, lambda bi, gi: (0, 0))
    out_block = (None, None, n, LANES) if is_key else (None, None, VROWS, n)
    out_shape = (b, g, n, LANES) if is_key else (b, g, VROWS, n)
    return pl.pallas_call(
        functools.partial(_compress_kernel, is_key=is_key, n=n),
        out_shape=jax.ShapeDtypeStruct(out_shape, BF16),
        grid=(b, g),
        in_specs=[
            pl.BlockSpec((s, LANES), lambda bi, gi: (bi, PA_C // LANES + (0 if is_key else 1))),
            pl.BlockSpec((None, CMP_LEN, LANES, CMP_HIDDEN), lambda bi, gi: (gi, 0, 0, 0)),
            full2(w1a.shape), full2(w1b.shape), full2(pe.shape), full2(w2.shape),
            full2(cs.shape), full2(sm.shape), full2(sp.shape), full2(nw.shape),
        ],
        out_specs=pl.BlockSpec(out_block, lambda bi, gi: (bi, gi, 0, 0)),
        compiler_params=_cparams(("parallel", "parallel")),
        name="compress_k" if is_key else "compress_v",
    )(pa, w1p, w1a, w1b, pe, w2, cs, sm, sp, nw)


def _weighted_values(vt, e):
    acc = _dot(vt, e.astype(BF16))
    return acc[0:HEAD_DIM, :], acc[HEAD_DIM:HEAD_DIM + 1, :]


def _attn_kernel(q_ref, kc_ref, vct_ref, ksa_ref, vst_ref, kw_ref, vwt_ref, gt_ref, ovt_ref,
                 o_ref, s_ref, cm_ref, qa_ref):
    gi = pl.program_id(1)
    q0 = pl.program_id(2) * TQ
    n_cmp = kc_ref.shape[0]
    ncols = GROUP_HEADS * TQ

    low = lax.broadcasted_iota(I32, (TQ, LANES), 1) < HEAD_DIM
    zero = jnp.zeros((TQ, LANES), BF16)
    parts = []
    for h in range(GROUP_HEADS):
        slab = q_ref[:, (h // 2) * LANES:(h // 2 + 1) * LANES]
        parts.append(jnp.where(low, slab, zero) if h % 2 == 0 else jnp.where(low, zero, slab))
    qs = jnp.concatenate(parts, axis=0)
    t_row = q0 + lax.broadcasted_iota(I32, (1, TQ), 1)

    def all_heads(bias):
        return jnp.concatenate([bias] * GROUP_HEADS, axis=1)

    def produce(keys, queries, slot, keep_max):
        st = _dot_nt(keys, queries)
        s_ref[slot, 0:keys.shape[0], :] = st
        if keep_max:
            cm_ref[slot] = jnp.broadcast_to(jnp.max(st, axis=0, keepdims=True), (8, ncols))

    def consume(slot, nk, vt, carry, bias):
        m_old, a_old = carry
        st = s_ref[slot, 0:nk, :]
        if bias is None:
            cm = cm_ref[slot][0:1, :]
        else:
            st = st + all_heads(bias)
            cm = jnp.max(st, axis=0, keepdims=True)
        m_new = jnp.maximum(m_old, cm)
        alpha = jnp.exp2(m_old - m_new)
        e = jnp.exp2(st - m_new)
        return m_new, alpha * a_old + _dot(vt, e.astype(BF16))

    init = (jnp.full((1, ncols), NEG, F32), jnp.zeros((VROWS, ncols), F32))

    def finish(acc):
        return acc[0:HEAD_DIM, :] * (1.0 / acc[HEAD_DIM:HEAD_DIM + 1, :])

    w0 = jnp.maximum(q0 - WINDOW, 0)
    wt0 = lax.shift_right_logical(w0, int(np.log2(LANES)))
    n_wb = WIN_TILES * LANES - TK
    produce(kc_ref[...], qs, 1, False)
    produce(kw_ref[pl.ds(pl.multiple_of(w0, LANES), TK), :], qs, 0, False)

    cmp_end = lax.broadcasted_iota(I32, (n_cmp, 1), 0) * CMP_STRIDE + (CMP_LEN - 1)
    st = s_ref[1, 0:n_cmp, :] + all_heads(jnp.where(cmp_end <= t_row, 0.0, NEG))
    m = jnp.max(st, axis=0, keepdims=True)
    e = jnp.exp2(st - jnp.where(m > 0.5 * NEG, m, 0.0))
    oc_all, l = _weighted_values(vct_ref[...], e)
    rl = 1.0 / jnp.where(l > 0.0, l, 1.0)
    oc_all = oc_all * rl
    psum = e[:, 0:TQ] * rl[:, 0:TQ]
    for h in range(1, GROUP_HEADS):
        psum = psum + e[:, h * TQ:(h + 1) * TQ] * rl[:, h * TQ:(h + 1) * TQ]
    ph, plo = _split_bf16(psum)
    imp_t = _dot(ovt_ref[...], ph) + _dot(ovt_ref[...], plo)

    def wbias(start, nk):
        kpos = start + lax.broadcasted_iota(I32, (nk, 1), 0)
        return jnp.where((kpos <= t_row) & (kpos > t_row - WINDOW), 0.0, NEG)

    produce(kw_ref[pl.ds(pl.multiple_of(w0 + TK, LANES), n_wb), :], qs, 1, False)
    vwa = jnp.concatenate([vwt_ref[wt0 + r] for r in range(TK // LANES)], axis=1)
    carry_w = consume(0, TK, vwa, init, wbias(w0, TK))
    produce(ksa_ref[0:TK, 0:LANES], qs, 0, False)

    jrow = lax.broadcasted_iota(I32, (LANES, TQ), 0)
    cur = lax.shift_right_logical(q0 + lax.broadcasted_iota(I32, (LANES, TQ), 1), int(np.log2(SLC_BLOCK)))
    forced = (jrow == 0) | (jrow == cur) | (jrow == cur - 1)
    valid = jrow <= cur
    bias = jnp.where(forced & valid, 0.0, NEG)
    val = jnp.where(valid & jnp.logical_not(forced), imp_t, -jnp.inf)
    for _ in range(SLC_TOPK - 3):
        mx = jnp.max(val, axis=0, keepdims=True)
        first = jnp.min(jnp.where(val == mx, jrow, LANES), axis=0, keepdims=True)
        pick = jrow == first
        bias = jnp.where(pick & valid, 0.0, bias)
        val = jnp.where(pick, -jnp.inf, val)

    vwb = jnp.concatenate([vwt_ref[wt0 + TK // LANES + r] for r in range(n_wb // LANES)], axis=1)
    _, a_w = consume(1, n_wb, vwb, carry_w, wbias(w0 + TK, n_wb))
    ow_all = finish(a_w)

    selb = bias.T.astype(BF16)
    qa_ref[...] = jnp.concatenate([qs, jnp.concatenate([selb] * GROUP_HEADS, axis=0)], axis=1)

    def sel_scores(ti, slot):
        produce(ksa_ref[pl.ds(pl.multiple_of(ti * TK, TK), TK), :], qa_ref[...], slot, True)

    def sel_consume(ti, slot, carry, causal):
        cb = None
        if causal:
            cb = jnp.where(ti * TK + lax.broadcasted_iota(I32, (TK, 1), 0) <= t_row, 0.0, NEG)
        return consume(slot, TK, vst_ref[ti], carry, cb)

    n_full = lax.shift_right_logical(q0, int(np.log2(TK)))
    rows_per_tile = TK // SLC_BLOCK
    bias0 = jnp.concatenate([jnp.broadcast_to(bias[r:r + 1, :], (SLC_BLOCK, TQ)) for r in range(rows_per_tile)],
                            axis=0)
    st0 = s_ref[0] + all_heads(bias0)
    s_ref[0] = st0
    cm_ref[0] = jnp.broadcast_to(jnp.max(st0, axis=0, keepdims=True), (8, ncols))

    def two_tiles(j, carry):
        sel_scores(2 * j + 1, 1)
        carry = sel_consume(2 * j, 0, carry, False)
        sel_scores(2 * j + 2, 0)
        return sel_consume(2 * j + 1, 1, carry, False)

    n_pairs = lax.shift_right_logical(n_full, 1)
    n_quads = lax.shift_right_logical(n_pairs, 1)
    carry = lax.fori_loop(0, n_quads, lambda j, c: two_tiles(2 * j + 1, two_tiles(2 * j, c)), init)
    carry = lax.fori_loop(2 * n_quads, n_pairs, two_tiles, carry)

    def odd_tail(c):
        sel_scores(n_full, 1)
        return sel_consume(n_full, 1, sel_consume(n_full - 1, 0, c, False), True)

    _, a_s = lax.cond((n_full & 1) == 1, odd_tail, lambda c: sel_consume(n_full, 0, c, True), carry)
    os_all = finish(a_s)

    for p in range(GROUP_HEADS // 2):
        halves = []
        for h in (2 * p, 2 * p + 1):
            c = slice(h * TQ, (h + 1) * TQ)
            gbase = (gi * GROUP_HEADS + h) * 3
            halves.append(gt_ref[pl.ds(gbase, 1), :] * oc_all[:, c]
                          + gt_ref[pl.ds(gbase + 1, 1), :] * os_all[:, c]
                          + gt_ref[pl.ds(gbase + 2, 1), :] * ow_all[:, c])
        o_ref[:, p * LANES:(p + 1) * LANES] = jnp.concatenate(halves, axis=0).T.astype(BF16)


def _attention(q, kc, vct, ksa, vst, kw, vwt, gt, ovt, b, s):
    nq = s // TQ
    n_cmp = kc.shape[2]
    gw = GROUP_HEADS * HEAD_DIM
    ncols = GROUP_HEADS * TQ
    return pl.pallas_call(
        _attn_kernel,
        out_shape=jax.ShapeDtypeStruct((b, s, NSA_Q), BF16),
        grid=(b, NSA_GROUPS, nq),
        in_specs=[
            pl.BlockSpec((None, TQ, gw), lambda bi, gi, qi: (bi, qi, gi)),
            pl.BlockSpec((None, None, n_cmp, LANES), lambda bi, gi, qi: (bi, gi, 0, 0)),
            pl.BlockSpec((None, None, VROWS, n_cmp), lambda bi, gi, qi: (bi, gi, 0, 0)),
            pl.BlockSpec((None, s, 2 * LANES), lambda bi, gi, qi: (bi, 0, gi)),
            pl.BlockSpec((None, None, s // TK, VROWS, TK), lambda bi, gi, qi: (bi, gi, 0, 0, 0)),
            pl.BlockSpec((None, s, LANES), lambda bi, gi, qi: (bi, 0, gi)),
            pl.BlockSpec((None, None, s // LANES, VROWS, LANES), lambda bi, gi, qi: (bi, gi, 0, 0, 0)),
            pl.BlockSpec((None, LANES, TQ), lambda bi, gi, qi: (bi, 0, qi)),
            pl.BlockSpec((LANES, n_cmp), lambda bi, gi, qi: (0, 0)),
        ],
        out_specs=pl.BlockSpec((None, TQ, gw), lambda bi, gi, qi: (bi, qi, gi)),
        scratch_shapes=[
            pltpu.VMEM((2, TK, ncols), F32),
            pltpu.VMEM((2, 8, ncols), F32),
            pltpu.VMEM((ncols, 2 * LANES), BF16),
        ],
        compiler_params=_cparams(("parallel", "parallel", "arbitrary")),
        name="nsa_attention",
    )(q, kc, vct, ksa, vst, kw, vwt, gt, ovt)


def _outproj_router_kernel(x_ref, ret_ref, att_ref, mg_ref, wr_ref, wn_ref, wo_ref,
                           nw_ref, wt_ref, b_ref, tri_ref,
                           o_ref, h_ref, e_ref, w_ref, r_ref, cnt_ref, base_ref):
    mg = mg_ref[...]
    a = jax.nn.sigmoid(mg[:, :D_MODEL]) * _dot(ret_ref[...], wr_ref[...])
    b = jax.nn.sigmoid(mg[:, D_MODEL:]) * _dot(att_ref[...], wn_ref[...])
    x = x_ref[...] + _dot((a + b).astype(BF16), wo_ref[...])
    o_ref[...] = x
    _route_tile(x, nw_ref, wt_ref, b_ref, tri_ref, h_ref, e_ref, w_ref, r_ref, cnt_ref, base_ref)


def _outproj_router(x2, ret, att, pa, wr, wn, wo, nw, wt, bcol, tri):
    n = x2.shape[0]
    row = lambda i: (i, 0)
    full = lambda i: (0, 0)
    small = pl.BlockSpec((8, TM_PROJ), lambda i: (0, i))
    return pl.pallas_call(
        _outproj_router_kernel,
        out_shape=(jax.ShapeDtypeStruct((n, D_MODEL), F32),
                   jax.ShapeDtypeStruct(_tiled(n), F32),
                   jax.ShapeDtypeStruct((8, n), I32),
                   jax.ShapeDtypeStruct((8, n), F32),
                   jax.ShapeDtypeStruct((8, n), I32),
                   jax.ShapeDtypeStruct((N_EXPERTS, LANES), F32)),
        grid=(n // TM_PROJ,),
        in_specs=[
            pl.BlockSpec((TM_PROJ, D_MODEL), row),
            pl.BlockSpec((TM_PROJ, RET_V), row),
            pl.BlockSpec((TM_PROJ, NSA_Q), row),
            pl.BlockSpec((TM_PROJ, 2 * D_MODEL), lambda i: (i, PA_MG // (2 * D_MODEL))),
            pl.BlockSpec((RET_V, D_MODEL), full),
            pl.BlockSpec((NSA_Q, D_MODEL), full),
            pl.BlockSpec((D_MODEL, D_MODEL), full),
            pl.BlockSpec((1, D_MODEL), full),
            pl.BlockSpec((N_EXPERTS, D_MODEL), full),
            pl.BlockSpec((N_EXPERTS, 1), full),
            pl.BlockSpec((TM_PROJ, TM_PROJ), full),
        ],
        out_specs=(pl.BlockSpec((TM_PROJ, D_MODEL), row), pl.BlockSpec(_tiled(TM_PROJ), row), small, small, small,
                   pl.BlockSpec((N_EXPERTS, LANES), full)),
        scratch_shapes=[pltpu.VMEM((N_EXPERTS, LANES), F32)],
        compiler_params=_cparams(("arbitrary",)),
        name="outproj_router",
    )(x2, ret, att, pa, wr, wn, wo, nw, wt, bcol, tri)


def _router_kernel(x_ref, nw_ref, wt_ref, b_ref, tri_ref, h_ref, e_ref, w_ref, r_ref, cnt_ref, base_ref):
    _route_tile(x_ref[...], nw_ref, wt_ref, b_ref, tri_ref, h_ref, e_ref, w_ref, r_ref, cnt_ref, base_ref)


def _route_tile(x, nw_ref, wt_ref, b_ref, tri_ref, h_ref, e_ref, w_ref, r_ref, cnt_ref, base_ref):
    @pl.when(pl.program_id(0) == 0)
    def _():
        base_ref[...] = jnp.zeros_like(base_ref)

    h = x * lax.rsqrt(jnp.mean(x * x, axis=-1, keepdims=True) + NORM_EPS) * nw_ref[...]
    _store_row_tiles(h_ref, h)
    h_hi, h_lo = _split_bf16(h)
    w_hi, w_lo = _split_bf16(wt_ref[...])
    lg = _dot_nt(w_hi, h_hi) + _dot_nt(w_hi, h_lo) + _dot_nt(w_lo, h_hi) + b_ref[...]
    erow = lax.broadcasted_iota(I32, lg.shape, 0)
    vals, hots = [], []
    for _ in range(TOP_K):
        mx = jnp.max(lg, axis=0, keepdims=True)
        first = jnp.min(jnp.where(lg == mx, erow, N_EXPERTS), axis=0, keepdims=True)
        hot = erow == first
        vals.append(mx)
        hots.append(hot)
        lg = jnp.where(hot, -jnp.inf, lg)
    ex = [jnp.exp(v - vals[0]) for v in vals]
    den = ex[0] + ex[1] + ex[2] + ex[3]
    onehots = [h.astype(F32) for h in hots]
    cnt = onehots[0] + onehots[1] + onehots[2] + onehots[3]
    pref = _dot(cnt.astype(BF16), tri_ref[...]) + base_ref[:, 0:1]
    erow_f = erow.astype(F32)
    idxs = [jnp.sum(oh * erow_f, axis=0, keepdims=True) for oh in onehots]
    ranks = [jnp.sum(oh * pref, axis=0, keepdims=True) for oh in onehots]
    base_ref[...] = base_ref[...] + jnp.sum(cnt, axis=1, keepdims=True)
    cnt_ref[...] = base_ref[...]
    pad_f = [jnp.zeros_like(den)] * (8 - TOP_K)
    e_ref[...] = jnp.concatenate(idxs + pad_f, axis=0).astype(I32)
    r_ref[...] = jnp.concatenate(ranks + pad_f, axis=0).astype(I32)
    w_ref[...] = jnp.concatenate([e / den for e in ex] + pad_f, axis=0)


def _router(x2, nw, wt, bcol, tri):
    n = x2.shape[0]
    small = pl.BlockSpec((8, TM_PROJ), lambda i: (0, i))
    return pl.pallas_call(
        _router_kernel,
        out_shape=(jax.ShapeDtypeStruct(_tiled(n), F32),
                   jax.ShapeDtypeStruct((8, n), I32),
                   jax.ShapeDtypeStruct((8, n), F32),
                   jax.ShapeDtypeStruct((8, n), I32),
                   jax.ShapeDtypeStruct((N_EXPERTS, LANES), F32)),
        grid=(n // TM_PROJ,),
        in_specs=[
            pl.BlockSpec((TM_PROJ, D_MODEL), lambda i: (i, 0)),
            pl.BlockSpec((1, D_MODEL), lambda i: (0, 0)),
            pl.BlockSpec((N_EXPERTS, D_MODEL), lambda i: (0, 0)),
            pl.BlockSpec((N_EXPERTS, 1), lambda i: (0, 0)),
            pl.BlockSpec((TM_PROJ, TM_PROJ), lambda i: (0, 0)),
        ],
        out_specs=(pl.BlockSpec(_tiled(TM_PROJ), lambda i: (i, 0)), small, small, small,
                   pl.BlockSpec((N_EXPERTS, LANES), lambda i: (0, 0))),
        scratch_shapes=[pltpu.VMEM((N_EXPERTS, LANES), F32)],
        compiler_params=_cparams(("arbitrary",)),
        name="router",
    )(x2, nw, wt, bcol, tri)


def _row_copy(src, src_row, dst, dst_row, sem):
    tile = lambda ref, r: ref.at[pl.ds(pl.multiple_of(r * SUB, SUB), SUB), :]
    return pltpu.make_async_copy(tile(src, src_row), tile(dst, dst_row), sem)


def _issue_rows(nrow, copy_of):
    def body(g, c):
        for j in range(8):
            copy_of(g * 8 + j).start(priority=j % 2)
        return c

    lax.fori_loop(0, nrow // 8, body, 0)


def _wait_rows(nrow, copy_of):
    def body(g, c):
        for j in range(8):
            copy_of(0).wait()
        return c

    lax.fori_loop(0, nrow // 8, body, 0)


def _store_row_tiles(ref, x):
    for c in range(SUB):
        ref[pl.ds(c, x.shape[0], stride=SUB), :] = x[:, c * LANES:(c + 1) * LANES]


def _load_row_tiles(ref, first_row, nrows):
    return jnp.concatenate([ref[pl.ds(first_row * SUB + c, nrows, stride=SUB), :] for c in range(SUB)], axis=1)


def _dispatch_kernel(zinfo_ref, dest_ref, h_ref, x_hbm, zbuf, sem, zsem, *, n_blocks):
    def zero_block(row):
        return pltpu.make_async_copy(
            zbuf, x_hbm.at[pl.ds(pl.multiple_of(row * SUB, TM_MOE * SUB), TM_MOE * SUB), :], zsem)

    @pl.when(pl.program_id(0) == 0)
    def _():
        zbuf[...] = jnp.zeros_like(zbuf)
        for e in range(N_EXPERTS):
            zero_block(zinfo_ref[e]).start()
        for e in range(N_EXPERTS):
            zero_block(0).wait()
        nvalid = zinfo_ref[N_EXPERTS]

        def ztail(bk, c):
            zero_block(bk * TM_MOE).start()
            return c

        lax.fori_loop(nvalid, n_blocks, ztail, 0)

        def zwait(bk, c):
            zero_block(0).wait()
            return c

        lax.fori_loop(nvalid, n_blocks, zwait, 0)

    nrow = TOP_K * TM_CMB
    _issue_rows(nrow, lambda r: _row_copy(h_ref, r & (TM_CMB - 1), x_hbm, dest_ref[0, 0, r], sem))
    _wait_rows(nrow, lambda r: _row_copy(h_ref, 0, x_hbm, 0, sem))


def _dispatch(zinfo, dest_b, h3, rows):
    n = h3.shape[0] // SUB
    grid_spec = pltpu.PrefetchScalarGridSpec(
        num_scalar_prefetch=1,
        grid=(n // TM_CMB,),
        in_specs=[
            pl.BlockSpec((1, 1, TOP_K * TM_CMB), lambda i, z: (i, 0, 0), memory_space=pltpu.SMEM),
            pl.BlockSpec(_tiled(TM_CMB), lambda i, z: (i, 0)),
        ],
        out_specs=pl.BlockSpec(memory_space=pl.ANY),
        scratch_shapes=[pltpu.VMEM(_tiled(TM_MOE), F32), pltpu.SemaphoreType.DMA(()),
                        pltpu.SemaphoreType.DMA(())],
    )
    return pl.pallas_call(
        functools.partial(_dispatch_kernel, n_blocks=rows // TM_MOE),
        out_shape=jax.ShapeDtypeStruct(_tiled(rows), F32),
        grid_spec=grid_spec,
        compiler_params=_cparams(("arbitrary",)),
        name="dispatch",
    )(zinfo, dest_b, h3)


def _expert_kernel(blk_e_ref, nvalid_ref, x_ref, wgu_ref, bgu_ref, wd_ref, bd_ref, y_ref, wgu_bf, wd_bf):
    i = pl.program_id(0)
    used = i < nvalid_ref[0]

    @pl.when(jnp.logical_not(used))
    def _():
        y_ref[...] = jnp.zeros_like(y_ref)

    @pl.when(used & ((i == 0) | (blk_e_ref[i] != blk_e_ref[jnp.maximum(i - 1, 0)])))
    def _():
        wgu_bf[...] = wgu_ref[...].astype(BF16)
        wd_bf[...] = wd_ref[...].astype(BF16)

    @pl.when(used)
    def _():
        gu = _dot(_load_row_tiles(x_ref, 0, TM_MOE).astype(BF16), wgu_bf[...]) + bgu_ref[...]
        gate = jnp.minimum(gu[:, :D_FF], SWIGLU_LIMIT)
        up = jnp.clip(gu[:, D_FF:], -SWIGLU_LIMIT, SWIGLU_LIMIT)
        glu = gate * jax.nn.sigmoid(gate * SWIGLU_ALPHA)
        _store_row_tiles(y_ref, _dot(((up + 1.0) * glu).astype(BF16), wd_bf[...]) + bd_ref[...])


def _experts(blk_e, nvalid, xbuf, wgu, bgu, wd, bd, layer):
    n_blocks = blk_e.shape[0]
    wblk = lambda i, be, nv: (be[i], 0, 0)
    wblk4 = lambda i, be, nv: (layer, be[i], 0, 0)
    grid_spec = pltpu.PrefetchScalarGridSpec(
        num_scalar_prefetch=2,
        grid=(n_blocks,),
        in_specs=[
            pl.BlockSpec(_tiled(TM_MOE), lambda i, be, nv: (jnp.minimum(i, nv[0] - 1), 0)),
            pl.BlockSpec((None, None, D_MODEL, 2 * D_FF), wblk4),
            pl.BlockSpec((None, 1, 2 * D_FF), wblk),
            pl.BlockSpec((None, None, D_FF, D_MODEL), wblk4),
            pl.BlockSpec((None, 1, D_MODEL), wblk),
        ],
        out_specs=pl.BlockSpec(_tiled(TM_MOE), lambda i, be, nv: (i, 0)),
        scratch_shapes=[pltpu.VMEM((D_MODEL, 2 * D_FF), BF16), pltpu.VMEM((D_FF, D_MODEL), BF16)],
    )
    return pl.pallas_call(
        _expert_kernel,
        out_shape=jax.ShapeDtypeStruct(_tiled(n_blocks * TM_MOE), F32),
        grid_spec=grid_spec,
        compiler_params=_cparams(("arbitrary",)),
        name="experts",
    )(blk_e, nvalid, xbuf, wgu, bgu, wd, bd)


def _combine_kernel(dcur_ref, dnext_ref, x_ref, w_ref, y_hbm, o_ref, ybuf, sem):
    i = pl.program_id(0)
    nrow = TOP_K * TM_CMB
    slot = i & 1

    def issue(d_ref, s):
        _issue_rows(nrow, lambda r: _row_copy(y_hbm, d_ref[0, 0, r], ybuf.at[s], r, sem.at[s]))

    @pl.when(i == 0)
    def _():
        issue(dcur_ref, 0)

    @pl.when(i + 1 < pl.num_programs(0))
    def _():
        issue(dnext_ref, 1 - slot)

    _wait_rows(nrow, lambda r: _row_copy(y_hbm, 0, ybuf.at[slot], 0, sem.at[slot]))
    acc = x_ref[...]
    w = w_ref[...]
    for k in range(TOP_K):
        acc = acc + w[:, k:k + 1] * _load_row_tiles(ybuf.at[slot], k * TM_CMB, TM_CMB)
    o_ref[...] = acc


def _combine(dest_b, x2, wts, ybuf):
    n = x2.shape[0]
    nt = n // TM_CMB
    return pl.pallas_call(
        _combine_kernel,
        out_shape=jax.ShapeDtypeStruct((n, D_MODEL), F32),
        grid=(nt,),
        in_specs=[
            pl.BlockSpec((1, 1, TOP_K * TM_CMB), lambda i: (i, 0, 0), memory_space=pltpu.SMEM),
            pl.BlockSpec((1, 1, TOP_K * TM_CMB), lambda i: (jnp.minimum(i + 1, nt - 1), 0, 0),
                         memory_space=pltpu.SMEM),
            pl.BlockSpec((TM_CMB, D_MODEL), lambda i: (i, 0)),
            pl.BlockSpec((TM_CMB, TOP_K), lambda i: (i, 0)),
            pl.BlockSpec(memory_space=pl.ANY),
        ],
        out_specs=pl.BlockSpec((TM_CMB, D_MODEL), lambda i: (i, 0)),
        scratch_shapes=[pltpu.VMEM((2,) + _tiled(TOP_K * TM_CMB), F32), pltpu.SemaphoreType.DMA((2,))],
        compiler_params=_cparams(("arbitrary",)),
        name="combine",
    )(dest_b, dest_b, x2, wts, ybuf)


def _route(e_t, r_t, cnt, n):
    counts = cnt[:, 0].astype(I32)
    padded = (counts + TM_MOE - 1) // TM_MOE * TM_MOE
    pend = jnp.cumsum(padded)
    pstart = pend - padded
    n_blocks = n * TOP_K // TM_MOE + N_EXPERTS
    rows = n_blocks * TM_MOE
    dest = r_t[:TOP_K]
    for e in range(N_EXPERTS):
        dest = dest + jnp.where(e_t[:TOP_K] == e, pstart[e].astype(I32), 0)
    nt = n // TM_CMB
    dest_b = dest.reshape(TOP_K, nt, TM_CMB).transpose(1, 0, 2).reshape(nt, 1, TOP_K * TM_CMB)
    blk_row = jnp.arange(n_blocks, dtype=I32)[:, None] * TM_MOE
    blk_e = jnp.minimum(jnp.sum((pend[None, :] <= blk_row).astype(I32), axis=1), N_EXPERTS - 1)
    nvalid = (pend[-1:] // TM_MOE).astype(I32)
    zrow = jnp.clip(pstart + padded - TM_MOE, 0, rows - TM_MOE).astype(I32)
    return dest_b, blk_e, nvalid, jnp.concatenate([zrow, nvalid]), rows


def _layer_weights(l, w_in, norm1_w, ret_gn_w, qk_norm_w, cmp_pos, cmp_w1, cmp_w2, w_o_ret, w_o_nsa, w_out,
                   norm2_w, router_w, router_b, w_gate_up, b_gate_up, w_down, b_down):
    w = w_in[l]
    o = np.cumsum((0, RET_QK, RET_QK, RET_V, RET_V, NSA_Q) + (NSA_KV,) * 6 + (NSA_GATE, 2 * D_MODEL))
    rq, rk, rv, rg, nq = (w[:, o[i]:o[i + 1]] for i in range(5))
    kc, vc, ksl, vsl, kwi, vwi = (w[:, o[5 + i]:o[6 + i]] for i in range(6))
    ng, mg = w[:, o[11]:o[12]], w[:, o[12]:o[13]]

    def dup(t):
        t = t.reshape(D_MODEL, NSA_GROUPS, 1, HEAD_DIM)
        return jnp.broadcast_to(t, (D_MODEL, NSA_GROUPS, 2, HEAD_DIM)).reshape(D_MODEL, NSA_GROUPS * LANES)

    ng_pad = jnp.pad(ng, ((0, 0), (0, LANES - NSA_GATE)))
    wa = jnp.concatenate([mg, rg, nq, kc, vc, ng_pad], axis=1).astype(BF16)
    wb = jnp.concatenate([rq, rk, rv, dup(ksl), dup(kwi), vsl, vwi], axis=1).astype(BF16)
    qk = qk_norm_w[l]
    tile = lambda v: jnp.tile(v, LANES // HEAD_DIM)[None, :]
    w1 = cmp_w1[l].astype(BF16)
    half = CMP_STRIDE * HEAD_DIM
    w1r = w1.reshape(2, 1, CMP_LEN, 1, HEAD_DIM, CMP_HIDDEN)
    gsel = jnp.eye(NSA_GROUPS, dtype=BF16).reshape(1, NSA_GROUPS, 1, NSA_GROUPS, 1, 1)
    w1p = (w1r * gsel).reshape(2, NSA_GROUPS, CMP_LEN, LANES, CMP_HIDDEN)
    w2 = jnp.concatenate([cmp_w2[l], cmp_w2[l]], axis=-1).astype(BF16)
    pe = jnp.broadcast_to(cmp_pos[l].reshape(2, 1, CMP_LEN * HEAD_DIM), (2, 8, CMP_LEN * HEAD_DIM))
    return dict(
        nw1=norm1_w[l][None, :], wa=wa, wb=wb, gnw=ret_gn_w[l][None, :],
        qw=tile(qk[0]), kcw=tile(qk[1]), ksw=tile(qk[2]), kww=tile(qk[3]),
        w1a=w1[:, :half], w1b=w1[:, half:], w1p=w1p, w2=w2, pe=pe,
        wr=w_o_ret[l].astype(BF16), wn=w_o_nsa[l].astype(BF16), wo=w_out[l].astype(BF16),
        nw2=norm2_w[l][None, :], rwt=router_w[l].T, rb=router_b[l][:, None],
        bgu=b_gate_up[l][:, None, :], bd=b_down[l][:, None, :],
    )


def kernel(x, norm1_w, w_in, ret_gn_w, qk_norm_w, cmp_pos, cmp_w1, cmp_w2, w_o_ret, w_o_nsa, w_out, norm2_w,
           router_w, router_b, w_gate_up, b_gate_up, w_down, b_down):
    b, s, _ = x.shape
    depth = w_in.shape[0]
    n = b * s
    n_slc = s // SLC_BLOCK
    assert s % TC_RET == 0 and s % TK == 0 and s >= WIN_TILES * LANES and n % TM_PROJ == 0
    assert n_slc <= LANES
    assert s // CMP_STRIDE <= TK

    pos = jnp.arange(s)
    inv = RET_THETA ** (-jnp.arange(0, RET_DK, 2, dtype=F32) / RET_DK)
    ang = pos.astype(F32)[:, None] * inv[None, :]
    ret_cos = jnp.concatenate([jnp.cos(ang), jnp.cos(ang)], axis=1)
    ret_sin = jnp.concatenate([-jnp.sin(ang), jnp.sin(ang)], axis=1)
    dec, zeta_b, xi_b, chunk_decay = _ret_consts()
    ret_tabs = (ret_cos, ret_sin, jnp.asarray(dec), jnp.asarray(zeta_b), jnp.asarray(xi_b), chunk_decay)
    tok_tabs = _rope_half_tables(pos, ROPE_DIM, ROPE_THETA, HEAD_DIM)
    n_cmp = s // CMP_STRIDE
    cmp_tabs = _rope_half_tables(jnp.arange(n_cmp) * CMP_STRIDE + CMP_LEN - 1, ROPE_DIM, ROPE_THETA, HEAD_DIM)
    li = np.arange(LANES)
    bd = jnp.asarray((li[:, None] // HEAD_DIM == li[None, :] // HEAD_DIM).astype(np.float32), BF16)
    ci, sj = np.arange(n_cmp)[None, :], np.arange(LANES)[:, None]
    ovt = ((ci * CMP_STRIDE < (sj + 1) * SLC_BLOCK) & (ci * CMP_STRIDE + CMP_LEN > sj * SLC_BLOCK)
           & (ci < n_cmp - CMP_LEN // CMP_STRIDE + 1) & (sj < n_slc))
    ovt = jnp.asarray(ovt.astype(np.float32), BF16)
    ti = np.arange(TM_PROJ)
    tri = jnp.asarray((ti[:, None] < ti[None, :]).astype(np.float32), BF16)

    x2 = x.reshape(n, D_MODEL)
    for l in range(depth):
        p = _layer_weights(l, w_in, norm1_w, ret_gn_w, qk_norm_w, cmp_pos, cmp_w1, cmp_w2, w_o_ret, w_o_nsa,
                           w_out, norm2_w, router_w, router_b, w_gate_up, b_gate_up, w_down, b_down)
        pa, pb = _inproj(x2, p["nw1"], p["wa"], p["wb"])
        ret = _retention(pa, pb, p["gnw"], ret_tabs, b, s)
        q, ksa, kw, vst, vwt, gt = _nsa_prep(pa, pb, tok_tabs, p["qw"], p["ksw"], p["kww"], bd, b, s)
        kcmp, vcmp_t = (_compress(pa, p["w1p"][kv], p["w1a"][kv], p["w1b"][kv], p["pe"][kv], p["w2"][kv],
                                  cmp_tabs, p["kcw"], kv == 0, b, s) for kv in range(2))
        att = _attention(q, kcmp, vcmp_t, ksa, vst, kw, vwt, gt, ovt, b, s)
        x2, h2, e_t, w_t, r_t, cnt = _outproj_router(x2, ret, att.reshape(n, NSA_Q), pa, p["wr"], p["wn"], p["wo"],
                                                     p["nw2"], p["rwt"], p["rb"], tri)
        dest_b, blk_e, nvalid, zinfo, rows = _route(e_t, r_t, cnt, n)
        xbuf = _dispatch(zinfo, dest_b, h2, rows)
        ybuf = _experts(blk_e, nvalid, xbuf, w_gate_up, p["bgu"], w_down, p["bd"], l)
        x2 = _combine(dest_b, x2, w_t[:TOP_K].T, ybuf)
    return x2.reshape(b, s, D_MODEL)
```
